```python
import math
import jax, jax.numpy as jnp
from jax import lax
import numpy as np

D_MODEL = 1024
BATCH = 1
SEQ = 16384
DEPTH = 4

GRID_W = 64
CTX_LEN = 256
ATTN_HEADS = 8
ATTN_KV_HEADS = 2
ATTN_HEAD_DIM = 64
WINDOW = 128
ATTN_BLOCK = 128
ROPE_THETA = 10000.0
GDN_HEADS = 4
GDN_HEAD_DIM = 128
GDN_W = GDN_HEADS * GDN_HEAD_DIM
GDN_CONV = 5
GDN_CHUNK = 64
POOL_SIZES = (2, 4, 8, 16)
POOL_GROUP = 128
POOL_WIDTH = POOL_GROUP * len(POOL_SIZES)
D_FF = 2816
N_MOD = 9
N_BRANCH = 3
EPS = 1e-6
IN_SIZES = (ATTN_HEADS * ATTN_HEAD_DIM, ATTN_KV_HEADS * ATTN_HEAD_DIM, ATTN_KV_HEADS * ATTN_HEAD_DIM,
            3 * GDN_W, GDN_W, 4 * GDN_HEADS, POOL_WIDTH, N_BRANCH * D_MODEL)
IN_WIDTH = sum(IN_SIZES)

kernel_name = 'hybrid_gated_parallel_diffusion_block'


def rms_norm(x, gain):
    x32 = x.astype(jnp.float32)
    y = x32 * lax.rsqrt(jnp.mean(x32 * x32, axis=-1, keepdims=True) + EPS)
    return (y * gain.astype(jnp.float32)).astype(x.dtype)


def modulate(h, shift, scale):
    return h * (1 + scale) + shift


def swiglu(h, w_in, w_out):
    gt, up = jnp.split(h @ w_in, 2, axis=-1)
    return (jax.nn.silu(gt) * up) @ w_out


def rope_axis(x, pos):
    half = x.shape[-1] // 2
    inv = ROPE_THETA ** (-jnp.arange(half, dtype=jnp.float32) / half)
    ang = pos.astype(jnp.float32)[:, None] * inv[None, :]
    cos = jnp.cos(ang)[:, None, :].astype(x.dtype)
    sin = jnp.sin(ang)[:, None, :].astype(x.dtype)
    x1, x2 = x[..., :half], x[..., half:]
    return jnp.concatenate([x1 * cos - x2 * sin, x1 * sin + x2 * cos], axis=-1)


def rope_2d(x, row, col):
    hd2 = x.shape[-1] // 2
    return jnp.concatenate([rope_axis(x[..., :hd2], row), rope_axis(x[..., hd2:], col)], axis=-1)


def window_attention(q, k, v, kc, vc, sink):
    B, L, Hkv, G, dh = q.shape
    NB = L // ATTN_BLOCK
    scale = dh ** -0.5
    qb = q.reshape(B, NB, ATTN_BLOCK, Hkv, G, dh)

    def band(t):
        tp = jnp.pad(t, ((0, 0), (ATTN_BLOCK, ATTN_BLOCK), (0, 0), (0, 0)))
        tp = tp.reshape(B, NB + 2, ATTN_BLOCK, Hkv, dh)
        return jnp.concatenate([tp[:, :-2], tp[:, 1:-1], tp[:, 2:]], axis=2)

    kw, vw = band(k), band(v)
    s_win = jnp.einsum('bnqhgd,bnkhd->bnhgqk', qb, kw, preferred_element_type=jnp.float32) * scale
    rel = jnp.arange(3 * ATTN_BLOCK)[None, :] - ATTN_BLOCK - jnp.arange(ATTN_BLOCK)[:, None]
    kpos = (jnp.arange(NB)[:, None] - 1) * ATTN_BLOCK + jnp.arange(3 * ATTN_BLOCK)[None, :]
    valid = (jnp.abs(rel) <= WINDOW)[None] & ((kpos >= 0) & (kpos < L))[:, None, :]
    s_win = jnp.where(valid[None, :, None, None], s_win, -jnp.inf)
    s_ctx = jnp.einsum('bnqhgd,bkhd->bnhgqk', qb, kc, preferred_element_type=jnp.float32) * scale
    sk = sink.astype(jnp.float32)[None, None, :, :, None]
    m = jnp.maximum(jnp.maximum(s_win.max(-1), s_ctx.max(-1)), sk)
    p_win = jnp.exp(s_win - m[..., None])
    p_ctx = jnp.exp(s_ctx - m[..., None])
    denom = p_win.sum(-1) + p_ctx.sum(-1) + jnp.exp(sk - m)
    o = (jnp.einsum('bnhgqk,bnkhd->bnqhgd', p_win.astype(vw.dtype), vw, preferred_element_type=jnp.float32)
         + jnp.einsum('bnhgqk,bkhd->bnqhgd', p_ctx.astype(vc.dtype), vc, preferred_element_type=jnp.float32))
    o = o / jnp.transpose(denom, (0, 1, 4, 2, 3))[..., None]
    return o.astype(q.dtype).reshape(B, L, Hkv * G * dh)


def context_attention(qc, kc, vc, sink):
    B, Lc, Hkv, G, dh = qc.shape
    s = jnp.einsum('bqhgd,bkhd->bhgqk', qc, kc, preferred_element_type=jnp.float32) * dh ** -0.5
    sk = sink.astype(jnp.float32)[None, :, :, None]
    m = jnp.maximum(s.max(-1), sk)
    p = jnp.exp(s - m[..., None])
    denom = p.sum(-1) + jnp.exp(sk - m)
    o = jnp.einsum('bhgqk,bkhd->bqhgd', p.astype(vc.dtype), vc, preferred_element_type=jnp.float32)
    o = o / jnp.transpose(denom, (0, 3, 1, 2))[..., None]
    return o.astype(qc.dtype).reshape(B, Lc, Hkv * G * dh)


def short_conv(x, w):
    K, C = w.shape
    pad = (K - 1) // 2
    y = lax.conv_general_dilated(x, w[:, None, :], window_strides=(1,), padding=[(pad, pad)],
                                 dimension_numbers=('NWC', 'WIO', 'NWC'), feature_group_count=C)
    return jax.nn.silu(y)


def l2norm(x):
    return x * lax.rsqrt(jnp.sum(x * x, axis=-1, keepdims=True) + EPS)


def gdn_chunk(q, k, v, g, beta, S0):
    B, L, H, dk = q.shape
    dv = v.shape[-1]
    out_dtype = v.dtype
    C = GDN_CHUNK
    N = L // C
    f32 = jnp.float32

    def chunks(t):
        t = t.astype(f32).reshape((B, N, C, H) + t.shape[3:])
        return jnp.moveaxis(t, 3, 1)

    q = chunks(l2norm(q.astype(f32)) * dk ** -0.5)
    k = chunks(l2norm(k.astype(f32)))
    v = chunks(v)
    g = chunks(g)
    beta = chunks(beta)
    gc = jnp.cumsum(g, axis=-1)
    diff = gc[..., :, None] - gc[..., None, :]
    strict = jnp.tril(jnp.ones((C, C), bool), -1)
    incl = jnp.tril(jnp.ones((C, C), bool))
    kk = jnp.einsum('bhnid,bhnjd->bhnij', k, k)
    a_mat = beta[..., :, None] * kk * jnp.exp(jnp.where(strict, diff, -jnp.inf))
    eye = jnp.eye(C, dtype=f32)
    t_mat = lax.linalg.triangular_solve(eye + a_mat, jnp.broadcast_to(eye, a_mat.shape),
                                        left_side=True, lower=True, unit_diagonal=True)
    u = t_mat @ (beta[..., None] * v)
    w = t_mat @ ((beta * jnp.exp(gc))[..., None] * k)
    qk = jnp.einsum('bhnid,bhnjd->bhnij', q, k) * jnp.exp(jnp.where(incl, diff, -jnp.inf))
    q_dec = q * jnp.exp(gc)[..., None]
    k_dec = k * jnp.exp(gc[..., -1:] - gc)[..., None]
    g_tot = jnp.exp(gc[..., -1])

    def step(S, inp):
        u_n, w_n, qd_n, kd_n, qk_n, gt_n = inp
        v_new = u_n - w_n @ S
        o_n = qd_n @ S + qk_n @ v_new
        S = S * gt_n[..., None, None] + jnp.swapaxes(kd_n, -1, -2) @ v_new
        return S, o_n

    xs = tuple(jnp.moveaxis(t, 2, 0) for t in (u, w, q_dec, k_dec, qk, g_tot))
    S, o = lax.scan(step, S0.astype(f32), xs)
    o = o.transpose(1, 0, 3, 2, 4).reshape(B, L, H, dv)
    return o.astype(out_dtype), S


def gdn_bidirectional(qkv, ab, conv_w, a_log, dt_bias, S0_f, S0_b):
    B, L, _ = qkv.shape
    qkv = short_conv(qkv, conv_w)
    q, k, v = [t.reshape(B, L, GDN_HEADS, GDN_HEAD_DIM) for t in jnp.split(qkv, 3, axis=-1)]
    a_f, a_b, b_f, b_b = jnp.split(ab.astype(jnp.float32), 4, axis=-1)
    f32 = jnp.float32
    g_f = -jnp.exp(a_log[0].astype(f32)) * jax.nn.softplus(a_f + dt_bias[0].astype(f32))
    g_b = -jnp.exp(a_log[1].astype(f32)) * jax.nn.softplus(a_b + dt_bias[1].astype(f32))
    beta_f = jax.nn.sigmoid(b_f)
    beta_b = jax.nn.sigmoid(b_b)
    flip = lambda t: jnp.flip(t, axis=1)
    o_f, S_f = gdn_chunk(q, k, v, g_f, beta_f, S0_f)
    o_b, S_b = gdn_chunk(flip(q), flip(k), flip(v), flip(g_b), flip(beta_b), S0_b)
    return o_f + flip(o_b), S_f, S_b


def gated_rms(o, z, gain):
    B, L, H, dv = o.shape
    o32 = o.astype(jnp.float32)
    o32 = o32 * lax.rsqrt(jnp.mean(o32 * o32, axis=-1, keepdims=True) + EPS) * gain.astype(jnp.float32)
    out = o32 * jax.nn.silu(z.astype(jnp.float32)).reshape(B, L, H, dv)
    return out.reshape(B, L, H * dv).astype(z.dtype)


def multiscale_pool(p, pool_w, pool_scale):
    B, L, W = p.shape
    p32 = p.astype(jnp.float32)
    cs = jnp.concatenate([jnp.zeros((B, 1, W), jnp.float32), jnp.cumsum(p32, axis=1)], axis=1)
    t = jnp.arange(L)
    outs = []
    for gi, win in enumerate(POOL_SIZES):
        lo = jnp.clip(t - win // 2, 0, L)
        hi = jnp.clip(t + win // 2, 0, L)
        csg = cs[..., gi * POOL_GROUP:(gi + 1) * POOL_GROUP]
        s = jnp.take(csg, hi, axis=1) - jnp.take(csg, lo, axis=1)
        cnt = (hi - lo).astype(jnp.float32)[None, :, None]
        outs.append(s / cnt - p32[..., gi * POOL_GROUP:(gi + 1) * POOL_GROUP])
    r = jnp.stack(outs, axis=2).astype(p.dtype)
    r = jnp.einsum('blgc,gcd->blgd', r, pool_w)
    return r.reshape(B, L, POOL_WIDTH) * pool_scale


def merge_branches(attn_o, gdn_o, pool_o, gate_logits, w_b_attn, w_b_gdn, w_b_pool, w_out):
    gates = jax.nn.sigmoid(gate_logits.reshape(gate_logits.shape[:-1] + (N_BRANCH, D_MODEL)))
    merged = (gates[..., 0, :] * (attn_o @ w_b_attn)
              + gates[..., 1, :] * (gdn_o @ w_b_gdn)
              + gates[..., 2, :] * (pool_o @ w_b_pool))
    return merged @ w_out


def hybrid_mixer(h, hc, row, col, w_in, attn_sink, conv_w, a_log, dt_bias, gdn_norm, pool_w, pool_scale,
                 w_b_attn, w_b_gdn, w_b_pool, w_out, need_ctx):
    B, L, _ = h.shape
    Lc = hc.shape[1]
    G = ATTN_HEADS // ATTN_KV_HEADS
    split = np.cumsum(IN_SIZES)[:-1].tolist()
    q, k, v, qkv, zg, ab, pin, gl = jnp.split(h @ w_in, split, axis=-1)
    qc, kc, vc, qkv_c, zg_c, ab_c, pin_c, gl_c = jnp.split(hc @ w_in, split, axis=-1)
    heads = lambda t, n: t.reshape(t.shape[:2] + (n, ATTN_HEAD_DIM))
    q = rope_2d(heads(q, ATTN_HEADS), row, col).reshape(B, L, ATTN_KV_HEADS, G, ATTN_HEAD_DIM)
    k = rope_2d(heads(k, ATTN_KV_HEADS), row, col)
    v = heads(v, ATTN_KV_HEADS)
    kc = heads(kc, ATTN_KV_HEADS)
    vc = heads(vc, ATTN_KV_HEADS)
    sink = attn_sink.reshape(ATTN_KV_HEADS, G)
    attn_o = window_attention(q, k, v, kc, vc, sink)
    S0 = jnp.zeros((B, GDN_HEADS, GDN_HEAD_DIM, GDN_HEAD_DIM), jnp.float32)
    oc_gdn, S_f, S_b = gdn_bidirectional(qkv_c, ab_c, conv_w, a_log, dt_bias, S0, S0)
    o_gdn, _, _ = gdn_bidirectional(qkv, ab, conv_w, a_log, dt_bias, S_f, S_b)
    gdn_o = gated_rms(o_gdn, zg, gdn_norm)
    pool_o = multiscale_pool(pin, pool_w, pool_scale)
    y = merge_branches(attn_o, gdn_o, pool_o, gl, w_b_attn, w_b_gdn, w_b_pool, w_out)
    if not need_ctx:
        return y, None
    attn_c = context_attention(qc.reshape(B, Lc, ATTN_KV_HEADS, G, ATTN_HEAD_DIM), kc, vc, sink)
    gdn_c = gated_rms(oc_gdn, zg_c, gdn_norm)
    pool_c = multiscale_pool(pin_c, pool_w, pool_scale)
    yc = merge_branches(attn_c, gdn_c, pool_c, gl_c, w_b_attn, w_b_gdn, w_b_pool, w_out)
    return y, yc


def setup_inputs(seed: int = 0) -> dict:
    key = jax.random.key(seed)
    ks = jax.random.split(key, 32)
    f32 = jnp.float32

    def nrm(i, shape, fan_in, gain=1.0):
        return jax.random.normal(ks[i], shape, f32) * (gain * fan_in ** -0.5)

    def gain_vec(i, shape):
        return 1.0 + 0.02 * jax.random.normal(ks[i], shape, f32)

    Dp = DEPTH
    dt = jnp.exp(jax.random.uniform(ks[14], (Dp, 2, GDN_HEADS), f32, math.log(1e-3), math.log(1e-1)))
    return {
        'x': jax.random.normal(ks[0], (BATCH, SEQ, D_MODEL), f32),
        'c': jax.random.normal(ks[1], (BATCH, D_MODEL), f32),
        'ctx': jax.random.normal(ks[2], (BATCH, CTX_LEN, D_MODEL), f32),
        'c_ctx': jax.random.normal(ks[3], (D_MODEL,), f32),
        'w_ada': nrm(4, (Dp, D_MODEL, N_MOD * D_MODEL), D_MODEL, 0.5),
        'b_ada': 0.02 * jax.random.normal(ks[5], (Dp, N_MOD * D_MODEL), f32),
        'norm_ffn1': gain_vec(6, (Dp, D_MODEL)),
        'w_ffn1_in': nrm(7, (Dp, D_MODEL, 2 * D_FF), D_MODEL),
        'w_ffn1_out': nrm(8, (Dp, D_FF, D_MODEL), D_FF),
        'norm_mix': gain_vec(9, (Dp, D_MODEL)),
        'w_in': nrm(10, (Dp, D_MODEL, IN_WIDTH), D_MODEL),
        'attn_sink': 0.5 * jax.random.normal(ks[11], (Dp, ATTN_HEADS), f32),
        'gdn_conv': nrm(12, (Dp, GDN_CONV, 3 * GDN_W), GDN_CONV),
        'gdn_a_log': jnp.log(jax.random.uniform(ks[13], (Dp, 2, GDN_HEADS), f32, 1.0, 16.0)),
        'gdn_dt_bias': dt + jnp.log(-jnp.expm1(-dt)),
        'gdn_norm': gain_vec(15, (Dp, GDN_HEAD_DIM)),
        'pool_w': nrm(16, (Dp, len(POOL_SIZES), POOL_GROUP, POOL_GROUP), POOL_GROUP),
        'pool_scale': 1.0 + 0.1 * jax.random.normal(ks[17], (Dp, POOL_WIDTH), f32),
        'w_branch_attn': nrm(18, (Dp, ATTN_HEADS * ATTN_HEAD_DIM, D_MODEL), ATTN_HEADS * ATTN_HEAD_DIM),
        'w_branch_gdn': nrm(19, (Dp, GDN_W, D_MODEL), GDN_W),
        'w_branch_pool': nrm(20, (Dp, POOL_WIDTH, D_MODEL), POOL_WIDTH),
        'w_out': nrm(21, (Dp, D_MODEL, D_MODEL), D_MODEL),
        'norm_ffn2': gain_vec(22, (Dp, D_MODEL)),
        'w_ffn2_in': nrm(23, (Dp, D_MODEL, 2 * D_FF), D_MODEL),
        'w_ffn2_out': nrm(24, (Dp, D_FF, D_MODEL), D_FF),
        'final_norm': gain_vec(25, (D_MODEL,)),
    }


def reference(x, c, ctx, c_ctx, w_ada, b_ada, norm_ffn1, w_ffn1_in, w_ffn1_out, norm_mix, w_in, attn_sink,
              gdn_conv, gdn_a_log, gdn_dt_bias, gdn_norm, pool_w, pool_scale, w_branch_attn, w_branch_gdn,
              w_branch_pool, w_out, norm_ffn2, w_ffn2_in, w_ffn2_out, final_norm):
    B, L, D = x.shape
    ROWS = L // GRID_W
    row = jnp.broadcast_to(jnp.arange(ROWS, dtype=jnp.int32)[:, None], (ROWS, GRID_W)).reshape(-1)
    col = jnp.broadcast_to(jnp.arange(GRID_W, dtype=jnp.int32)[None, :], (ROWS, GRID_W)).reshape(-1)
    sc = jax.nn.silu(c)
    scc = jax.nn.silu(c_ctx)
    xc = ctx
    for l in range(DEPTH):
        last = l == DEPTH - 1
        mod = (sc @ w_ada[l] + b_ada[l]).reshape(B, 1, N_MOD, D)
        modc = (scc @ w_ada[l] + b_ada[l]).reshape(N_MOD, D)
        m = [mod[:, :, i] for i in range(N_MOD)]
        mc = [modc[i] for i in range(N_MOD)]
        x = x + 0.5 * m[2] * swiglu(modulate(rms_norm(x, norm_ffn1[l]), m[0], m[1]), w_ffn1_in[l], w_ffn1_out[l])
        xc = xc + 0.5 * mc[2] * swiglu(modulate(rms_norm(xc, norm_ffn1[l]), mc[0], mc[1]), w_ffn1_in[l], w_ffn1_out[l])
        h = modulate(rms_norm(x, norm_mix[l]), m[3], m[4])
        hc = modulate(rms_norm(xc, norm_mix[l]), mc[3], mc[4])
        y, yc = hybrid_mixer(h, hc, row, col, w_in[l], attn_sink[l], gdn_conv[l], gdn_a_log[l], gdn_dt_bias[l],
                             gdn_norm[l], pool_w[l], pool_scale[l], w_branch_attn[l], w_branch_gdn[l],
                             w_branch_pool[l], w_out[l], not last)
        x = x + m[5] * y
        x = x + 0.5 * m[8] * swiglu(modulate(rms_norm(x, norm_ffn2[l]), m[6], m[7]), w_ffn2_in[l], w_ffn2_out[l])
        if not last:
            xc = xc + mc[5] * yc
            xc = xc + 0.5 * mc[8] * swiglu(modulate(rms_norm(xc, norm_ffn2[l]), mc[6], mc[7]), w_ffn2_in[l], w_ffn2_out[l])
    return rms_norm(x, final_norm)
```

```python
import functools
import math

import jax
import jax.numpy as jnp
from jax import lax
from jax.experimental import pallas as pl
from jax.experimental.pallas import tpu as pltpu

F32 = jnp.float32
BF16 = jnp.bfloat16
HIGHEST = lax.Precision.HIGHEST

GRID_W = 64
ATTN_HEADS = 8
ATTN_KV_HEADS = 2
ATTN_HEAD_DIM = 64
ATTN_GROUP = ATTN_HEADS // ATTN_KV_HEADS
ATTN_BLOCK = 128
ROPE_THETA = 10000.0
GDN_HEADS = 4
GDN_HEAD_DIM = 128
GDN_W = GDN_HEADS * GDN_HEAD_DIM
GDN_CONV = 5
GDN_CHUNK = 64
POOL_SIZES = (2, 4, 8, 16)
POOL_GROUP = 128
POOL_WIDTH = POOL_GROUP * len(POOL_SIZES)
N_MOD = 9
N_BRANCH = 3
EPS = 1e-6

LANES = 128
HALO = 8
VMEM_LIMIT = 56 * 1024 * 1024


def _dot(a, b, precision=None):
    return jnp.dot(a, b, preferred_element_type=F32, precision=precision)


def _dot_nt(a, b):
    return lax.dot_general(a, b, (((1,), (1,)), ((), ())), preferred_element_type=F32)


def _dot_tn(a, b):
    return lax.dot_general(a, b, (((0,), (0,)), ((), ())), preferred_element_type=F32)


def _silu(x):
    return x * jax.nn.sigmoid(x)


def _params(sem):
    return pltpu.CompilerParams(dimension_semantics=sem, vmem_limit_bytes=VMEM_LIMIT)


def _const_spec(shape):
    nd = len(shape)
    return pl.BlockSpec(shape, lambda *_: (0,) * nd, pipeline_mode=pl.Buffered(1))


def _norm_mod(x, gain, mod_ref, row0, ctx_len):
    y = x * lax.rsqrt(jnp.mean(x * x, axis=-1, keepdims=True) + EPS) * gain
    rows = row0 + lax.broadcasted_iota(jnp.int32, (x.shape[0], 1), 0)
    is_ctx = rows < ctx_len
    shift = jnp.where(is_ctx, mod_ref[3:4, :], mod_ref[0:1, :])
    scale = jnp.where(is_ctx, mod_ref[4:5, :], mod_ref[1:2, :])
    gate = jnp.where(is_ctx, mod_ref[5:6, :], mod_ref[2:3, :])
    return y * (1.0 + scale) + shift, gate


def _ada_kernel(c_ref, w_ref, b_ref, out_ref):
    sc = _silu(c_ref[...])
    out_ref[...] = _dot(sc, w_ref[...], HIGHEST) + b_ref[...]


def _ada_call(cvec, w_ada, b_ada):
    depth, d, n = w_ada.shape
    tn = n // 4
    return pl.pallas_call(
        _ada_kernel,
        grid=(depth, n // tn),
        in_specs=[pl.BlockSpec((8, d), lambda l, j: (0, 0)),
                  pl.BlockSpec((None, d, tn), lambda l, j: (l, 0, j)),
                  pl.BlockSpec((None, 1, tn), lambda l, j: (l, 0, j))],
        out_specs=pl.BlockSpec((None, 8, tn), lambda l, j: (l, 0, j)),
        out_shape=jax.ShapeDtypeStruct((depth, 8, n), F32),
        compiler_params=_params(("arbitrary", "arbitrary")),
        name="ada_mod",
    )(cvec, w_ada, b_ada.reshape(depth, 1, n))


def _ffn_kernel(x_ref, mod_ref, gain_ref, win_ref, wout_ref, out_ref, *, tm, ctx_len, d_ff, fc):
    x = x_ref[...]
    h, gate = _norm_mod(x, gain_ref[...], mod_ref, pl.program_id(0) * tm, ctx_len)
    hb = h.astype(BF16)
    acc = jnp.zeros(x.shape, F32)
    for j in range(d_ff // fc):
        gt = _dot(hb, win_ref[:, j * fc:(j + 1) * fc])
        up = _dot(hb, win_ref[:, d_ff + j * fc:d_ff + (j + 1) * fc])
        acc = acc + _dot((_silu(gt) * up).astype(BF16), wout_ref[j * fc:(j + 1) * fc, :])
    out_ref[...] = x + (0.5 * gate) * acc


def _ffn_call(xa, mod, gain, w_in, w_out, *, tm, ctx_len):
    la, d = xa.shape
    d_ff = w_out.shape[0]
    fc = d_ff // 2
    kern = functools.partial(_ffn_kernel, tm=tm, ctx_len=ctx_len, d_ff=d_ff, fc=fc)
    return pl.pallas_call(
        kern,
        grid=(la // tm,),
        in_specs=[pl.BlockSpec((tm, d), lambda i: (i, 0)),
                  _const_spec((8, d)), _const_spec((1, d)),
                  _const_spec(w_in.shape), _const_spec(w_out.shape)],
        out_specs=pl.BlockSpec((tm, d), lambda i: (i, 0)),
        out_shape=jax.ShapeDtypeStruct((la, d), F32),
        input_output_aliases={0: 0},
        compiler_params=_params(("parallel",)),
        name="ffn",
    )(xa, mod, gain, w_in, w_out)


_SEC = (("q", 512), ("k", 128), ("v", 128), ("gqkv", 3 * GDN_W), ("z", GDN_W), ("ab", LANES),
        ("pin", POOL_WIDTH), ("gl", N_BRANCH * 1024))


def _sec_offsets():
    offs, o = {}, 0
    for name, width in _SEC:
        offs[name] = (o, width)
        o += width
    return offs, o


def _rope(t, cos, sin):
    n = t.shape[-1]
    lane = lax.broadcasted_iota(jnp.int32, t.shape, 1)
    swapped = jnp.where(lane % 32 < 16, pltpu.roll(t, n - 16, 1), pltpu.roll(t, 16, 1))
    reps = n // LANES
    c = jnp.concatenate([cos] * reps, axis=1) if reps > 1 else cos
    s = jnp.concatenate([sin] * reps, axis=1) if reps > 1 else sin
    return t * c + swapped * s


def _proj_kernel(x_ref, mod_ref, gain_ref, w_ref, cos_ref, sin_ref,
                 q_ref, k_ref, v_ref, gqkv_ref, z_ref, ab_ref, pin_ref, gl_ref, *, tm, ctx_len):
    offs, _ = _sec_offsets()
    x = x_ref[...]
    h, _ = _norm_mod(x, gain_ref[...], mod_ref, pl.program_id(0) * tm, ctx_len)
    hb = h.astype(BF16)

    def sec(name):
        o, width = offs[name]
        return _dot(hb, w_ref[:, o:o + width])

    cos, sin = cos_ref[...], sin_ref[...]
    q_ref[...] = (_rope(sec("q"), cos, sin) * (ATTN_HEAD_DIM ** -0.5)).astype(BF16)
    k_ref[...] = _rope(sec("k"), cos, sin).astype(BF16)
    v_ref[...] = sec("v").astype(BF16)
    gqkv_ref[...] = sec("gqkv")
    z_ref[...] = sec("z")
    ab_ref[...] = sec("ab")
    pin_ref[...] = sec("pin")
    gl_ref[...] = sec("gl")


def _proj_call(xa, mod, gain, w, cos, sin, *, tm, ctx_len):
    la, d = xa.shape
    offs, _ = _sec_offsets()
    kern = functools.partial(_proj_kernel, tm=tm, ctx_len=ctx_len)
    row = lambda width: pl.BlockSpec((tm, width), lambda i: (i, 0))
    names = ("q", "k", "v", "gqkv", "z", "ab", "pin", "gl")
    dtypes = (BF16, BF16, BF16, F32, F32, F32, F32, F32)
    return pl.pallas_call(
        kern,
        grid=(la // tm,),
        in_specs=[row(d), _const_spec((8, d)), _const_spec((1, d)), _const_spec(w.shape),
                  row(LANES), row(LANES)],
        out_specs=[row(offs[n][1]) for n in names],
        out_shape=[jax.ShapeDtypeStruct((la, offs[n][1]), dt) for n, dt in zip(names, dtypes)],
        compiler_params=_params(("parallel",)),
        name="proj",
    )(xa, mod, gain, w, cos, sin)


def _attn_kernel(sink_ref, q_ref, kp_ref, kc_ref, kn_ref, vp_ref, vc_ref, vn_ref, kx_ref, vx_ref,
                 o_ref, *, ctx_blocks, n_blocks):
    b = pl.program_id(0)
    blk = ATTN_BLOCK
    is_lat = b >= ctx_blocks
    prev_ok = b - 1 >= ctx_blocks
    next_ok = jnp.logical_and(is_lat, b + 1 < n_blocks)
    nctx = kx_ref.shape[0]
    ri = lax.broadcasted_iota(jnp.int32, (blk, blk), 0)
    ci = lax.broadcasted_iota(jnp.int32, (blk, blk), 1)
    valid = jnp.concatenate(
        [jnp.logical_and(ci >= ri, prev_ok),
         jnp.broadcast_to(is_lat, (blk, blk)),
         jnp.logical_and(ci <= ri, next_ok),
         jnp.full((blk, nctx), True)], axis=1)
    dh = ATTN_HEAD_DIM
    for hk in range(ATTN_KV_HEADS):
        ks = slice(hk * dh, (hk + 1) * dh)
        k_all = jnp.concatenate([kp_ref[:, ks], kc_ref[:, ks], kn_ref[:, ks], kx_ref[:, ks]], axis=0)
        v_all = jnp.concatenate([vp_ref[:, ks], vc_ref[:, ks], vn_ref[:, ks], vx_ref[:, ks]], axis=0)
        for g in range(ATTN_GROUP):
            hd = hk * ATTN_GROUP + g
            s = _dot_nt(q_ref[:, hd * dh:(hd + 1) * dh], k_all)
            s = jnp.where(valid, s, -jnp.inf)
            sk = sink_ref[hd]
            m = jnp.maximum(jnp.max(s, axis=-1, keepdims=True), sk)
            p = jnp.exp(s - m)
            denom = jnp.sum(p, axis=-1, keepdims=True) + jnp.exp(sk - m)
            o = _dot(p.astype(BF16), v_all) / denom
            o_ref[:, hd * dh:(hd + 1) * dh] = o.astype(BF16)


def _attn_call(sink, q, k, v, *, ctx_len):
    la = q.shape[0]
    blk = ATTN_BLOCK
    nb = la // blk
    cb = ctx_len // blk
    kern = functools.partial(_attn_kernel, ctx_blocks=cb, n_blocks=nb)
    kvw = k.shape[1]
    prev = pl.BlockSpec((blk, kvw), lambda b: (jnp.maximum(b - 1, 0), 0))
    cur = pl.BlockSpec((blk, kvw), lambda b: (b, 0))
    nxt = pl.BlockSpec((blk, kvw), lambda b: (jnp.minimum(b + 1, nb - 1), 0))
    cx = pl.BlockSpec((ctx_len, kvw), lambda b: (0, 0))
    return pl.pallas_call(
        kern,
        grid=(nb,),
        in_specs=[pl.BlockSpec(memory_space=pltpu.SMEM),
                  pl.BlockSpec((blk, q.shape[1]), lambda b: (b, 0)),
                  prev, cur, nxt, prev, cur, nxt, cx, cx],
        out_specs=pl.BlockSpec((blk, q.shape[1]), lambda b: (b, 0)),
        out_shape=jax.ShapeDtypeStruct(q.shape, BF16),
        compiler_params=_params(("parallel",)),
        name="attn",
    )(sink, q, k, k, k, v, v, v, k, v)


def _inv_unit_triangular(a):
    n = a.shape[0]
    eye = (lax.broadcasted_iota(jnp.int32, (n, n), 0) ==
           lax.broadcasted_iota(jnp.int32, (n, n), 1)).astype(F32)
    t = eye - a
    p = _dot(a, a, HIGHEST)
    steps = int(math.log2(n)) - 1
    for s in range(steps):
        t = t + _dot(t, p, HIGHEST)
        if s + 1 < steps:
            p = _dot(p, p, HIGHEST)
    return t


def _gdn_prep_kernel(prev_ref, cur_ref, next_ref, ab_ref, convw_ref, abp_ref,
                     u_ref, w_ref, qd_ref, kd_ref, qk_ref, gt_ref, xs_ref, *, tg, ctx_len, total):
    i = pl.program_id(0)
    row0 = i * tg
    first = jnp.logical_or(row0 == 0, row0 == ctx_len)
    last = jnp.logical_or(row0 + tg == ctx_len, row0 + tg == total)
    xs_ref[0:HALO, :] = jnp.where(first, 0.0, prev_ref[...])
    xs_ref[HALO:HALO + tg, :] = cur_ref[...]
    xs_ref[HALO + tg:HALO + tg + HALO, :] = jnp.where(last, 0.0, next_ref[...])
    pad = (GDN_CONV - 1) // 2
    dk = GDN_HEAD_DIM
    c_len = GDN_CHUNK

    def conv_silu(col):
        ls = slice(col * dk, (col + 1) * dk)
        acc = xs_ref[pl.ds(HALO - pad, tg), ls] * convw_ref[0:1, ls]
        for j in range(1, GDN_CONV):
            acc = acc + xs_ref[pl.ds(HALO - pad + j, tg), ls] * convw_ref[j:j + 1, ls]
        return _silu(acc)

    def l2n(t):
        return t * lax.rsqrt(jnp.sum(t * t, axis=-1, keepdims=True) + EPS)

    abv = ab_ref[...]
    g_all = -jnp.exp(abp_ref[1:2, :]) * jax.nn.softplus(abv + abp_ref[0:1, :])
    b_all = jax.nn.sigmoid(abv)

    ri = lax.broadcasted_iota(jnp.int32, (c_len, c_len), 0)
    ci = lax.broadcasted_iota(jnp.int32, (c_len, c_len), 1)
    strict = (ci < ri, ci > ri)
    incl = (ci <= ri, ci >= ri)
    tri = (incl[0].astype(F32), incl[1].astype(F32))
    last_row = (c_len - 1, 0)
    gcs = [[_dot(tri[d], g_all[c * c_len:(c + 1) * c_len], HIGHEST) for d in range(2)]
           for c in range(tg // c_len)]

    for h in range(GDN_HEADS):
        q_h = l2n(conv_silu(h)) * (dk ** -0.5)
        k_h = l2n(conv_silu(GDN_HEADS + h))
        v_h = conv_silu(2 * GDN_HEADS + h)
        hs = slice(h * dk, (h + 1) * dk)
        for c in range(tg // c_len):
            rs = slice(c * c_len, (c + 1) * c_len)
            q, k, v = q_h[rs], k_h[rs], v_h[rs]
            kb = k.astype(BF16)
            kk = _dot_nt(kb, kb)
            qkt = _dot_nt(q.astype(BF16), kb)
            for d in range(2):
                lg = d * GDN_HEADS + h
                lb = 2 * GDN_HEADS + lg
                g = g_all[rs, lg:lg + 1]
                beta = b_all[rs, lb:lb + 1]
                dmat = _dot(tri[d], jnp.where(strict[d], g, 0.0), HIGHEST)
                gc = gcs[c][d][:, lg:lg + 1]
                e = jnp.where(incl[d], jnp.exp(dmat), 0.0)
                a = jnp.where(strict[d], beta * kk * e, 0.0)
                t = _inv_unit_triangular(a).astype(BF16)
                egc = jnp.exp(gc)
                g_last = gc[last_row[d]:last_row[d] + 1, :]
                u_ref[d, rs, hs] = _dot(t, (beta * v).astype(BF16))
                w_ref[d, rs, hs] = _dot(t, (beta * egc * k).astype(BF16)).astype(BF16)
                qd_ref[d, rs, hs] = (q * egc).astype(BF16)
                kd_ref[d, rs, hs] = (k * jnp.exp(g_last - gc)).astype(BF16)
                qk_ref[d, rs, h * c_len:(h + 1) * c_len] = (qkt * e).astype(BF16)
                gt_ref[d, c, h:h + 1, :] = jnp.broadcast_to(jnp.exp(g_last), (1, LANES))


def _gdn_prep_call(gqkv, ab, conv_w, abp, *, tg, ctx_len):
    la, wq = gqkv.shape
    nch = la // GDN_CHUNK
    cpt = tg // GDN_CHUNK
    hb = tg // HALO
    nhalo = la // HALO
    kern = functools.partial(_gdn_prep_kernel, tg=tg, ctx_len=ctx_len, total=la)
    blk3 = lambda width: pl.BlockSpec((2, tg, width), lambda i: (0, i, 0))
    return pl.pallas_call(
        kern,
        grid=(la // tg,),
        in_specs=[pl.BlockSpec((HALO, wq), lambda i: (jnp.maximum(i * hb - 1, 0), 0)),
                  pl.BlockSpec((tg, wq), lambda i: (i, 0)),
                  pl.BlockSpec((HALO, wq), lambda i: (jnp.minimum((i + 1) * hb, nhalo - 1), 0)),
                  pl.BlockSpec((tg, LANES), lambda i: (i, 0)),
                  _const_spec(conv_w.shape), _const_spec(abp.shape)],
        out_specs=[blk3(GDN_W), blk3(GDN_W), blk3(GDN_W), blk3(GDN_W), blk3(GDN_HEADS * GDN_CHUNK),
                   pl.BlockSpec((2, cpt, GDN_HEADS, LANES), lambda i: (0, i, 0, 0))],
        out_shape=[jax.ShapeDtypeStruct((2, la, GDN_W), F32),
                   jax.ShapeDtypeStruct((2, la, GDN_W), BF16),
                   jax.ShapeDtypeStruct((2, la, GDN_W), BF16),
                   jax.ShapeDtypeStruct((2, la, GDN_W), BF16),
                   jax.ShapeDtypeStruct((2, la, GDN_HEADS * GDN_CHUNK), BF16),
                   jax.ShapeDtypeStruct((2, nch, GDN_HEADS, LANES), F32)],
        scratch_shapes=[pltpu.VMEM((tg + 2 * HALO, wq), F32)],
        compiler_params=_params(("parallel",)),
        name="gdn_prep",
    )(gqkv, gqkv, gqkv, ab, conv_w, abp)


def _gdn_scan_kernel(uf, ub, wf, wb, qdf, qdb, kdf, kdb, qkf, qkb, gtf, gtb, of_ref, ob_ref, s_ref):
    @pl.when(pl.program_id(0) == 0)
    def _():
        s_ref[...] = jnp.zeros(s_ref.shape, F32)

    dk = GDN_HEAD_DIM
    c_len = GDN_CHUNK
    dirs = ((uf, wf, qdf, kdf, qkf, gtf, of_ref), (ub, wb, qdb, kdb, qkb, gtb, ob_ref))
    for d, (u, w, qd, kd, qk, gt, o_ref) in enumerate(dirs):
        for h in range(GDN_HEADS):
            hs = slice(h * dk, (h + 1) * dk)
            s = s_ref[d * GDN_HEADS + h]
            sb = s.astype(BF16)
            v_new = u[:, hs] - _dot(w[:, hs], sb)
            vb = v_new.astype(BF16)
            o_ref[:, hs] = _dot(qd[:, hs], sb) + _dot(qk[:, h * c_len:(h + 1) * c_len], vb)
            s_ref[d * GDN_HEADS + h] = s * gt[h:h + 1, :] + _dot_tn(kd[:, hs], vb)


def _gdn_scan_call(u, w, qd, kd, qk, gt, *, ctx_len):
    la = u.shape[1]
    c_len = GDN_CHUNK
    nch = la // c_len
    ncc = ctx_len // c_len

    def bwd(n):
        return jnp.where(n < ncc, ncc - 1 - n, nch + ncc - 1 - n)

    def pair(width):
        return [pl.BlockSpec((None, c_len, width), lambda n: (0, n, 0)),
                pl.BlockSpec((None, c_len, width), lambda n: (1, bwd(n), 0))]

    gts = [pl.BlockSpec((None, None, GDN_HEADS, LANES), lambda n: (0, n, 0, 0)),
           pl.BlockSpec((None, None, GDN_HEADS, LANES), lambda n: (1, bwd(n), 0, 0))]
    args, specs = [], []
    for arr, sp in ((u, pair(GDN_W)), (w, pair(GDN_W)), (qd, pair(GDN_W)), (kd, pair(GDN_W)),
                    (qk, pair(GDN_HEADS * c_len)), (gt, gts)):
        args += [arr, arr]
        specs += sp
    return pl.pallas_call(
        _gdn_scan_kernel,
        grid=(nch,),
        in_specs=specs,
        out_specs=[pl.BlockSpec((c_len, GDN_W), lambda n: (n, 0)),
                   pl.BlockSpec((c_len, GDN_W), lambda n: (bwd(n), 0))],
        out_shape=[jax.ShapeDtypeStruct((la, GDN_W), F32), jax.ShapeDtypeStruct((la, GDN_W), F32)],
        scratch_shapes=[pltpu.VMEM((2 * GDN_HEADS, GDN_HEAD_DIM, GDN_HEAD_DIM), F32)],
        compiler_params=_params(("arbitrary",)),
        name="gdn_scan",
    )(*args)


def _merge_kernel(x_ref, attn_ref, of_ref, ob_ref, z_ref, pprev_ref, pcur_ref, pnext_ref, gl_ref,
                  mod_ref, gnorm_ref, poolw_ref, pscale_ref, wba_ref, wbg_ref, wbp_ref, wout_ref,
                  out_ref, ps_ref, *, tm, ctx_len, total):
    i = pl.program_id(0)
    row0 = i * tm
    first = jnp.logical_or(row0 == 0, row0 == ctx_len)
    last = jnp.logical_or(row0 + tm == ctx_len, row0 + tm == total)
    ps_ref[0:HALO, :] = jnp.where(first, 0.0, pprev_ref[...])
    ps_ref[HALO:HALO + tm, :] = pcur_ref[...]
    ps_ref[HALO + tm:HALO + tm + HALO, :] = jnp.where(last, 0.0, pnext_ref[...])

    rows = row0 + lax.broadcasted_iota(jnp.int32, (tm, 1), 0)
    is_ctx = rows < ctx_len
    seg_pos = jnp.where(is_ctx, rows, rows - ctx_len)
    seg_len = jnp.where(is_ctx, ctx_len, total - ctx_len)

    pooled = []
    for gi, win in enumerate(POOL_SIZES):
        ls = slice(gi * POOL_GROUP, (gi + 1) * POOL_GROUP)
        s = ps_ref[pl.ds(HALO - win // 2, tm), ls]
        for o in range(1, win):
            s = s + ps_ref[pl.ds(HALO - win // 2 + o, tm), ls]
        cnt = (jnp.minimum(seg_pos + win // 2, seg_len) - jnp.maximum(seg_pos - win // 2, 0)).astype(F32)
        r = s / cnt - pcur_ref[:, ls]
        pooled.append(_dot(r.astype(BF16), poolw_ref[gi]) * pscale_ref[:, ls])
    pool_o = jnp.concatenate(pooled, axis=1).astype(BF16)

    gdn = []
    for h in range(GDN_HEADS):
        hs = slice(h * GDN_HEAD_DIM, (h + 1) * GDN_HEAD_DIM)
        o = of_ref[:, hs] + ob_ref[:, hs]
        o = o * lax.rsqrt(jnp.mean(o * o, axis=-1, keepdims=True) + EPS) * gnorm_ref[...]
        gdn.append(o * _silu(z_ref[:, hs]))
    gdn_o = jnp.concatenate(gdn, axis=1).astype(BF16)

    d = x_ref.shape[1]
    merged = (jax.nn.sigmoid(gl_ref[:, 0:d]) * _dot(attn_ref[...], wba_ref[...])
              + jax.nn.sigmoid(gl_ref[:, d:2 * d]) * _dot(gdn_o, wbg_ref[...])
              + jax.nn.sigmoid(gl_ref[:, 2 * d:3 * d]) * _dot(pool_o, wbp_ref[...]))
    y = _dot(merged.astype(BF16), wout_ref[...])
    gate = jnp.where(is_ctx, mod_ref[5:6, :], mod_ref[2:3, :])
    out_ref[...] = x_ref[...] + gate * y


def _merge_call(xa, attn_o, o_f, o_b, z, pin, gl, mod, gnorm, pool_w, pool_scale, wba, wbg, wbp, wout,
                *, tm, ctx_len):
    la, d = xa.shape
    hb = tm // HALO
    nhalo = la // HALO
    pw = pin.shape[1]
    kern = functools.partial(_merge_kernel, tm=tm, ctx_len=ctx_len, total=la)
    row = lambda width: pl.BlockSpec((tm, width), lambda i: (i, 0))
    return pl.pallas_call(
        kern,
        grid=(la // tm,),
        in_specs=[row(d), row(attn_o.shape[1]), row(GDN_W), row(GDN_W), row(GDN_W),
                  pl.BlockSpec((HALO, pw), lambda i: (jnp.maximum(i * hb - 1, 0), 0)),
                  row(pw),
                  pl.BlockSpec((HALO, pw), lambda i: (jnp.minimum((i + 1) * hb, nhalo - 1), 0)),
                  row(gl.shape[1]),
                  _const_spec((8, d)), _const_spec(gnorm.shape), _const_spec(pool_w.shape),
                  _const_spec(pool_scale.shape), _const_spec(wba.shape), _const_spec(wbg.shape),
                  _const_spec(wbp.shape), _const_spec(wout.shape)],
        out_specs=row(d),
        out_shape=jax.ShapeDtypeStruct((la, d), F32),
        input_output_aliases={0: 0},
        scratch_shapes=[pltpu.VMEM((tm + 2 * HALO, pw), F32)],
        compiler_params=_params(("parallel",)),
        name="merge",
    )(xa, attn_o, o_f, o_b, z, pin, pin, pin, gl, mod, gnorm, pool_w, pool_scale, wba, wbg, wbp, wout)


def _final_norm_kernel(x_ref, gain_ref, out_ref):
    x = x_ref[...]
    out_ref[...] = x * lax.rsqrt(jnp.mean(x * x, axis=-1, keepdims=True) + EPS) * gain_ref[...]


def _final_norm_call(xa, gain, *, tm, ctx_len):
    la, d = xa.shape
    skip = ctx_len // tm
    return pl.pallas_call(
        _final_norm_kernel,
        grid=((la - ctx_len) // tm,),
        in_specs=[pl.BlockSpec((tm, d), lambda i: (i + skip, 0)), _const_spec((1, d))],
        out_specs=pl.BlockSpec((tm, d), lambda i: (i, 0)),
        out_shape=jax.ShapeDtypeStruct((la - ctx_len, d), F32),
        compiler_params=_params(("parallel",)),
        name="final_norm",
    )(xa, gain)


def _rope_tables(seq, ctx_len):
    t = jnp.arange(seq, dtype=jnp.int32)
    lane = jnp.arange(LANES, dtype=jnp.int32)
    sub = lane % ATTN_HEAD_DIM
    quarter = ATTN_HEAD_DIM // 4
    inv = ROPE_THETA ** (-(sub % quarter).astype(F32) / quarter)
    pos = jnp.where((sub // (2 * quarter))[None, :] == 0, (t // GRID_W)[:, None], (t % GRID_W)[:, None])
    ang = pos.astype(F32) * inv[None, :]
    cos = jnp.cos(ang)
    sin = jnp.where(((sub % (2 * quarter)) < quarter)[None, :], -jnp.sin(ang), jnp.sin(ang))
    cos = jnp.concatenate([jnp.ones((ctx_len, LANES), F32), cos], axis=0)
    sin = jnp.concatenate([jnp.zeros((ctx_len, LANES), F32), sin], axis=0)
    return cos, sin


def _pack_w_in(w_in_l):
    sizes = (512, 128, 128, 3 * GDN_W, GDN_W, 4 * GDN_HEADS, POOL_WIDTH, N_BRANCH * 1024)
    parts, o = [], 0
    for sz, (_, width) in zip(sizes, _SEC):
        p = w_in_l[:, o:o + sz]
        if width != sz:
            p = jnp.pad(p, ((0, 0), (0, width - sz)))
        parts.append(p)
        o += sz
    return jnp.concatenate(parts, axis=1).astype(BF16)


def kernel(x, c, ctx, c_ctx, w_ada, b_ada, norm_ffn1, w_ffn1_in, w_ffn1_out, norm_mix, w_in, attn_sink,
           gdn_conv, gdn_a_log, gdn_dt_bias, gdn_norm, pool_w, pool_scale, w_branch_attn, w_branch_gdn,
           w_branch_pool, w_out, norm_ffn2, w_ffn2_in, w_ffn2_out, final_norm):
    batch, seq, d = x.shape
    assert batch == 1, "single-sequence kernel"
    ctx_len = ctx.shape[1]
    depth = w_ada.shape[0]
    tm = 256
    assert ctx_len % tm == 0 and seq % tm == 0 and seq % GRID_W == 0

    xa = jnp.concatenate([ctx[0], x[0]], axis=0)
    cvec = jnp.zeros((8, d), F32).at[0].set(c[0]).at[1].set(c_ctx)
    mods = _ada_call(cvec, w_ada, b_ada).reshape(depth, 8, N_MOD, d)
    cos, sin = _rope_tables(seq, ctx_len)

    for l in range(depth):
        def mod_rows(j):
            m = jnp.concatenate([mods[l, 0, 3 * j:3 * j + 3], mods[l, 1, 3 * j:3 * j + 3]], axis=0)
            return jnp.pad(m, ((0, 2), (0, 0)))

        xa = _ffn_call(xa, mod_rows(0), norm_ffn1[l][None], w_ffn1_in[l].astype(BF16),
                       w_ffn1_out[l].astype(BF16), tm=tm, ctx_len=ctx_len)

        mod_mix = mod_rows(1)
        q, k, v, gqkv, z, ab, pin, gl = _proj_call(xa, mod_mix, norm_mix[l][None], _pack_w_in(w_in[l]),
                                                   cos, sin, tm=tm, ctx_len=ctx_len)
        attn_o = _attn_call(attn_sink[l], q, k, v, ctx_len=ctx_len)
        conv_w = jnp.pad(gdn_conv[l], ((0, 8 - GDN_CONV), (0, 0)))
        abp = jnp.zeros((8, LANES), F32)
        abp = abp.at[0, :2 * GDN_HEADS].set(gdn_dt_bias[l].reshape(-1))
        abp = abp.at[1, :2 * GDN_HEADS].set(gdn_a_log[l].reshape(-1))
        u, w, qd, kd, qk, gt = _gdn_prep_call(gqkv, ab, conv_w, abp, tg=tm, ctx_len=ctx_len)
        o_f, o_b = _gdn_scan_call(u, w, qd, kd, qk, gt, ctx_len=ctx_len)
        xa = _merge_call(xa, attn_o, o_f, o_b, z, pin, gl, mod_mix, gdn_norm[l][None],
                         pool_w[l].astype(BF16), pool_scale[l][None],
                         w_branch_attn[l].astype(BF16), w_branch_gdn[l].astype(BF16),
                         w_branch_pool[l].astype(BF16), w_out[l].astype(BF16), tm=tm, ctx_len=ctx_len)

        xa = _ffn_call(xa, mod_rows(2), norm_ffn2[l][None], w_ffn2_in[l].astype(BF16),
                       w_ffn2_out[l].astype(BF16), tm=tm, ctx_len=ctx_len)

    out = _final_norm_call(xa, final_norm[None], tm=tm, ctx_len=ctx_len)
    return out[None]
```

```python
import functools

import jax
import jax.numpy as jnp
from jax import lax
from jax.experimental import pallas as pl
from jax.experimental.pallas import tpu as pltpu

F32 = jnp.float32
BF16 = jnp.bfloat16
HIGHEST = lax.Precision.HIGHEST

GRID_W = 64
ATTN_HEADS = 8
ATTN_KV_HEADS = 2
ATTN_HEAD_DIM = 64
ATTN_BLOCK = 128
ROPE_THETA = 10000.0
GDN_HEADS = 4
GDN_HEAD_DIM = 128
GDN_W = GDN_HEADS * GDN_HEAD_DIM
GDN_CONV = 5
GDN_CHUNK = 128
GDN_BASE = 16
POOL_SIZES = (2, 4, 8, 16)
POOL_GROUP = 128
POOL_WIDTH = POOL_GROUP * len(POOL_SIZES)
N_MOD = 9
N_BRANCH = 3
EPS = 1e-6

LANES = 128
HALO = 8
VMEM_LIMIT = 56 * 1024 * 1024
COL_CHUNK = 512


def _dot(a, b, precision=None):
    return jnp.dot(a, b, preferred_element_type=F32, precision=precision)


def _dot_nt(a, b):
    return lax.dot_general(a, b, (((1,), (1,)), ((), ())), preferred_element_type=F32)


def _silu(x):
    return x * jax.nn.sigmoid(x)


def _params(sem):
    return pltpu.CompilerParams(dimension_semantics=sem, vmem_limit_bytes=VMEM_LIMIT)


def _const_spec(shape):
    nd = len(shape)
    return pl.BlockSpec(shape, lambda *_: (0,) * nd, pipeline_mode=pl.Buffered(1))


def _norm_mod(x, gain, mod_ref, row0, ctx_len):
    y = x * lax.rsqrt(jnp.mean(x * x, axis=-1, keepdims=True) + EPS) * gain
    rows = row0 + lax.broadcasted_iota(jnp.int32, (x.shape[0], 1), 0)
    is_ctx = rows < ctx_len
    shift = jnp.where(is_ctx, mod_ref[3:4, :], mod_ref[0:1, :])
    scale = jnp.where(is_ctx, mod_ref[4:5, :], mod_ref[1:2, :])
    gate = jnp.where(is_ctx, mod_ref[5:6, :], mod_ref[2:3, :])
    return y * (1.0 + scale) + shift, gate


def _ada_kernel(c_ref, w_ref, b_ref, out_ref):
    sc = _silu(c_ref[...])
    out_ref[...] = _dot(sc, w_ref[...], HIGHEST) + b_ref[...]


def _ada_call(cvec, w_ada, b_ada):
    depth, d, n = w_ada.shape
    tn = n // 4
    return pl.pallas_call(
        _ada_kernel,
        grid=(depth, n // tn),
        in_specs=[pl.BlockSpec((8, d), lambda l, j: (0, 0)),
                  pl.BlockSpec((None, d, tn), lambda l, j: (l, 0, j)),
                  pl.BlockSpec((None, 1, tn), lambda l, j: (l, 0, j))],
        out_specs=pl.BlockSpec((None, 8, tn), lambda l, j: (l, 0, j)),
        out_shape=jax.ShapeDtypeStruct((depth, 8, n), F32),
        compiler_params=_params(("arbitrary", "arbitrary")),
        name="ada_mod",
    )(cvec, w_ada, b_ada.reshape(depth, 1, n))


def _ffn_kernel(x_ref, mod_ref, gain_ref, win_ref, wout_ref, out_ref, *, tm, ctx_len, d_ff, fc):
    x = x_ref[...]
    h, gate = _norm_mod(x, gain_ref[...], mod_ref, pl.program_id(0) * tm, ctx_len)
    hb = h.astype(BF16)
    acc = jnp.zeros(x.shape, F32)
    for j in range(d_ff // fc):
        gt = _dot(hb, win_ref[:, j * fc:(j + 1) * fc])
        up = _dot(hb, win_ref[:, d_ff + j * fc:d_ff + (j + 1) * fc])
        acc = acc + _dot((_silu(gt) * up).astype(BF16), wout_ref[j * fc:(j + 1) * fc, :])
    out_ref[...] = x + (0.5 * gate) * acc


def _ffn_call(xa, mod, gain, w_in, w_out, *, tm, ctx_len):
    la, d = xa.shape
    d_ff = w_out.shape[0]
    fc = d_ff // 2
    kern = functools.partial(_ffn_kernel, tm=tm, ctx_len=ctx_len, d_ff=d_ff, fc=fc)
    return pl.pallas_call(
        kern,
        grid=(la // tm,),
        in_specs=[pl.BlockSpec((tm, d), lambda i: (i, 0)),
                  _const_spec((8, d)), _const_spec((1, d)),
                  _const_spec(w_in.shape), _const_spec(w_out.shape)],
        out_specs=pl.BlockSpec((tm, d), lambda i: (i, 0)),
        out_shape=jax.ShapeDtypeStruct((la, d), F32),
        input_output_aliases={0: 0},
        compiler_params=_params(("parallel",)),
        name="ffn",
    )(xa, mod, gain, w_in, w_out)


_SEC = (("q", 512), ("k", 128), ("v", 128), ("gqkv", 3 * GDN_W), ("z", GDN_W), ("ab", LANES),
        ("pin", POOL_WIDTH), ("gl", N_BRANCH * 1024))


def _sec_offsets():
    offs, o = {}, 0
    for name, width in _SEC:
        offs[name] = (o, width)
        o += width
    return offs, o


def _rope(t, cos, sin):
    n = t.shape[-1]
    lane = lax.broadcasted_iota(jnp.int32, t.shape, 1)
    swapped = jnp.where(lane % 32 < 16, pltpu.roll(t, n - 16, 1), pltpu.roll(t, 16, 1))
    reps = n // LANES
    c = jnp.concatenate([cos] * reps, axis=1) if reps > 1 else cos
    s = jnp.concatenate([sin] * reps, axis=1) if reps > 1 else sin
    return t * c + swapped * s


def _dup_halves(t):
    lo = lax.broadcasted_iota(jnp.int32, t.shape, 1) < LANES // 2
    r = pltpu.roll(t, LANES // 2, 1)
    return jnp.concatenate([jnp.where(lo, t, r), jnp.where(lo, r, t)], axis=1)


def _proj_kernel(x_ref, mod_ref, gain_ref, w_ref, cos_ref, sin_ref,
                 q_ref, k_ref, v_ref, gqkv_ref, z_ref, ab_ref, pin_ref, gate_ref, *, tm, ctx_len):
    offs, _ = _sec_offsets()
    x = x_ref[...]
    h, _ = _norm_mod(x, gain_ref[...], mod_ref, pl.program_id(0) * tm, ctx_len)
    hb = h.astype(BF16)

    def sec(name, out_ref, fn):
        o, width = offs[name]
        step = min(width, COL_CHUNK)
        for a in range(0, width, step):
            out_ref[:, a:a + step] = fn(_dot(hb, w_ref[:, o + a:o + a + step])).astype(out_ref.dtype)

    cos, sin = cos_ref[...], sin_ref[...]
    sec("q", q_ref, lambda t: _rope(t, cos, sin) * (ATTN_HEAD_DIM ** -0.5))
    o, width = offs["k"]
    k_ref[...] = _dup_halves(_rope(_dot(hb, w_ref[:, o:o + width]), cos, sin)).astype(BF16)
    o, width = offs["v"]
    v_ref[...] = _dup_halves(_dot(hb, w_ref[:, o:o + width])).astype(BF16)
    ident = lambda t: t
    sec("gqkv", gqkv_ref, ident)
    sec("z", z_ref, ident)
    sec("ab", ab_ref, ident)
    sec("pin", pin_ref, ident)
    sec("gl", gate_ref, jax.nn.sigmoid)


def _proj_call(xa, mod, gain, w, cos, sin, *, tm, ctx_len):
    la, d = xa.shape
    offs, _ = _sec_offsets()
    kern = functools.partial(_proj_kernel, tm=tm, ctx_len=ctx_len)
    row = lambda width: pl.BlockSpec((tm, width), lambda i: (i, 0))
    outs = (("q", 512, BF16), ("k", 2 * LANES, BF16), ("v", 2 * LANES, BF16), ("gqkv", 3 * GDN_W, F32),
            ("z", GDN_W, BF16), ("ab", LANES, F32), ("pin", POOL_WIDTH, F32), ("gl", offs["gl"][1], BF16))
    return pl.pallas_call(
        kern,
        grid=(la // tm,),
        in_specs=[row(d), _const_spec((8, d)), _const_spec((1, d)), _const_spec(w.shape),
                  row(LANES), row(LANES)],
        out_specs=[row(width) for _, width, _ in outs],
        out_shape=[jax.ShapeDtypeStruct((la, width), dt) for _, width, dt in outs],
        compiler_params=_params(("parallel",)),
        name="proj",
    )(xa, mod, gain, w, cos, sin)


def _attn_kernel(sink_ref, q_ref, kp_ref, kc_ref, kn_ref, vp_ref, vc_ref, vn_ref, kx_ref, vx_ref,
                 o_ref, *, ctx_blocks, n_blocks):
    b = pl.program_id(0)
    blk = ATTN_BLOCK
    is_lat = b >= ctx_blocks
    prev_ok = b - 1 >= ctx_blocks
    next_ok = jnp.logical_and(is_lat, b + 1 < n_blocks)
    ri = lax.broadcasted_iota(jnp.int32, (blk, blk), 0)
    ci = lax.broadcasted_iota(jnp.int32, (blk, blk), 1)
    ninf = -jnp.inf
    bias = jnp.concatenate(
        [jnp.where(jnp.logical_and(ci >= ri, prev_ok), 0.0, ninf),
         jnp.where(jnp.broadcast_to(is_lat, (blk, blk)), 0.0, ninf),
         jnp.where(jnp.logical_and(ci <= ri, next_ok), 0.0, ninf)], axis=1)
    lo = lax.broadcasted_iota(jnp.int32, (blk, LANES), 1) < LANES // 2
    pairs = ATTN_HEADS // ATTN_KV_HEADS // 2
    for hk in range(ATTN_KV_HEADS):
        ks = slice(hk * LANES, (hk + 1) * LANES)
        kw = jnp.concatenate([kp_ref[:, ks], kc_ref[:, ks], kn_ref[:, ks]], axis=0)
        vw = jnp.concatenate([vp_ref[:, ks], vc_ref[:, ks], vn_ref[:, ks]], axis=0)
        kx, vx = kx_ref[:, ks], vx_ref[:, ks]
        q_parts, sinks = [], []
        for j in range(pairs):
            col = (hk * pairs + j) * LANES
            qp = q_ref[:, col:col + LANES]
            q_parts += [jnp.where(lo, qp, jnp.zeros_like(qp)), jnp.where(lo, jnp.zeros_like(qp), qp)]
            sinks += [sink_ref[(hk * pairs + j) * 2], sink_ref[(hk * pairs + j) * 2 + 1]]
        q4 = jnp.concatenate(q_parts, axis=0)
        s_win = _dot_nt(q4, kw)
        s_ctx = _dot_nt(q4, kx)
        p_win, p_ctx, inv = [], [], []
        for j, sk in enumerate(sinks):
            rs = slice(j * blk, (j + 1) * blk)
            sw = s_win[rs] + bias
            sx = s_ctx[rs]
            m = jnp.maximum(jnp.maximum(jnp.max(sw, axis=-1, keepdims=True),
                                        jnp.max(sx, axis=-1, keepdims=True)), sk)
            pw = jnp.exp(sw - m)
            px = jnp.exp(sx - m)
            denom = jnp.sum(pw, axis=-1, keepdims=True) + jnp.sum(px, axis=-1, keepdims=True) + jnp.exp(sk - m)
            p_win.append(pw.astype(BF16))
            p_ctx.append(px.astype(BF16))
            inv.append(1.0 / denom)
        r = _dot(jnp.concatenate(p_win, axis=0), vw) + _dot(jnp.concatenate(p_ctx, axis=0), vx)
        for j in range(pairs):
            r0 = r[(2 * j) * blk:(2 * j + 1) * blk] * inv[2 * j]
            r1 = r[(2 * j + 1) * blk:(2 * j + 2) * blk] * inv[2 * j + 1]
            col = (hk * pairs + j) * LANES
            o_ref[:, col:col + LANES] = jnp.where(lo, r0, r1).astype(BF16)


def _attn_call(sink, q, k, v, *, ctx_len):
    la = q.shape[0]
    blk = ATTN_BLOCK
    nb = la // blk
    cb = ctx_len // blk
    kern = functools.partial(_attn_kernel, ctx_blocks=cb, n_blocks=nb)
    kvw = k.shape[1]
    prev = pl.BlockSpec((blk, kvw), lambda b: (jnp.maximum(b - 1, 0), 0))
    cur = pl.BlockSpec((blk, kvw), lambda b: (b, 0))
    nxt = pl.BlockSpec((blk, kvw), lambda b: (jnp.minimum(b + 1, nb - 1), 0))
    cx = pl.BlockSpec((ctx_len, kvw), lambda b: (0, 0))
    return pl.pallas_call(
        kern,
        grid=(nb,),
        in_specs=[pl.BlockSpec(memory_space=pltpu.SMEM),
                  pl.BlockSpec((blk, q.shape[1]), lambda b: (b, 0)),
                  prev, cur, nxt, prev, cur, nxt, cx, cx],
        out_specs=pl.BlockSpec((blk, q.shape[1]), lambda b: (b, 0)),
        out_shape=jax.ShapeDtypeStruct(q.shape, BF16),
        compiler_params=_params(("parallel",)),
        name="attn",
    )(sink, q, k, k, k, v, v, v, k, v)


def _tri_inverse_masks(n):
    ri = lax.broadcasted_iota(jnp.int32, (n, n), 0)
    ci = lax.broadcasted_iota(jnp.int32, (n, n), 1)
    eye = (ri == ci).astype(F32)
    same = lambda s: (ri // s) == (ci // s)
    offs, s = [], GDN_BASE
    while s < n:
        offs.append(jnp.logical_and(same(2 * s), jnp.logical_not(same(s))))
        s *= 2
    return eye, same(GDN_BASE), offs


def _tri_inverse(a_list, masks):
    eye, diag, offs = masks
    a0 = [jnp.where(diag, a, 0.0) for a in a_list]
    a0b = [x.astype(BF16) for x in a0]
    t = [eye - x for x in a0]
    p = [_dot(x, x) for x in a0b]
    s = 2
    while True:
        pb = [x.astype(BF16) for x in p]
        t = [ti + _dot(ti.astype(BF16), pi) for ti, pi in zip(t, pb)]
        s *= 2
        if s >= GDN_BASE:
            break
        p = [_dot(x, x) for x in pb]
    for off in offs:
        tb = [x.astype(BF16) for x in t]
        mid = [_dot(ti, jnp.where(off, a, 0.0).astype(BF16)).astype(BF16) for ti, a in zip(tb, a_list)]
        t = [ti - _dot(mi, tbi) for ti, mi, tbi in zip(t, mid, tb)]
    return t


def _gdn_prep_kernel(prev_ref, cur_ref, next_ref, ab_ref, convw_ref, abp_ref,
                     u_ref, a1_ref, a2_ref, gt_ref, xs_ref, *, tg, ctx_len, total):
    i = pl.program_id(0)
    row0 = i * tg
    first = jnp.logical_or(row0 == 0, row0 == ctx_len)
    last = jnp.logical_or(row0 + tg == ctx_len, row0 + tg == total)
    xs_ref[0:HALO, :] = jnp.where(first, 0.0, prev_ref[...])
    xs_ref[HALO:HALO + tg, :] = cur_ref[...]
    xs_ref[HALO + tg:HALO + tg + HALO, :] = jnp.where(last, 0.0, next_ref[...])
    pad = (GDN_CONV - 1) // 2
    dk = GDN_HEAD_DIM
    c_len = GDN_CHUNK

    def conv_silu(col, r0):
        ls = slice(col * dk, (col + 1) * dk)
        acc = xs_ref[pl.ds(HALO - pad + r0, c_len), ls] * convw_ref[0:1, ls]
        for j in range(1, GDN_CONV):
            acc = acc + xs_ref[pl.ds(HALO - pad + j + r0, c_len), ls] * convw_ref[j:j + 1, ls]
        return _silu(acc)

    def l2n(t):
        return t * lax.rsqrt(jnp.sum(t * t, axis=-1, keepdims=True) + EPS)

    ri = lax.broadcasted_iota(jnp.int32, (c_len, c_len), 0)
    ci = lax.broadcasted_iota(jnp.int32, (c_len, c_len), 1)
    strict = (ci < ri, ci > ri)
    incl = (ci <= ri, ci >= ri)
    tri = (incl[0].astype(F32), incl[1].astype(F32))
    last_row = (c_len - 1, 0)
    masks = _tri_inverse_masks(c_len)

    for c in range(tg // c_len):
        rs = slice(c * c_len, (c + 1) * c_len)
        abv = ab_ref[rs, :]
        g_all = -jnp.exp(abp_ref[1:2, :]) * jax.nn.softplus(abv + abp_ref[0:1, :])
        b_all = jax.nn.sigmoid(abv)
        gcs = [_dot(tri[d], g_all, HIGHEST) for d in range(2)]
        gct = [jnp.transpose(x) for x in gcs]

        heads = range(GDN_HEADS)
        qs = [l2n(conv_silu(h, rs.start)) * (dk ** -0.5) for h in heads]
        ks = [l2n(conv_silu(GDN_HEADS + h, rs.start)) for h in heads]
        vs = [conv_silu(2 * GDN_HEADS + h, rs.start) for h in heads]
        kts = [jnp.transpose(k) for k in ks]
        ktb = [x.astype(BF16) for x in kts]
        kk = [_dot(k.astype(BF16), t) for k, t in zip(ks, ktb)]
        qkt = [_dot(q.astype(BF16), t) for q, t in zip(qs, ktb)]

        units = [(h, d) for h in heads for d in range(2)]
        gc = [gcs[d][:, d * GDN_HEADS + h:d * GDN_HEADS + h + 1] for h, d in units]
        gr = [gct[d][d * GDN_HEADS + h:d * GDN_HEADS + h + 1, :] for h, d in units]
        beta = [b_all[:, (2 + d) * GDN_HEADS + h:(2 + d) * GDN_HEADS + h + 1] for h, d in units]
        e = [jnp.exp(jnp.where(incl[d], c_ - r_, -jnp.inf)) for (h, d), c_, r_ in zip(units, gc, gr)]
        a = [jnp.where(strict[d], b_ * kk[h] * e_, 0.0) for (h, d), b_, e_ in zip(units, beta, e)]
        tb = [x.astype(BF16) for x in _tri_inverse(a, masks)]
        egc = [jnp.exp(x) for x in gc]
        us = [_dot(t, (b_ * vs[h]).astype(BF16)) for (h, d), t, b_ in zip(units, tb, beta)]
        ws = [_dot(t, (b_ * g_ * ks[h]).astype(BF16)) for (h, d), t, b_, g_ in zip(units, tb, beta, egc)]
        for n, (h, d) in enumerate(units):
            hs = slice(h * dk, (h + 1) * dk)
            g_last = gc[n][last_row[d]:last_row[d] + 1, :]
            u_ref[d, rs, hs] = us[n]
            a1_ref[d, c, 0:c_len, hs] = ws[n].astype(BF16)
            a1_ref[d, c, c_len:2 * c_len, hs] = (qs[h] * egc[n]).astype(BF16)
            a2_ref[d, c, 0:c_len, hs] = (qkt[h] * e[n]).astype(BF16)
            a2_ref[d, c, c_len:2 * c_len, hs] = (kts[h] * jnp.exp(g_last - gr[n])).astype(BF16)
            gt_ref[d, c, h:h + 1, :] = jnp.broadcast_to(jnp.exp(g_last), (1, LANES))


def _gdn_prep_call(gqkv, ab, conv_w, abp, *, tg, ctx_len):
    la, wq = gqkv.shape
    c_len = GDN_CHUNK
    nch = la // c_len
    cpt = tg // c_len
    hb = tg // HALO
    nhalo = la // HALO
    kern = functools.partial(_gdn_prep_kernel, tg=tg, ctx_len=ctx_len, total=la)
    stacked = pl.BlockSpec((2, cpt, 2 * c_len, GDN_W), lambda i: (0, i, 0, 0))
    return pl.pallas_call(
        kern,
        grid=(la // tg,),
        in_specs=[pl.BlockSpec((HALO, wq), lambda i: (jnp.maximum(i * hb - 1, 0), 0)),
                  pl.BlockSpec((tg, wq), lambda i: (i, 0)),
                  pl.BlockSpec((HALO, wq), lambda i: (jnp.minimum((i + 1) * hb, nhalo - 1), 0)),
                  pl.BlockSpec((tg, LANES), lambda i: (i, 0)),
                  _const_spec(conv_w.shape), _const_spec(abp.shape)],
        out_specs=[pl.BlockSpec((2, tg, GDN_W), lambda i: (0, i, 0)), stacked, stacked,
                   pl.BlockSpec((2, cpt, GDN_HEADS, LANES), lambda i: (0, i, 0, 0))],
        out_shape=[jax.ShapeDtypeStruct((2, la, GDN_W), F32),
                   jax.ShapeDtypeStruct((2, nch, 2 * c_len, GDN_W), BF16),
                   jax.ShapeDtypeStruct((2, nch, 2 * c_len, GDN_W), BF16),
                   jax.ShapeDtypeStruct((2, nch, GDN_HEADS, LANES), F32)],
        scratch_shapes=[pltpu.VMEM((tg + 2 * HALO, wq), F32)],
        compiler_params=_params(("parallel",)),
        name="gdn_prep",
    )(gqkv, gqkv, gqkv, ab, conv_w, abp)


def _gdn_scan_kernel(uf, ub, a1f, a1b, a2f, a2b, gtf, gtb, of_ref, ob_ref, s_ref):
    @pl.when(pl.program_id(0) == 0)
    def _():
        s_ref[...] = jnp.zeros(s_ref.shape, F32)

    dk = GDN_HEAD_DIM
    c_len = GDN_CHUNK
    dirs = ((uf, a1f, a2f, gtf, of_ref), (ub, a1b, a2b, gtb, ob_ref))
    chains = [(d, h) for d in range(2) for h in range(GDN_HEADS)]
    hs = [slice(h * dk, (h + 1) * dk) for h in range(GDN_HEADS)]
    s = [s_ref[d * GDN_HEADS + h] for d, h in chains]
    r1 = [_dot(dirs[d][1][:, hs[h]], s_.astype(BF16)) for (d, h), s_ in zip(chains, s)]
    vb = [(dirs[d][0][:, hs[h]] - r[0:c_len]).astype(BF16) for (d, h), r in zip(chains, r1)]
    r2 = [_dot(dirs[d][2][:, hs[h]], v_) for (d, h), v_ in zip(chains, vb)]
    for (d, h), s_, ra, rb in zip(chains, s, r1, r2):
        dirs[d][4][:, hs[h]] = ra[c_len:2 * c_len] + rb[0:c_len]
        s_ref[d * GDN_HEADS + h] = s_ * dirs[d][3][h:h + 1, :] + rb[c_len:2 * c_len]


def _gdn_scan_call(u, a1, a2, gt, *, ctx_len):
    la = u.shape[1]
    c_len = GDN_CHUNK
    nch = la // c_len
    ncc = ctx_len // c_len

    def bwd(n):
        return jnp.where(n < ncc, ncc - 1 - n, nch + ncc - 1 - n)

    def both(make):
        return [make(0, lambda n: n), make(1, bwd)]

    u_specs = both(lambda d, f: pl.BlockSpec((None, c_len, GDN_W), lambda n: (d, f(n), 0)))
    a_specs = both(lambda d, f: pl.BlockSpec((None, None, 2 * c_len, GDN_W), lambda n: (d, f(n), 0, 0)))
    g_specs = both(lambda d, f: pl.BlockSpec((None, None, GDN_HEADS, LANES), lambda n: (d, f(n), 0, 0)))
    return pl.pallas_call(
        _gdn_scan_kernel,
        grid=(nch,),
        in_specs=u_specs + a_specs + a_specs + g_specs,
        out_specs=[pl.BlockSpec((c_len, GDN_W), lambda n: (n, 0)),
                   pl.BlockSpec((c_len, GDN_W), lambda n: (bwd(n), 0))],
        out_shape=[jax.ShapeDtypeStruct((la, GDN_W), F32), jax.ShapeDtypeStruct((la, GDN_W), F32)],
        scratch_shapes=[pltpu.VMEM((2 * GDN_HEADS, GDN_HEAD_DIM, GDN_HEAD_DIM), F32)],
        compiler_params=_params(("arbitrary",)),
        name="gdn_scan",
    )(u, u, a1, a1, a2, a2, gt, gt)


def _merge_kernel(x_ref, attn_ref, of_ref, ob_ref, z_ref, pprev_ref, pcur_ref, pnext_ref, gate_ref,
                  mod_ref, gnorm_ref, poolw_ref, pscale_ref, wba_ref, wbg_ref, wbp_ref, wout_ref,
                  out_ref, ps_ref, *, tm, ctx_len, total):
    i = pl.program_id(0)
    row0 = i * tm
    first = jnp.logical_or(row0 == 0, row0 == ctx_len)
    last = jnp.logical_or(row0 + tm == ctx_len, row0 + tm == total)
    ps_ref[0:HALO, :] = jnp.where(first, 0.0, pprev_ref[...])
    ps_ref[HALO:HALO + tm, :] = pcur_ref[...]
    ps_ref[HALO + tm:HALO + tm + HALO, :] = jnp.where(last, 0.0, pnext_ref[...])

    rows = row0 + lax.broadcasted_iota(jnp.int32, (tm, 1), 0)
    is_ctx = rows < ctx_len
    seg_pos = jnp.where(is_ctx, rows, rows - ctx_len)
    seg_len = jnp.where(is_ctx, ctx_len, total - ctx_len)

    pooled = []
    for gi, win in enumerate(POOL_SIZES):
        ls = slice(gi * POOL_GROUP, (gi + 1) * POOL_GROUP)
        s = ps_ref[pl.ds(HALO - win // 2, tm), ls]
        for o in range(1, win):
            s = s + ps_ref[pl.ds(HALO - win // 2 + o, tm), ls]
        cnt = (jnp.minimum(seg_pos + win // 2, seg_len) - jnp.maximum(seg_pos - win // 2, 0)).astype(F32)
        r = s / cnt - pcur_ref[:, ls]
        pooled.append(_dot(r.astype(BF16), poolw_ref[gi]) * pscale_ref[:, ls])
    pool_o = jnp.concatenate(pooled, axis=1).astype(BF16)

    gdn = []
    for h in range(GDN_HEADS):
        hs = slice(h * GDN_HEAD_DIM, (h + 1) * GDN_HEAD_DIM)
        o = of_ref[:, hs] + ob_ref[:, hs]
        o = o * lax.rsqrt(jnp.mean(o * o, axis=-1, keepdims=True) + EPS) * gnorm_ref[...]
        gdn.append(o * _silu(z_ref[:, hs].astype(F32)))
    gdn_o = jnp.concatenate(gdn, axis=1).astype(BF16)

    d = x_ref.shape[1]
    merged = (gate_ref[:, 0:d].astype(F32) * _dot(attn_ref[...], wba_ref[...])
              + gate_ref[:, d:2 * d].astype(F32) * _dot(gdn_o, wbg_ref[...])
              + gate_ref[:, 2 * d:3 * d].astype(F32) * _dot(pool_o, wbp_ref[...]))
    y = _dot(merged.astype(BF16), wout_ref[...])
    gate = jnp.where(is_ctx, mod_ref[5:6, :], mod_ref[2:3, :])
    out_ref[...] = x_ref[...] + gate * y


def _merge_call(xa, attn_o, o_f, o_b, z, pin, gates, mod, gnorm, pool_w, pool_scale, wba, wbg, wbp, wout,
                *, tm, ctx_len):
    la, d = xa.shape
    hb = tm // HALO
    nhalo = la // HALO
    pw = pin.shape[1]
    kern = functools.partial(_merge_kernel, tm=tm, ctx_len=ctx_len, total=la)
    row = lambda width: pl.BlockSpec((tm, width), lambda i: (i, 0))
    return pl.pallas_call(
        kern,
        grid=(la // tm,),
        in_specs=[row(d), row(attn_o.shape[1]), row(GDN_W), row(GDN_W), row(GDN_W),
                  pl.BlockSpec((HALO, pw), lambda i: (jnp.maximum(i * hb - 1, 0), 0)),
                  row(pw),
                  pl.BlockSpec((HALO, pw), lambda i: (jnp.minimum((i + 1) * hb, nhalo - 1), 0)),
                  row(gates.shape[1]),
                  _const_spec((8, d)), _const_spec(gnorm.shape), _const_spec(pool_w.shape),
                  _const_spec(pool_scale.shape), _const_spec(wba.shape), _const_spec(wbg.shape),
                  _const_spec(wbp.shape), _const_spec(wout.shape)],
        out_specs=row(d),
        out_shape=jax.ShapeDtypeStruct((la, d), F32),
        input_output_aliases={0: 0},
        scratch_shapes=[pltpu.VMEM((tm + 2 * HALO, pw), F32)],
        compiler_params=_params(("parallel",)),
        name="merge",
    )(xa, attn_o, o_f, o_b, z, pin, pin, pin, gates, mod, gnorm, pool_w, pool_scale, wba, wbg, wbp, wout)


def _final_norm_kernel(x_ref, gain_ref, out_ref):
    x = x_ref[...]
    out_ref[...] = x * lax.rsqrt(jnp.mean(x * x, axis=-1, keepdims=True) + EPS) * gain_ref[...]


def _final_norm_call(xa, gain, *, tm, ctx_len):
    la, d = xa.shape
    skip = ctx_len // tm
    return pl.pallas_call(
        _final_norm_kernel,
        grid=((la - ctx_len) // tm,),
        in_specs=[pl.BlockSpec((tm, d), lambda i: (i + skip, 0)), _const_spec((1, d))],
        out_specs=pl.BlockSpec((tm, d), lambda i: (i, 0)),
        out_shape=jax.ShapeDtypeStruct((la - ctx_len, d), F32),
        compiler_params=_params(("parallel",)),
        name="final_norm",
    )(xa, gain)


def _rope_tables(seq, ctx_len):
    quarter = ATTN_HEAD_DIM // 4
    rows = seq // GRID_W
    inv = ROPE_THETA ** (-jnp.arange(quarter, dtype=F32) / quarter)

    def axis_tables(n):
        ang = jnp.arange(n, dtype=F32)[:, None] * inv[None, :]
        c, s = jnp.cos(ang), jnp.sin(ang)
        return jnp.concatenate([c, c], axis=1), jnp.concatenate([-s, s], axis=1)

    cr, sr = axis_tables(rows)
    cc, sc = axis_tables(GRID_W)

    def full(tr, tc):
        t = jnp.concatenate([jnp.broadcast_to(tr[:, None, :], (rows, GRID_W, 2 * quarter)),
                             jnp.broadcast_to(tc[None, :, :], (rows, GRID_W, 2 * quarter))], axis=-1)
        t = t.reshape(seq, ATTN_HEAD_DIM)
        return jnp.concatenate([t] * (LANES // ATTN_HEAD_DIM), axis=1)

    cos = jnp.concatenate([jnp.ones((ctx_len, LANES), F32), full(cr, cc)], axis=0)
    sin = jnp.concatenate([jnp.zeros((ctx_len, LANES), F32), full(sr, sc)], axis=0)
    return cos, sin


def _pack_w_in(w_in_l):
    sizes = (512, 128, 128, 3 * GDN_W, GDN_W, 4 * GDN_HEADS, POOL_WIDTH, N_BRANCH * 1024)
    parts, o = [], 0
    for sz, (_, width) in zip(sizes, _SEC):
        p = w_in_l[:, o:o + sz]
        if width != sz:
            p = jnp.pad(p, ((0, 0), (0, width - sz)))
        parts.append(p)
        o += sz
    return jnp.concatenate(parts, axis=1).astype(BF16)


def kernel(x, c, ctx, c_ctx, w_ada, b_ada, norm_ffn1, w_ffn1_in, w_ffn1_out, norm_mix, w_in, attn_sink,
           gdn_conv, gdn_a_log, gdn_dt_bias, gdn_norm, pool_w, pool_scale, w_branch_attn, w_branch_gdn,
           w_branch_pool, w_out, norm_ffn2, w_ffn2_in, w_ffn2_out, final_norm):
    batch, seq, d = x.shape
    assert batch == 1, "single-sequence kernel"
    ctx_len = ctx.shape[1]
    depth = w_ada.shape[0]
    la = ctx_len + seq
    tm = 256
    tm_big = 640
    assert ctx_len % tm == 0 and seq % tm == 0 and seq % GRID_W == 0 and la % tm_big == 0

    xa = jnp.concatenate([ctx[0], x[0]], axis=0)
    cvec = jnp.zeros((8, d), F32).at[0].set(c[0]).at[1].set(c_ctx)
    mods = _ada_call(cvec, w_ada, b_ada).reshape(depth, 8, N_MOD, d)
    cos, sin = _rope_tables(seq, ctx_len)

    for l in range(depth):
        def mod_rows(j):
            m = jnp.concatenate([mods[l, 0, 3 * j:3 * j + 3], mods[l, 1, 3 * j:3 * j + 3]], axis=0)
            return jnp.pad(m, ((0, 2), (0, 0)))

        xa = _ffn_call(xa, mod_rows(0), norm_ffn1[l][None], w_ffn1_in[l].astype(BF16),
                       w_ffn1_out[l].astype(BF16), tm=tm_big, ctx_len=ctx_len)

        mod_mix = mod_rows(1)
        q, k, v, gqkv, z, ab, pin, gates = _proj_call(xa, mod_mix, norm_mix[l][None], _pack_w_in(w_in[l]),
                                                      cos, sin, tm=tm_big, ctx_len=ctx_len)
        attn_o = _attn_call(attn_sink[l], q, k, v, ctx_len=ctx_len)
        conv_w = jnp.pad(gdn_conv[l], ((0, 8 - GDN_CONV), (0, 0)))
        abp = jnp.zeros((8, LANES), F32)
        abp = abp.at[0, :2 * GDN_HEADS].set(gdn_dt_bias[l].reshape(-1))
        abp = abp.at[1, :2 * GDN_HEADS].set(gdn_a_log[l].reshape(-1))
        u, a1, a2, gt = _gdn_prep_call(gqkv, ab, conv_w, abp, tg=tm, ctx_len=ctx_len)
        o_f, o_b = _gdn_scan_call(u, a1, a2, gt, ctx_len=ctx_len)
        xa = _merge_call(xa, attn_o, o_f, o_b, z, pin, gates, mod_mix, gdn_norm[l][None],
                         pool_w[l].astype(BF16), pool_scale[l][None],
                         w_branch_attn[l].astype(BF16), w_branch_gdn[l].astype(BF16),
                         w_branch_pool[l].astype(BF16), w_out[l].astype(BF16), tm=tm, ctx_len=ctx_len)

        xa = _ffn_call(xa, mod_rows(2), norm_ffn2[l][None], w_ffn2_in[l].astype(BF16),
                       w_ffn2_out[l].astype(BF16), tm=tm_big, ctx_len=ctx_len)

    out = _final_norm_call(xa, final_norm[None], tm=tm, ctx_len=ctx_len)
    return out[None]
```

```python
import functools

import jax
import jax.numpy as jnp
from jax import lax
from jax.experimental import pallas as pl
from jax.experimental.pallas import tpu as pltpu

F32 = jnp.float32
BF16 = jnp.bfloat16
HIGHEST = lax.Precision.HIGHEST

GRID_W = 64
ATTN_HEADS = 8
ATTN_KV_HEADS = 2
ATTN_HEAD_DIM = 64
ATTN_BLOCK = 128
ROPE_THETA = 10000.0
GDN_HEADS = 4
GDN_HEAD_DIM = 128
GDN_W = GDN_HEADS * GDN_HEAD_DIM
GDN_CONV = 5
GDN_CHUNK = 128
GDN_BASE = 16
POOL_SIZES = (2, 4, 8, 16)
POOL_GROUP = 128
POOL_WIDTH = POOL_GROUP * len(POOL_SIZES)
N_MOD = 9
N_BRANCH = 3
EPS = 1e-6

LANES = 128
HALO = 8
VMEM_LIMIT = 56 * 1024 * 1024
COL_CHUNK = 512


def _dot(a, b, precision=None):
    return jnp.dot(a, b, preferred_element_type=F32, precision=precision)


def _dot_nt(a, b):
    return lax.dot_general(a, b, (((1,), (1,)), ((), ())), preferred_element_type=F32)


def _silu(x):
    return x * jax.nn.sigmoid(x)


def _params(sem):
    return pltpu.CompilerParams(dimension_semantics=sem, vmem_limit_bytes=VMEM_LIMIT)


def _const_spec(shape):
    nd = len(shape)
    return pl.BlockSpec(shape, lambda *_: (0,) * nd, pipeline_mode=pl.Buffered(1))


def _layer_spec(arr, l):
    nd = arr.ndim - 1
    return pl.BlockSpec((None,) + arr.shape[1:], lambda *_: (l,) + (0,) * nd, pipeline_mode=pl.Buffered(1))


def _mod_spec(d, l, j):
    return pl.BlockSpec((None, 8, 3 * d), lambda *_: (l, 0, j), pipeline_mode=pl.Buffered(1))


def _norm_mod(x, gain, mod_ref, rows, ctx_len):
    d = x.shape[1]
    y = x * lax.rsqrt(jnp.mean(x * x, axis=-1, keepdims=True) + EPS) * gain
    is_ctx = rows < ctx_len
    shift = jnp.where(is_ctx, mod_ref[1:2, 0:d], mod_ref[0:1, 0:d])
    scale = jnp.where(is_ctx, mod_ref[1:2, d:2 * d], mod_ref[0:1, d:2 * d])
    return y * (1.0 + scale) + shift


def _mod_gate(mod_ref, rows, ctx_len, d):
    return jnp.where(rows < ctx_len, mod_ref[1:2, 2 * d:3 * d], mod_ref[0:1, 2 * d:3 * d])


def _row_ids(row0, n):
    return row0 + lax.broadcasted_iota(jnp.int32, (n, 1), 0)


def _ada_kernel(c_ref, w_ref, b_ref, out_ref):
    sc = _silu(c_ref[...])
    out_ref[...] = _dot(sc, w_ref[...], HIGHEST) + b_ref[...]


def _ada_call(cvec, w_ada, b_ada):
    depth, d, n = w_ada.shape
    tn = n // 4
    return pl.pallas_call(
        _ada_kernel,
        grid=(depth, n // tn),
        in_specs=[pl.BlockSpec((8, d), lambda l, j: (0, 0)),
                  pl.BlockSpec((None, d, tn), lambda l, j: (l, 0, j)),
                  pl.BlockSpec((None, 1, tn), lambda l, j: (l, 0, j))],
        out_specs=pl.BlockSpec((None, 8, tn), lambda l, j: (l, 0, j)),
        out_shape=jax.ShapeDtypeStruct((depth, 8, n), F32),
        compiler_params=_params(("arbitrary", "arbitrary")),
        name="ada_mod",
    )(cvec, w_ada, b_ada.reshape(depth, 1, n))


def _ffn_kernel(x_ref, mod_ref, gain_ref, win_ref, wout_ref, out_ref, *, tm, ctx_len, d_ff, fc):
    x = x_ref[...]
    rows = _row_ids(pl.program_id(0) * tm, tm)
    hb = _norm_mod(x, gain_ref[...], mod_ref, rows, ctx_len).astype(BF16)
    acc = jnp.zeros(x.shape, F32)
    for j in range(d_ff // fc):
        gt = _dot(hb, win_ref[:, j * fc:(j + 1) * fc])
        up = _dot(hb, win_ref[:, d_ff + j * fc:d_ff + (j + 1) * fc])
        acc = acc + _dot((_silu(gt) * up).astype(BF16), wout_ref[j * fc:(j + 1) * fc, :])
    out_ref[...] = x + (0.5 * _mod_gate(mod_ref, rows, ctx_len, x.shape[1])) * acc


def _ffn_call(xa, mods, gain, w_in, w_out, *, l, j, tm, ctx_len):
    la, d = xa.shape
    d_ff = w_out.shape[1]
    fc = d_ff // 2
    kern = functools.partial(_ffn_kernel, tm=tm, ctx_len=ctx_len, d_ff=d_ff, fc=fc)
    return pl.pallas_call(
        kern,
        grid=(la // tm,),
        in_specs=[pl.BlockSpec((tm, d), lambda i: (i, 0)),
                  _mod_spec(d, l, j), _layer_spec(gain, l), _layer_spec(w_in, l), _layer_spec(w_out, l)],
        out_specs=pl.BlockSpec((tm, d), lambda i: (i, 0)),
        out_shape=jax.ShapeDtypeStruct((la, d), F32),
        input_output_aliases={0: 0},
        compiler_params=_params(("parallel",)),
        name="ffn",
    )(xa, mods, gain, w_in, w_out)


_SEC = (("q", 512), ("k", 128), ("v", 128), ("gqkv", 3 * GDN_W), ("z", GDN_W), ("ab", LANES),
        ("pin", POOL_WIDTH), ("gl", N_BRANCH * 1024))
_SEC_SRC = (512, 128, 128, 3 * GDN_W, GDN_W, 4 * GDN_HEADS, POOL_WIDTH, N_BRANCH * 1024)


def _sec_offsets():
    offs, o = {}, 0
    for name, width in _SEC:
        offs[name] = (o, width)
        o += width
    return offs, o


def _rope(t, cos, sin):
    n = t.shape[-1]
    lane = lax.broadcasted_iota(jnp.int32, t.shape, 1)
    swapped = jnp.where(lane % 32 < 16, pltpu.roll(t, n - 16, 1), pltpu.roll(t, 16, 1))
    reps = n // LANES
    c = jnp.concatenate([cos] * reps, axis=1) if reps > 1 else cos
    s = jnp.concatenate([sin] * reps, axis=1) if reps > 1 else sin
    return t * c + swapped * s


def _dup_halves(t):
    lo = lax.broadcasted_iota(jnp.int32, t.shape, 1) < LANES // 2
    r = pltpu.roll(t, LANES // 2, 1)
    return jnp.concatenate([jnp.where(lo, t, r), jnp.where(lo, r, t)], axis=1)


def _proj_kernel(x_ref, mod_ref, gain_ref, w_ref, cos_ref, sin_ref,
                 q_ref, k_ref, v_ref, gqkv_ref, z_ref, ab_ref, pin_ref, gate_ref, *, tm, ctx_len):
    offs, _ = _sec_offsets()
    rows = _row_ids(pl.program_id(0) * tm, tm)
    hb = _norm_mod(x_ref[...], gain_ref[...], mod_ref, rows, ctx_len).astype(BF16)

    def sec(name, out_ref, fn):
        o, width = offs[name]
        step = min(width, COL_CHUNK)
        for a in range(0, width, step):
            out_ref[:, a:a + step] = fn(_dot(hb, w_ref[:, o + a:o + a + step])).astype(out_ref.dtype)

    cos, sin = cos_ref[...], sin_ref[...]
    sec("q", q_ref, lambda t: _rope(t, cos, sin) * (ATTN_HEAD_DIM ** -0.5))
    o, width = offs["k"]
    k_ref[...] = _dup_halves(_rope(_dot(hb, w_ref[:, o:o + width]), cos, sin)).astype(BF16)
    o, width = offs["v"]
    v_ref[...] = _dup_halves(_dot(hb, w_ref[:, o:o + width])).astype(BF16)
    ident = lambda t: t
    sec("gqkv", gqkv_ref, ident)
    sec("z", z_ref, ident)
    sec("ab", ab_ref, ident)
    sec("pin", pin_ref, ident)
    sec("gl", gate_ref, jax.nn.sigmoid)


def _proj_call(xa, mods, gain, w, cos, sin, *, l, tm, ctx_len):
    la, d = xa.shape
    offs, _ = _sec_offsets()
    kern = functools.partial(_proj_kernel, tm=tm, ctx_len=ctx_len)
    row = lambda width: pl.BlockSpec((tm, width), lambda i: (i, 0))
    outs = ((512, BF16), (2 * LANES, BF16), (2 * LANES, BF16), (3 * GDN_W, F32), (GDN_W, BF16), (LANES, F32),
            (POOL_WIDTH, F32), (offs["gl"][1], BF16))
    return pl.pallas_call(
        kern,
        grid=(la // tm,),
        in_specs=[row(d), _mod_spec(d, l, 1), _layer_spec(gain, l), _layer_spec(w, l), row(LANES), row(LANES)],
        out_specs=[row(width) for width, _ in outs],
        out_shape=[jax.ShapeDtypeStruct((la, width), dt) for width, dt in outs],
        compiler_params=_params(("parallel",)),
        name="proj",
    )(xa, mods, gain, w, cos, sin)


def _attn_kernel(sink_ref, q_ref, kp_ref, kc_ref, kn_ref, vp_ref, vc_ref, vn_ref, kx_ref, vx_ref,
                 o_ref, *, l, ctx_blocks, n_blocks):
    b = pl.program_id(0)
    blk = ATTN_BLOCK
    is_lat = b >= ctx_blocks
    prev_ok = b - 1 >= ctx_blocks
    next_ok = jnp.logical_and(is_lat, b + 1 < n_blocks)
    ri = lax.broadcasted_iota(jnp.int32, (blk, blk), 0)
    ci = lax.broadcasted_iota(jnp.int32, (blk, blk), 1)
    ninf = -jnp.inf
    bias = jnp.concatenate(
        [jnp.where(jnp.logical_and(ci >= ri, prev_ok), 0.0, ninf),
         jnp.where(jnp.broadcast_to(is_lat, (blk, blk)), 0.0, ninf),
         jnp.where(jnp.logical_and(ci <= ri, next_ok), 0.0, ninf)], axis=1)
    lo = lax.broadcasted_iota(jnp.int32, (blk, LANES), 1) < LANES // 2
    pairs = ATTN_HEADS // ATTN_KV_HEADS // 2
    kv_groups = range(ATTN_KV_HEADS)
    scores, values, sinks = [], [], []
    for hk in kv_groups:
        ks = slice(hk * LANES, (hk + 1) * LANES)
        kw = jnp.concatenate([kp_ref[:, ks], kc_ref[:, ks], kn_ref[:, ks]], axis=0)
        vw = jnp.concatenate([vp_ref[:, ks], vc_ref[:, ks], vn_ref[:, ks]], axis=0)
        q_parts, sk = [], []
        for j in range(pairs):
            col = (hk * pairs + j) * LANES
            qp = q_ref[:, col:col + LANES]
            q_parts += [jnp.where(lo, qp, jnp.zeros_like(qp)), jnp.where(lo, jnp.zeros_like(qp), qp)]
            sk += [sink_ref[l, (hk * pairs + j) * 2], sink_ref[l, (hk * pairs + j) * 2 + 1]]
        q4 = jnp.concatenate(q_parts, axis=0)
        scores.append((_dot_nt(q4, kw), _dot_nt(q4, kx_ref[:, ks])))
        values.append((vw, vx_ref[:, ks]))
        sinks.append(sk)
    probs = []
    for hk in kv_groups:
        s_win, s_ctx = scores[hk]
        p_win, p_ctx, inv = [], [], []
        for j, sk in enumerate(sinks[hk]):
            rs = slice(j * blk, (j + 1) * blk)
            sw = s_win[rs] + bias
            sx = s_ctx[rs]
            m = jnp.maximum(jnp.maximum(jnp.max(sw, axis=-1, keepdims=True),
                                        jnp.max(sx, axis=-1, keepdims=True)), sk)
            pw = jnp.exp(sw - m)
            px = jnp.exp(sx - m)
            denom = jnp.sum(pw, axis=-1, keepdims=True) + jnp.sum(px, axis=-1, keepdims=True) + jnp.exp(sk - m)
            p_win.append(pw.astype(BF16))
            p_ctx.append(px.astype(BF16))
            inv.append(1.0 / denom)
        probs.append((jnp.concatenate(p_win, axis=0), jnp.concatenate(p_ctx, axis=0), inv))
    for hk in kv_groups:
        p_win, p_ctx, inv = probs[hk]
        vw, vx = values[hk]
        r = _dot(p_win, vw) + _dot(p_ctx, vx)
        for j in range(pairs):
            r0 = r[(2 * j) * blk:(2 * j + 1) * blk] * inv[2 * j]
            r1 = r[(2 * j + 1) * blk:(2 * j + 2) * blk] * inv[2 * j + 1]
            col = (hk * pairs + j) * LANES
            o_ref[:, col:col + LANES] = jnp.where(lo, r0, r1).astype(BF16)


def _attn_call(sink, q, k, v, *, l, ctx_len):
    la = q.shape[0]
    blk = ATTN_BLOCK
    nb = la // blk
    cb = ctx_len // blk
    kern = functools.partial(_attn_kernel, l=l, ctx_blocks=cb, n_blocks=nb)
    kvw = k.shape[1]
    prev = pl.BlockSpec((blk, kvw), lambda b: (jnp.maximum(b - 1, 0), 0))
    cur = pl.BlockSpec((blk, kvw), lambda b: (b, 0))
    nxt = pl.BlockSpec((blk, kvw), lambda b: (jnp.minimum(b + 1, nb - 1), 0))
    cx = pl.BlockSpec((ctx_len, kvw), lambda b: (0, 0))
    return pl.pallas_call(
        kern,
        grid=(nb,),
        in_specs=[pl.BlockSpec(memory_space=pltpu.SMEM),
                  pl.BlockSpec((blk, q.shape[1]), lambda b: (b, 0)),
                  prev, cur, nxt, prev, cur, nxt, cx, cx],
        out_specs=pl.BlockSpec((blk, q.shape[1]), lambda b: (b, 0)),
        out_shape=jax.ShapeDtypeStruct(q.shape, BF16),
        compiler_params=_params(("parallel",)),
        name="attn",
    )(sink, q, k, k, k, v, v, v, k, v)


def _tri_inverse_masks(n):
    ri = lax.broadcasted_iota(jnp.int32, (n, n), 0)
    ci = lax.broadcasted_iota(jnp.int32, (n, n), 1)
    eye = (ri == ci).astype(F32)
    same = lambda s: (ri // s) == (ci // s)
    offs, s = [], GDN_BASE
    while s < n:
        offs.append(jnp.logical_and(same(2 * s), jnp.logical_not(same(s))))
        s *= 2
    return eye, same(GDN_BASE), offs


def _tri_inverse(a_list, masks):
    eye, diag, offs = masks
    a0 = [jnp.where(diag, a, 0.0) for a in a_list]
    a0b = [x.astype(BF16) for x in a0]
    t = [eye - x for x in a0]
    p = [_dot(x, x) for x in a0b]
    s = 2
    while True:
        pb = [x.astype(BF16) for x in p]
        t = [ti + _dot(ti.astype(BF16), pi) for ti, pi in zip(t, pb)]
        s *= 2
        if s >= GDN_BASE:
            break
        p = [_dot(x, x) for x in pb]
    for off in offs:
        tb = [x.astype(BF16) for x in t]
        mid = [_dot(ti, jnp.where(off, a, 0.0).astype(BF16)).astype(BF16) for ti, a in zip(tb, a_list)]
        t = [ti - _dot(mi, tbi) for ti, mi, tbi in zip(t, mid, tb)]
    return t


def _gdn_prep_kernel(prev_ref, cur_ref, next_ref, ab_ref, convw_ref, abp_ref,
                     u_ref, a1_ref, a2_ref, gt_ref, xs_ref, *, tg, ctx_len, total):
    row0 = pl.program_id(0) * tg
    first = jnp.logical_or(row0 == 0, row0 == ctx_len)
    last = jnp.logical_or(row0 + tg == ctx_len, row0 + tg == total)
    xs_ref[0:HALO, :] = jnp.where(first, 0.0, prev_ref[...])
    xs_ref[HALO:HALO + tg, :] = cur_ref[...]
    xs_ref[HALO + tg:HALO + tg + HALO, :] = jnp.where(last, 0.0, next_ref[...])
    pad = (GDN_CONV - 1) // 2
    dk = GDN_HEAD_DIM
    c_len = GDN_CHUNK

    def conv_silu(col, r0):
        ls = slice(col * dk, (col + 1) * dk)
        acc = xs_ref[pl.ds(HALO - pad + r0, c_len), ls] * convw_ref[0:1, ls]
        for j in range(1, GDN_CONV):
            acc = acc + xs_ref[pl.ds(HALO - pad + j + r0, c_len), ls] * convw_ref[j:j + 1, ls]
        return _silu(acc)

    def l2n(t):
        return t * lax.rsqrt(jnp.sum(t * t, axis=-1, keepdims=True) + EPS)

    ri = lax.broadcasted_iota(jnp.int32, (c_len, c_len), 0)
    ci = lax.broadcasted_iota(jnp.int32, (c_len, c_len), 1)
    strict = (ci < ri, ci > ri)
    incl = (ci <= ri, ci >= ri)
    tri = (incl[0].astype(F32), incl[1].astype(F32))
    last_row = (c_len - 1, 0)
    masks = _tri_inverse_masks(c_len)
    heads = range(GDN_HEADS)
    hsl = [slice(h * dk, (h + 1) * dk) for h in heads]

    chunks = range(tg // c_len)
    g_all = [-jnp.exp(abp_ref[1:2, :]) * jax.nn.softplus(ab_ref[c * c_len:(c + 1) * c_len, :] + abp_ref[0:1, :])
             for c in chunks]
    b_all = [jax.nn.sigmoid(ab_ref[c * c_len:(c + 1) * c_len, :]) for c in chunks]
    gcs = [[_dot(tri[d], g_all[c], HIGHEST) for d in range(2)] for c in chunks]
    gct = [[jnp.transpose(x) for x in gcs[c]] for c in chunks]

    pairs = [(c, h) for c in chunks for h in heads]
    qs = {ch: l2n(conv_silu(ch[1], ch[0] * c_len)) * (dk ** -0.5) for ch in pairs}
    ks = {ch: l2n(conv_silu(GDN_HEADS + ch[1], ch[0] * c_len)) for ch in pairs}
    vs = {ch: conv_silu(2 * GDN_HEADS + ch[1], ch[0] * c_len) for ch in pairs}
    kts = {ch: jnp.transpose(ks[ch]) for ch in pairs}
    ktb = {ch: kts[ch].astype(BF16) for ch in pairs}
    kk = {ch: _dot(ks[ch].astype(BF16), ktb[ch]) for ch in pairs}
    qkt = {ch: _dot(qs[ch].astype(BF16), ktb[ch]) for ch in pairs}

    units = [(c, h, d) for c in chunks for h in heads for d in range(2)]
    gc = [gcs[c][d][:, d * GDN_HEADS + h:d * GDN_HEADS + h + 1] for c, h, d in units]
    gr = [gct[c][d][d * GDN_HEADS + h:d * GDN_HEADS + h + 1, :] for c, h, d in units]
    beta = [b_all[c][:, (2 + d) * GDN_HEADS + h:(2 + d) * GDN_HEADS + h + 1] for c, h, d in units]
    e = [jnp.exp(jnp.where(incl[d], c_ - r_, -jnp.inf)) for (c, h, d), c_, r_ in zip(units, gc, gr)]
    a = [jnp.where(strict[d], b_ * kk[(c, h)] * e_, 0.0) for (c, h, d), b_, e_ in zip(units, beta, e)]
    tb = [x.astype(BF16) for x in _tri_inverse(a, masks)]
    egc = [jnp.exp(x) for x in gc]
    us = [_dot(t, (b_ * vs[(c, h)]).astype(BF16)) for (c, h, d), t, b_ in zip(units, tb, beta)]
    ws = [_dot(t, (b_ * g_ * ks[(c, h)]).astype(BF16)) for (c, h, d), t, b_, g_ in zip(units, tb, beta, egc)]
    for n, (c, h, d) in enumerate(units):
        rs = slice(c * c_len, (c + 1) * c_len)
        g_last = gc[n][last_row[d]:last_row[d] + 1, :]
        u_ref[d, rs, hsl[h]] = us[n].astype(BF16)
        a1_ref[d, c, 0:c_len, hsl[h]] = ws[n].astype(BF16)
        a1_ref[d, c, c_len:2 * c_len, hsl[h]] = (qs[(c, h)] * egc[n]).astype(BF16)
        a2_ref[d, c, 0:c_len, hsl[h]] = (qkt[(c, h)] * e[n]).astype(BF16)
        a2_ref[d, c, c_len:2 * c_len, hsl[h]] = (kts[(c, h)] * jnp.exp(g_last - gr[n])).astype(BF16)
        gt_ref[d, c, h:h + 1, :] = jnp.broadcast_to(jnp.exp(g_last), (1, LANES))


def _gdn_prep_call(gqkv, ab, conv_w, abp, *, l, tg, ctx_len):
    la, wq = gqkv.shape
    c_len = GDN_CHUNK
    nch = la // c_len
    cpt = tg // c_len
    hb = tg // HALO
    nhalo = la // HALO
    kern = functools.partial(_gdn_prep_kernel, tg=tg, ctx_len=ctx_len, total=la)
    stacked = pl.BlockSpec((2, cpt, 2 * c_len, GDN_W), lambda i: (0, i, 0, 0))
    return pl.pallas_call(
        kern,
        grid=(la // tg,),
        in_specs=[pl.BlockSpec((HALO, wq), lambda i: (jnp.maximum(i * hb - 1, 0), 0)),
                  pl.BlockSpec((tg, wq), lambda i: (i, 0)),
                  pl.BlockSpec((HALO, wq), lambda i: (jnp.minimum((i + 1) * hb, nhalo - 1), 0)),
                  pl.BlockSpec((tg, LANES), lambda i: (i, 0)),
                  _layer_spec(conv_w, l), _layer_spec(abp, l)],
        out_specs=[pl.BlockSpec((2, tg, GDN_W), lambda i: (0, i, 0)), stacked, stacked,
                   pl.BlockSpec((2, cpt, GDN_HEADS, LANES), lambda i: (0, i, 0, 0))],
        out_shape=[jax.ShapeDtypeStruct((2, la, GDN_W), BF16),
                   jax.ShapeDtypeStruct((2, nch, 2 * c_len, GDN_W), BF16),
                   jax.ShapeDtypeStruct((2, nch, 2 * c_len, GDN_W), BF16),
                   jax.ShapeDtypeStruct((2, nch, GDN_HEADS, LANES), F32)],
        scratch_shapes=[pltpu.VMEM((tg + 2 * HALO, wq), F32)],
        compiler_params=_params(("parallel",)),
        name="gdn_prep",
    )(gqkv, gqkv, gqkv, ab, conv_w, abp)


def _gdn_scan_kernel(uf, ub, a1f, a1b, a2f, a2b, gtf, gtb, of_ref, ob_ref, s_ref):
    @pl.when(pl.program_id(0) == 0)
    def _():
        s_ref[...] = jnp.zeros(s_ref.shape, F32)

    dk = GDN_HEAD_DIM
    c_len = GDN_CHUNK
    dirs = ((uf, a1f, a2f, gtf, of_ref), (ub, a1b, a2b, gtb, ob_ref))
    chains = [(d, h) for d in range(2) for h in range(GDN_HEADS)]
    hs = [slice(h * dk, (h + 1) * dk) for h in range(GDN_HEADS)]
    s = [s_ref[d * GDN_HEADS + h] for d, h in chains]
    r1 = [_dot(dirs[d][1][:, hs[h]], s_.astype(BF16)) for (d, h), s_ in zip(chains, s)]
    vb = [(dirs[d][0][:, hs[h]].astype(F32) - r[0:c_len]).astype(BF16) for (d, h), r in zip(chains, r1)]
    r2 = [_dot(dirs[d][2][:, hs[h]], v_) for (d, h), v_ in zip(chains, vb)]
    for (d, h), s_, ra, rb in zip(chains, s, r1, r2):
        dirs[d][4][:, hs[h]] = (ra[c_len:2 * c_len] + rb[0:c_len]).astype(BF16)
        s_ref[d * GDN_HEADS + h] = s_ * dirs[d][3][h:h + 1, :] + rb[c_len:2 * c_len]


def _gdn_scan_call(u, a1, a2, gt, *, ctx_len):
    la = u.shape[1]
    c_len = GDN_CHUNK
    nch = la // c_len
    ncc = ctx_len // c_len

    def bwd(n):
        return jnp.where(n < ncc, ncc - 1 - n, nch + ncc - 1 - n)

    def both(make):
        return [make(0, lambda n: n), make(1, bwd)]

    u_specs = both(lambda d, f: pl.BlockSpec((None, c_len, GDN_W), lambda n: (d, f(n), 0)))
    a_specs = both(lambda d, f: pl.BlockSpec((None, None, 2 * c_len, GDN_W), lambda n: (d, f(n), 0, 0)))
    g_specs = both(lambda d, f: pl.BlockSpec((None, None, GDN_HEADS, LANES), lambda n: (d, f(n), 0, 0)))
    return pl.pallas_call(
        _gdn_scan_kernel,
        grid=(nch,),
        in_specs=u_specs + a_specs + a_specs + g_specs,
        out_specs=[pl.BlockSpec((c_len, GDN_W), lambda n: (n, 0)),
                   pl.BlockSpec((c_len, GDN_W), lambda n: (bwd(n), 0))],
        out_shape=[jax.ShapeDtypeStruct((la, GDN_W), BF16), jax.ShapeDtypeStruct((la, GDN_W), BF16)],
        scratch_shapes=[pltpu.VMEM((2 * GDN_HEADS, GDN_HEAD_DIM, GDN_HEAD_DIM), F32)],
        compiler_params=_params(("arbitrary",)),
        name="gdn_scan",
    )(u, u, a1, a1, a2, a2, gt, gt)


def _merge_kernel(x_ref, attn_ref, of_ref, ob_ref, z_ref, pprev_ref, pcur_ref, pnext_ref, gate_ref,
                  mod_ref, gnorm_ref, poolw_ref, pscale_ref, wba_ref, wbg_ref, wbp_ref, wout_ref,
                  out_ref, ps_ref, *, tm, ctx_len, total):
    i = pl.program_id(0)
    row0 = i * tm
    first = jnp.logical_or(row0 == 0, row0 == ctx_len)
    last = jnp.logical_or(row0 + tm == ctx_len, row0 + tm == total)
    ps_ref[0:HALO, :] = jnp.where(first, 0.0, pprev_ref[...])
    ps_ref[HALO:HALO + tm, :] = pcur_ref[...]
    ps_ref[HALO + tm:HALO + tm + HALO, :] = jnp.where(last, 0.0, pnext_ref[...])

    rows = _row_ids(row0, tm)
    is_ctx = rows < ctx_len
    seg_pos = jnp.where(is_ctx, rows, rows - ctx_len)
    seg_len = jnp.where(is_ctx, ctx_len, total - ctx_len)

    pooled = []
    for gi, win in enumerate(POOL_SIZES):
        ls = slice(gi * POOL_GROUP, (gi + 1) * POOL_GROUP)
        s = ps_ref[pl.ds(HALO - win // 2, tm), ls]
        for o in range(1, win):
            s = s + ps_ref[pl.ds(HALO - win // 2 + o, tm), ls]
        cnt = (jnp.minimum(seg_pos + win // 2, seg_len) - jnp.maximum(seg_pos - win // 2, 0)).astype(F32)
        r = s / cnt - pcur_ref[:, ls]
        pooled.append(_dot(r.astype(BF16), poolw_ref[gi]) * pscale_ref[:, ls])
    pool_o = jnp.concatenate(pooled, axis=1).astype(BF16)

    gdn = []
    for h in range(GDN_HEADS):
        hs = slice(h * GDN_HEAD_DIM, (h + 1) * GDN_HEAD_DIM)
        o = of_ref[:, hs].astype(F32) + ob_ref[:, hs].astype(F32)
        o = o * lax.rsqrt(jnp.mean(o * o, axis=-1, keepdims=True) + EPS) * gnorm_ref[...]
        gdn.append(o * _silu(z_ref[:, hs].astype(F32)))
    gdn_o = jnp.concatenate(gdn, axis=1).astype(BF16)

    d = x_ref.shape[1]
    merged = (gate_ref[:, 0:d].astype(F32) * _dot(attn_ref[...], wba_ref[...])
              + gate_ref[:, d:2 * d].astype(F32) * _dot(gdn_o, wbg_ref[...])
              + gate_ref[:, 2 * d:3 * d].astype(F32) * _dot(pool_o, wbp_ref[...]))
    y = _dot(merged.astype(BF16), wout_ref[...])
    out_ref[...] = x_ref[...] + _mod_gate(mod_ref, rows, ctx_len, d) * y


def _merge_call(xa, attn_o, o_f, o_b, z, pin, gates, mods, gnorm, pool_w, pool_scale, wba, wbg, wbp, wout,
                *, l, tm, ctx_len):
    la, d = xa.shape
    hb = tm // HALO
    nhalo = la // HALO
    pw = pin.shape[1]
    kern = functools.partial(_merge_kernel, tm=tm, ctx_len=ctx_len, total=la)
    row = lambda width: pl.BlockSpec((tm, width), lambda i: (i, 0))
    return pl.pallas_call(
        kern,
        grid=(la // tm,),
        in_specs=[row(d), row(attn_o.shape[1]), row(GDN_W), row(GDN_W), row(GDN_W),
                  pl.BlockSpec((HALO, pw), lambda i: (jnp.maximum(i * hb - 1, 0), 0)),
                  row(pw),
                  pl.BlockSpec((HALO, pw), lambda i: (jnp.minimum((i + 1) * hb, nhalo - 1), 0)),
                  row(gates.shape[1]),
                  _mod_spec(d, l, 1), _layer_spec(gnorm, l), _layer_spec(pool_w, l), _layer_spec(pool_scale, l),
                  _layer_spec(wba, l), _layer_spec(wbg, l), _layer_spec(wbp, l), _layer_spec(wout, l)],
        out_specs=row(d),
        out_shape=jax.ShapeDtypeStruct((la, d), F32),
        input_output_aliases={0: 0},
        scratch_shapes=[pltpu.VMEM((tm + 2 * HALO, pw), F32)],
        compiler_params=_params(("parallel",)),
        name="merge",
    )(xa, attn_o, o_f, o_b, z, pin, pin, pin, gates, mods, gnorm, pool_w, pool_scale, wba, wbg, wbp, wout)


def _final_norm_kernel(x_ref, gain_ref, out_ref):
    x = x_ref[...]
    out_ref[...] = x * lax.rsqrt(jnp.mean(x * x, axis=-1, keepdims=True) + EPS) * gain_ref[...]


def _final_norm_call(xa, gain, *, tm, ctx_len):
    la, d = xa.shape
    skip = ctx_len // tm
    return pl.pallas_call(
        _final_norm_kernel,
        grid=((la - ctx_len) // tm,),
        in_specs=[pl.BlockSpec((tm, d), lambda i: (i + skip, 0)), _const_spec((1, d))],
        out_specs=pl.BlockSpec((tm, d), lambda i: (i, 0)),
        out_shape=jax.ShapeDtypeStruct((la - ctx_len, d), F32),
        compiler_params=_params(("parallel",)),
        name="final_norm",
    )(xa, gain)


def _rope_tables(seq, ctx_len):
    quarter = ATTN_HEAD_DIM // 4
    rows = seq // GRID_W
    inv = ROPE_THETA ** (-jnp.arange(quarter, dtype=F32) / quarter)

    def axis_tables(n):
        ang = jnp.arange(n, dtype=F32)[:, None] * inv[None, :]
        c, s = jnp.cos(ang), jnp.sin(ang)
        return jnp.concatenate([c, c], axis=1), jnp.concatenate([-s, s], axis=1)

    cr, sr = axis_tables(rows)
    cc, sc = axis_tables(GRID_W)

    def full(tr, tc):
        t = jnp.concatenate([jnp.broadcast_to(tr[:, None, :], (rows, GRID_W, 2 * quarter)),
                             jnp.broadcast_to(tc[None, :, :], (rows, GRID_W, 2 * quarter))], axis=-1)
        t = t.reshape(seq, ATTN_HEAD_DIM)
        return jnp.concatenate([t] * (LANES // ATTN_HEAD_DIM), axis=1)

    cos = jnp.concatenate([jnp.ones((ctx_len, LANES), F32), full(cr, cc)], axis=0)
    sin = jnp.concatenate([jnp.zeros((ctx_len, LANES), F32), full(sr, sc)], axis=0)
    return cos, sin


def _pack_w_in(w_in):
    parts, o = [], 0
    for sz, (_, width) in zip(_SEC_SRC, _SEC):
        p = w_in[:, :, o:o + sz]
        if width != sz:
            p = jnp.pad(p, ((0, 0), (0, 0), (0, width - sz)))
        parts.append(p)
        o += sz
    return jnp.concatenate(parts, axis=2).astype(BF16)


def kernel(x, c, ctx, c_ctx, w_ada, b_ada, norm_ffn1, w_ffn1_in, w_ffn1_out, norm_mix, w_in, attn_sink,
           gdn_conv, gdn_a_log, gdn_dt_bias, gdn_norm, pool_w, pool_scale, w_branch_attn, w_branch_gdn,
           w_branch_pool, w_out, norm_ffn2, w_ffn2_in, w_ffn2_out, final_norm):
    batch, seq, d = x.shape
    assert batch == 1, "single-sequence kernel"
    ctx_len = ctx.shape[1]
    depth = w_ada.shape[0]
    la = ctx_len + seq
    tm = 256
    tm_big = 640
    assert ctx_len % tm == 0 and seq % tm == 0 and seq % GRID_W == 0 and la % tm_big == 0

    xa = jnp.concatenate([ctx[0], x[0]], axis=0)
    cvec = jnp.zeros((8, d), F32).at[0].set(c[0]).at[1].set(c_ctx)
    mods = _ada_call(cvec, w_ada, b_ada)
    cos, sin = _rope_tables(seq, ctx_len)

    bf = lambda w: w.astype(BF16)
    unit = lambda p: p[:, None, :]
    w1i, w1o, w2i, w2o = bf(w_ffn1_in), bf(w_ffn1_out), bf(w_ffn2_in), bf(w_ffn2_out)
    wproj = _pack_w_in(w_in)
    wba, wbg, wbp, wo, pw = bf(w_branch_attn), bf(w_branch_gdn), bf(w_branch_pool), bf(w_out), bf(pool_w)
    n1, nm, n2, gn, psc = unit(norm_ffn1), unit(norm_mix), unit(norm_ffn2), unit(gdn_norm), unit(pool_scale)
    abp = jnp.stack([gdn_dt_bias.reshape(depth, -1), gdn_a_log.reshape(depth, -1)], axis=1)
    abp = jnp.pad(abp, ((0, 0), (0, 6), (0, LANES - 2 * GDN_HEADS)))

    for l in range(depth):
        xa = _ffn_call(xa, mods, n1, w1i, w1o, l=l, j=0, tm=tm_big, ctx_len=ctx_len)
        q, k, v, gqkv, z, ab, pin, gates = _proj_call(xa, mods, nm, wproj, cos, sin,
                                                      l=l, tm=tm_big, ctx_len=ctx_len)
        attn_o = _attn_call(attn_sink, q, k, v, l=l, ctx_len=ctx_len)
        u, a1, a2, gt = _gdn_prep_call(gqkv, ab, gdn_conv, abp, l=l, tg=tm, ctx_len=ctx_len)
        o_f, o_b = _gdn_scan_call(u, a1, a2, gt, ctx_len=ctx_len)
        xa = _merge_call(xa, attn_o, o_f, o_b, z, pin, gates, mods, gn, pw, psc, wba, wbg, wbp, wo,
                         l=l, tm=tm, ctx_len=ctx_len)
        xa = _ffn_call(xa, mods, n2, w2i, w2o, l=l, j=2, tm=tm_big, ctx_len=ctx_len)

    out = _final_norm_call(xa, final_norm[None], tm=tm, ctx_len=ctx_len)
    return out[None]
```

```python
import functools

import jax
import jax.numpy as jnp
from jax import lax
from jax.experimental import pallas as pl
from jax.experimental.pallas import tpu as pltpu

F32 = jnp.float32
BF16 = jnp.bfloat16
HIGHEST = lax.Precision.HIGHEST

GRID_W = 64
ATTN_HEADS = 8
ATTN_KV_HEADS = 2
ATTN_HEAD_DIM = 64
ATTN_BLOCK = 128
ROPE_THETA = 10000.0
GDN_HEADS = 4
GDN_HEAD_DIM = 128
GDN_W = GDN_HEADS * GDN_HEAD_DIM
GDN_CONV = 5
GDN_CHUNK = 128
GDN_BASE = 16
SCAN_CHUNKS = 2
POOL_SIZES = (2, 4, 8, 16)
POOL_GROUP = 128
POOL_WIDTH = POOL_GROUP * len(POOL_SIZES)
N_MOD = 9
N_BRANCH = 3
EPS = 1e-6

LANES = 128
HALO = 8
VMEM_LIMIT = 56 * 1024 * 1024
COL_CHUNK = 512


def _dot(a, b, precision=None):
    return jnp.dot(a, b, preferred_element_type=F32, precision=precision)


def _dot_nt(a, b):
    return lax.dot_general(a, b, (((1,), (1,)), ((), ())), preferred_element_type=F32)


def _silu(x):
    return x * jax.nn.sigmoid(x)


def _params(sem):
    return pltpu.CompilerParams(dimension_semantics=sem, vmem_limit_bytes=VMEM_LIMIT)


def _const_spec(shape):
    nd = len(shape)
    return pl.BlockSpec(shape, lambda *_: (0,) * nd, pipeline_mode=pl.Buffered(1))


def _layer_spec(arr, l):
    nd = arr.ndim - 1
    return pl.BlockSpec((None,) + arr.shape[1:], lambda *_: (l,) + (0,) * nd, pipeline_mode=pl.Buffered(1))


def _mod_spec(d, l, j):
    return pl.BlockSpec((None, 8, 3 * d), lambda *_: (l, 0, j), pipeline_mode=pl.Buffered(1))


def _norm_mod(x, gain, mod_ref, rows, ctx_len):
    d = x.shape[1]
    y = x * lax.rsqrt(jnp.mean(x * x, axis=-1, keepdims=True) + EPS) * gain
    is_ctx = rows < ctx_len
    shift = jnp.where(is_ctx, mod_ref[1:2, 0:d], mod_ref[0:1, 0:d])
    scale = jnp.where(is_ctx, mod_ref[1:2, d:2 * d], mod_ref[0:1, d:2 * d])
    return y * (1.0 + scale) + shift


def _mod_gate(mod_ref, rows, ctx_len, d):
    return jnp.where(rows < ctx_len, mod_ref[1:2, 2 * d:3 * d], mod_ref[0:1, 2 * d:3 * d])


def _row_ids(row0, n):
    return row0 + lax.broadcasted_iota(jnp.int32, (n, 1), 0)


def _ada_kernel(c_ref, w_ref, b_ref, out_ref):
    sc = _silu(c_ref[...])
    out_ref[...] = _dot(sc, w_ref[...], HIGHEST) + b_ref[...]


def _ada_call(cvec, w_ada, b_ada):
    depth, d, n = w_ada.shape
    tn = n // 4
    return pl.pallas_call(
        _ada_kernel,
        grid=(depth, n // tn),
        in_specs=[pl.BlockSpec((8, d), lambda l, j: (0, 0)),
                  pl.BlockSpec((None, d, tn), lambda l, j: (l, 0, j)),
                  pl.BlockSpec((None, 1, tn), lambda l, j: (l, 0, j))],
        out_specs=pl.BlockSpec((None, 8, tn), lambda l, j: (l, 0, j)),
        out_shape=jax.ShapeDtypeStruct((depth, 8, n), F32),
        compiler_params=_params(("arbitrary", "arbitrary")),
        name="ada_mod",
    )(cvec, w_ada, b_ada.reshape(depth, 1, n))


def _ffn_kernel(x_ref, mod_ref, gain_ref, win_ref, wout_ref, out_ref, *, tm, ctx_len, d_ff, fc):
    x = x_ref[...]
    rows = _row_ids(pl.program_id(0) * tm, tm)
    hb = _norm_mod(x, gain_ref[...], mod_ref, rows, ctx_len).astype(BF16)
    acc = jnp.zeros(x.shape, F32)
    for j in range(d_ff // fc):
        gt = _dot(hb, win_ref[:, j * fc:(j + 1) * fc])
        up = _dot(hb, win_ref[:, d_ff + j * fc:d_ff + (j + 1) * fc])
        acc = acc + _dot((_silu(gt) * up).astype(BF16), wout_ref[j * fc:(j + 1) * fc, :])
    out_ref[...] = x + (0.5 * _mod_gate(mod_ref, rows, ctx_len, x.shape[1])) * acc


def _ffn_call(xa, mods, gain, w_in, w_out, *, l, j, tm, ctx_len):
    la, d = xa.shape
    d_ff = w_out.shape[1]
    fc = d_ff // 2
    kern = functools.partial(_ffn_kernel, tm=tm, ctx_len=ctx_len, d_ff=d_ff, fc=fc)
    return pl.pallas_call(
        kern,
        grid=(la // tm,),
        in_specs=[pl.BlockSpec((tm, d), lambda i: (i, 0)),
                  _mod_spec(d, l, j), _layer_spec(gain, l), _layer_spec(w_in, l), _layer_spec(w_out, l)],
        out_specs=pl.BlockSpec((tm, d), lambda i: (i, 0)),
        out_shape=jax.ShapeDtypeStruct((la, d), F32),
        input_output_aliases={0: 0},
        compiler_params=_params(("parallel",)),
        name="ffn",
    )(xa, mods, gain, w_in, w_out)


_SEC = (("q", 512), ("k", 128), ("v", 128), ("gqkv", 3 * GDN_W), ("z", GDN_W), ("ab", LANES),
        ("pin", POOL_WIDTH), ("gl", N_BRANCH * 1024))
_SEC_SRC = (512, 128, 128, 3 * GDN_W, GDN_W, 4 * GDN_HEADS, POOL_WIDTH, N_BRANCH * 1024)


def _sec_offsets():
    offs, o = {}, 0
    for name, width in _SEC:
        offs[name] = (o, width)
        o += width
    return offs, o


def _rope(t, cos, sin):
    n = t.shape[-1]
    lane = lax.broadcasted_iota(jnp.int32, t.shape, 1)
    swapped = jnp.where(lane % 32 < 16, pltpu.roll(t, n - 16, 1), pltpu.roll(t, 16, 1))
    reps = n // LANES
    c = jnp.concatenate([cos] * reps, axis=1) if reps > 1 else cos
    s = jnp.concatenate([sin] * reps, axis=1) if reps > 1 else sin
    return t * c + swapped * s


def _dup_halves(t):
    lo = lax.broadcasted_iota(jnp.int32, t.shape, 1) < LANES // 2
    r = pltpu.roll(t, LANES // 2, 1)
    return jnp.concatenate([jnp.where(lo, t, r), jnp.where(lo, r, t)], axis=1)


def _proj_kernel(x_ref, mod_ref, gain_ref, w_ref, cos_ref, sin_ref,
                 q_ref, k_ref, v_ref, gqkv_ref, z_ref, ab_ref, pin_ref, gate_ref, *, tm, ctx_len):
    offs, _ = _sec_offsets()
    rows = _row_ids(pl.program_id(0) * tm, tm)
    hb = _norm_mod(x_ref[...], gain_ref[...], mod_ref, rows, ctx_len).astype(BF16)

    def sec(name, out_ref, fn):
        o, width = offs[name]
        step = min(width, COL_CHUNK)
        for a in range(0, width, step):
            out_ref[:, a:a + step] = fn(_dot(hb, w_ref[:, o + a:o + a + step])).astype(out_ref.dtype)

    cos, sin = cos_ref[...], sin_ref[...]
    sec("q", q_ref, lambda t: _rope(t, cos, sin) * (ATTN_HEAD_DIM ** -0.5))
    o, width = offs["k"]
    k_ref[...] = _dup_halves(_rope(_dot(hb, w_ref[:, o:o + width]), cos, sin)).astype(BF16)
    o, width = offs["v"]
    v_ref[...] = _dup_halves(_dot(hb, w_ref[:, o:o + width])).astype(BF16)
    ident = lambda t: t
    sec("gqkv", gqkv_ref, ident)
    sec("z", z_ref, ident)
    sec("ab", ab_ref, ident)
    sec("pin", pin_ref, ident)
    sec("gl", gate_ref, jax.nn.sigmoid)


def _proj_call(xa, mods, gain, w, cos, sin, *, l, tm, ctx_len):
    la, d = xa.shape
    offs, _ = _sec_offsets()
    kern = functools.partial(_proj_kernel, tm=tm, ctx_len=ctx_len)
    row = lambda width: pl.BlockSpec((tm, width), lambda i: (i, 0))
    outs = ((512, BF16), (2 * LANES, BF16), (2 * LANES, BF16), (3 * GDN_W, F32), (GDN_W, BF16), (LANES, F32),
            (POOL_WIDTH, F32), (offs["gl"][1], BF16))
    return pl.pallas_call(
        kern,
        grid=(la // tm,),
        in_specs=[row(d), _mod_spec(d, l, 1), _layer_spec(gain, l), _layer_spec(w, l), row(LANES), row(LANES)],
        out_specs=[row(width) for width, _ in outs],
        out_shape=[jax.ShapeDtypeStruct((la, width), dt) for width, dt in outs],
        compiler_params=_params(("parallel",)),
        name="proj",
    )(xa, mods, gain, w, cos, sin)


def _attn_kernel(sink_ref, q_ref, kp_ref, kc_ref, kn_ref, vp_ref, vc_ref, vn_ref, kx_ref, vx_ref,
                 o_ref, *, l, ctx_blocks, n_blocks):
    b = pl.program_id(0)
    blk = ATTN_BLOCK
    is_lat = b >= ctx_blocks
    prev_ok = b - 1 >= ctx_blocks
    next_ok = jnp.logical_and(is_lat, b + 1 < n_blocks)
    ri = lax.broadcasted_iota(jnp.int32, (blk, blk), 0)
    ci = lax.broadcasted_iota(jnp.int32, (blk, blk), 1)
    ninf = -jnp.inf
    bias = jnp.concatenate(
        [jnp.where(jnp.logical_and(ci >= ri, prev_ok), 0.0, ninf),
         jnp.where(jnp.broadcast_to(is_lat, (blk, blk)), 0.0, ninf),
         jnp.where(jnp.logical_and(ci <= ri, next_ok), 0.0, ninf)], axis=1)
    lo = lax.broadcasted_iota(jnp.int32, (blk, LANES), 1) < LANES // 2
    pairs = ATTN_HEADS // ATTN_KV_HEADS // 2
    kv_groups = range(ATTN_KV_HEADS)
    scores, values, sinks = [], [], []
    for hk in kv_groups:
        ks = slice(hk * LANES, (hk + 1) * LANES)
        kw = jnp.concatenate([kp_ref[:, ks], kc_ref[:, ks], kn_ref[:, ks]], axis=0)
        vw = jnp.concatenate([vp_ref[:, ks], vc_ref[:, ks], vn_ref[:, ks]], axis=0)
        q_parts, sk = [], []
        for j in range(pairs):
            col = (hk * pairs + j) * LANES
            qp = q_ref[:, col:col + LANES]
            q_parts += [jnp.where(lo, qp, jnp.zeros_like(qp)), jnp.where(lo, jnp.zeros_like(qp), qp)]
            sk += [sink_ref[l, (hk * pairs + j) * 2], sink_ref[l, (hk * pairs + j) * 2 + 1]]
        q4 = jnp.concatenate(q_parts, axis=0)
        scores.append((_dot_nt(q4, kw), _dot_nt(q4, kx_ref[:, ks])))
        values.append((vw, vx_ref[:, ks]))
        sinks.append(sk)
    probs = []
    for hk in kv_groups:
        s_win, s_ctx = scores[hk]
        p_win, p_ctx, inv = [], [], []
        for j, sk in enumerate(sinks[hk]):
            rs = slice(j * blk, (j + 1) * blk)
            sw = s_win[rs] + bias
            sx = s_ctx[rs]
            m = jnp.maximum(jnp.maximum(jnp.max(sw, axis=-1, keepdims=True),
                                        jnp.max(sx, axis=-1, keepdims=True)), sk)
            pw = jnp.exp(sw - m)
            px = jnp.exp(sx - m)
            denom = jnp.sum(pw, axis=-1, keepdims=True) + jnp.sum(px, axis=-1, keepdims=True) + jnp.exp(sk - m)
            p_win.append(pw.astype(BF16))
            p_ctx.append(px.astype(BF16))
            inv.append(1.0 / denom)
        probs.append((jnp.concatenate(p_win, axis=0), jnp.concatenate(p_ctx, axis=0), inv))
    for hk in kv_groups:
        p_win, p_ctx, inv = probs[hk]
        vw, vx = values[hk]
        r = _dot(p_win, vw) + _dot(p_ctx, vx)
        for j in range(pairs):
            r0 = r[(2 * j) * blk:(2 * j + 1) * blk] * inv[2 * j]
            r1 = r[(2 * j + 1) * blk:(2 * j + 2) * blk] * inv[2 * j + 1]
            col = (hk * pairs + j) * LANES
            o_ref[:, col:col + LANES] = jnp.where(lo, r0, r1).astype(BF16)


def _attn_call(sink, q, k, v, *, l, ctx_len):
    la = q.shape[0]
    blk = ATTN_BLOCK
    nb = la // blk
    cb = ctx_len // blk
    kern = functools.partial(_attn_kernel, l=l, ctx_blocks=cb, n_blocks=nb)
    kvw = k.shape[1]
    prev = pl.BlockSpec((blk, kvw), lambda b: (jnp.maximum(b - 1, 0), 0))
    cur = pl.BlockSpec((blk, kvw), lambda b: (b, 0))
    nxt = pl.BlockSpec((blk, kvw), lambda b: (jnp.minimum(b + 1, nb - 1), 0))
    cx = pl.BlockSpec((ctx_len, kvw), lambda b: (0, 0))
    return pl.pallas_call(
        kern,
        grid=(nb,),
        in_specs=[pl.BlockSpec(memory_space=pltpu.SMEM),
                  pl.BlockSpec((blk, q.shape[1]), lambda b: (b, 0)),
                  prev, cur, nxt, prev, cur, nxt, cx, cx],
        out_specs=pl.BlockSpec((blk, q.shape[1]), lambda b: (b, 0)),
        out_shape=jax.ShapeDtypeStruct(q.shape, BF16),
        compiler_params=_params(("parallel",)),
        name="attn",
    )(sink, q, k, k, k, v, v, v, k, v)


def _tri_inverse_masks(n):
    ri = lax.broadcasted_iota(jnp.int32, (n, n), 0)
    ci = lax.broadcasted_iota(jnp.int32, (n, n), 1)
    eye = (ri == ci).astype(F32)
    same = lambda s: (ri // s) == (ci // s)
    offs, s = [], GDN_BASE
    while s < n:
        offs.append(jnp.logical_and(same(2 * s), jnp.logical_not(same(s))))
        s *= 2
    return eye, same(GDN_BASE), offs


def _tri_inverse(a_list, masks, between=None):
    eye, diag, offs = masks
    a0 = [jnp.where(diag, a, 0.0) for a in a_list]
    a0b = [x.astype(BF16) for x in a0]
    t = [eye - x for x in a0]
    p = [_dot(x, x) for x in a0b]
    s = 2
    while True:
        pb = [x.astype(BF16) for x in p]
        t = [ti + _dot(ti.astype(BF16), pi) for ti, pi in zip(t, pb)]
        s *= 2
        if s >= GDN_BASE:
            break
        p = [_dot(x, x) for x in pb]
    if between is not None:
        between()
    for off in offs:
        tb = [x.astype(BF16) for x in t]
        mid = [_dot(ti, jnp.where(off, a, 0.0).astype(BF16)).astype(BF16) for ti, a in zip(tb, a_list)]
        t = [ti - _dot(mi, tbi) for ti, mi, tbi in zip(t, mid, tb)]
    return t


def _gdn_prep_kernel(prev_ref, cur_ref, next_ref, ab_ref, convw_ref, abp_ref,
                     u_ref, w_ref, qd_ref, a2_ref, gt_ref, xs_ref, a_s, rhs_s, *, tg, ctx_len, total, ntiles):
    i = pl.program_id(0)
    dk = GDN_HEAD_DIM
    c_len = GDN_CHUNK
    heads = range(GDN_HEADS)
    chunks = range(tg // c_len)
    units = [(c, h, d) for c in chunks for h in heads for d in range(2)]
    hsl = [slice(h * dk, (h + 1) * dk) for h in heads]

    @pl.when(i == 0)
    def _():
        a_s[1] = jnp.zeros(a_s.shape[1:], F32)
        rhs_s[1] = jnp.zeros(rhs_s.shape[1:], BF16)

    slot = i % 2
    row0 = jnp.minimum(i, ntiles - 1) * tg
    first = jnp.logical_or(row0 == 0, row0 == ctx_len)
    last = jnp.logical_or(row0 + tg == ctx_len, row0 + tg == total)
    xs_ref[0:HALO, :] = jnp.where(first, 0.0, prev_ref[...])
    xs_ref[HALO:HALO + tg, :] = cur_ref[...]
    xs_ref[HALO + tg:HALO + tg + HALO, :] = jnp.where(last, 0.0, next_ref[...])
    pad = (GDN_CONV - 1) // 2

    def conv_silu(col, r0):
        ls = slice(col * dk, (col + 1) * dk)
        acc = xs_ref[pl.ds(HALO - pad + r0, c_len), ls] * convw_ref[0:1, ls]
        for j in range(1, GDN_CONV):
            acc = acc + xs_ref[pl.ds(HALO - pad + j + r0, c_len), ls] * convw_ref[j:j + 1, ls]
        return _silu(acc)

    def l2n(t):
        return t * lax.rsqrt(jnp.sum(t * t, axis=-1, keepdims=True) + EPS)

    ri = lax.broadcasted_iota(jnp.int32, (c_len, c_len), 0)
    ci = lax.broadcasted_iota(jnp.int32, (c_len, c_len), 1)
    strict = (ci < ri, ci > ri)
    incl = (ci <= ri, ci >= ri)
    tri = (incl[0].astype(F32), incl[1].astype(F32))
    last_row = (c_len - 1, 0)
    masks = _tri_inverse_masks(c_len)
    f = {}

    def front_matmuls():
        g_all = [-jnp.exp(abp_ref[1:2, :]) * jax.nn.softplus(ab_ref[c * c_len:(c + 1) * c_len, :] + abp_ref[0:1, :])
                 for c in chunks]
        f["b_all"] = [jax.nn.sigmoid(ab_ref[c * c_len:(c + 1) * c_len, :]) for c in chunks]
        f["gcs"] = [[_dot(tri[d], g_all[c], HIGHEST) for d in range(2)] for c in chunks]
        f["gct"] = [[jnp.transpose(x) for x in f["gcs"][c]] for c in chunks]
        pairs = [(c, h) for c in chunks for h in heads]
        f["qs"] = {ch: l2n(conv_silu(ch[1], ch[0] * c_len)) * (dk ** -0.5) for ch in pairs}
        f["ks"] = {ch: l2n(conv_silu(GDN_HEADS + ch[1], ch[0] * c_len)) for ch in pairs}
        f["vs"] = {ch: conv_silu(2 * GDN_HEADS + ch[1], ch[0] * c_len) for ch in pairs}
        f["kts"] = {ch: jnp.transpose(f["ks"][ch]) for ch in pairs}
        ktb = {ch: f["kts"][ch].astype(BF16) for ch in pairs}
        f["kk"] = {ch: _dot(f["ks"][ch].astype(BF16), ktb[ch]) for ch in pairs}
        f["qkt"] = {ch: _dot(f["qs"][ch].astype(BF16), ktb[ch]) for ch in pairs}

    a_prev = [a_s[1 - slot, n] for n in range(len(units))]
    t_prev = _tri_inverse(a_prev, masks, between=front_matmuls)
    uw = [_dot(t_.astype(BF16), rhs_s[1 - slot, n]) for n, t_ in enumerate(t_prev)]
    for n, (c, h, d) in enumerate(units):
        rs = slice(c * c_len, (c + 1) * c_len)
        u_ref[d, rs, hsl[h]] = uw[n][:, 0:dk].astype(BF16)
        w_ref[d, rs, hsl[h]] = uw[n][:, dk:2 * dk].astype(BF16)

    for n, (c, h, d) in enumerate(units):
        rs = slice(c * c_len, (c + 1) * c_len)
        lane = d * GDN_HEADS + h
        gc = f["gcs"][c][d][:, lane:lane + 1]
        gr = f["gct"][c][d][lane:lane + 1, :]
        beta = f["b_all"][c][:, 2 * GDN_HEADS + lane:2 * GDN_HEADS + lane + 1]
        e = jnp.exp(jnp.where(incl[d], gc - gr, -jnp.inf))
        egc = jnp.exp(gc)
        g_last = gc[last_row[d]:last_row[d] + 1, :]
        a_s[slot, n] = jnp.where(strict[d], beta * f["kk"][(c, h)] * e, 0.0)
        rhs_s[slot, n, :, 0:dk] = (beta * f["vs"][(c, h)]).astype(BF16)
        rhs_s[slot, n, :, dk:2 * dk] = (beta * egc * f["ks"][(c, h)]).astype(BF16)
        qd_ref[d, rs, hsl[h]] = (f["qs"][(c, h)] * egc).astype(BF16)
        a2_ref[d, c, 0:c_len, hsl[h]] = (f["qkt"][(c, h)] * e).astype(BF16)
        a2_ref[d, c, c_len:2 * c_len, hsl[h]] = (f["kts"][(c, h)] * jnp.exp(g_last - gr)).astype(BF16)
        gt_ref[d, c, h:h + 1, :] = jnp.broadcast_to(jnp.exp(g_last), (1, LANES))


def _gdn_prep_call(gqkv, ab, conv_w, abp, *, l, tg, ctx_len):
    la, wq = gqkv.shape
    c_len = GDN_CHUNK
    nch = la // c_len
    cpt = tg // c_len
    hb = tg // HALO
    nhalo = la // HALO
    ntiles = la // tg
    n_units = cpt * GDN_HEADS * 2
    kern = functools.partial(_gdn_prep_kernel, tg=tg, ctx_len=ctx_len, total=la, ntiles=ntiles)
    cur = lambda i: jnp.minimum(i, ntiles - 1)
    prv = lambda i: jnp.maximum(i - 1, 0)
    rows_cur = pl.BlockSpec((2, tg, GDN_W), lambda i: (0, cur(i), 0))
    rows_prv = pl.BlockSpec((2, tg, GDN_W), lambda i: (0, prv(i), 0))
    return pl.pallas_call(
        kern,
        grid=(ntiles + 1,),
        in_specs=[pl.BlockSpec((HALO, wq), lambda i: (jnp.maximum(cur(i) * hb - 1, 0), 0)),
                  pl.BlockSpec((tg, wq), lambda i: (cur(i), 0)),
                  pl.BlockSpec((HALO, wq), lambda i: (jnp.minimum((cur(i) + 1) * hb, nhalo - 1), 0)),
                  pl.BlockSpec((tg, LANES), lambda i: (cur(i), 0)),
                  _layer_spec(conv_w, l), _layer_spec(abp, l)],
        out_specs=[rows_prv, rows_prv, rows_cur,
                   pl.BlockSpec((2, cpt, 2 * c_len, GDN_W), lambda i: (0, cur(i), 0, 0)),
                   pl.BlockSpec((2, cpt, GDN_HEADS, LANES), lambda i: (0, cur(i), 0, 0))],
        out_shape=[jax.ShapeDtypeStruct((2, la, GDN_W), BF16),
                   jax.ShapeDtypeStruct((2, la, GDN_W), BF16),
                   jax.ShapeDtypeStruct((2, la, GDN_W), BF16),
                   jax.ShapeDtypeStruct((2, nch, 2 * c_len, GDN_W), BF16),
                   jax.ShapeDtypeStruct((2, nch, GDN_HEADS, LANES), F32)],
        scratch_shapes=[pltpu.VMEM((tg + 2 * HALO, wq), F32),
                        pltpu.VMEM((2, n_units, c_len, c_len), F32),
                        pltpu.VMEM((2, n_units, c_len, 2 * GDN_HEAD_DIM), BF16)],
        compiler_params=_params(("arbitrary",)),
        name="gdn_prep",
    )(gqkv, gqkv, gqkv, ab, conv_w, abp)


def _gdn_scan_kernel(uf, ub, wf, wb, qdf, qdb, a2f, a2b, gtf, gtb, of_ref, ob_ref, s_ref):
    @pl.when(pl.program_id(0) == 0)
    def _():
        s_ref[...] = jnp.zeros(s_ref.shape, F32)

    dk = GDN_HEAD_DIM
    c_len = GDN_CHUNK
    dirs = ((uf, wf, qdf, a2f, gtf, of_ref), (ub, wb, qdb, a2b, gtb, ob_ref))
    chains = [(d, h) for d in range(2) for h in range(GDN_HEADS)]
    hs = [slice(h * dk, (h + 1) * dk) for h in range(GDN_HEADS)]
    s = [s_ref[d * GDN_HEADS + h] for d, h in chains]
    for step in range(SCAN_CHUNKS):
        sub = (step, SCAN_CHUNKS - 1 - step)
        rows = [slice(sub[d] * c_len, (sub[d] + 1) * c_len) for d in range(2)]
        r1 = [_dot(jnp.concatenate([dirs[d][1][rows[d], hs[h]], dirs[d][2][rows[d], hs[h]]], axis=0),
                   s_.astype(BF16)) for (d, h), s_ in zip(chains, s)]
        vb = [(dirs[d][0][rows[d], hs[h]].astype(F32) - r[0:c_len]).astype(BF16) for (d, h), r in zip(chains, r1)]
        r2 = [_dot(dirs[d][3][sub[d], :, hs[h]], v_) for (d, h), v_ in zip(chains, vb)]
        for (d, h), ra, rb in zip(chains, r1, r2):
            dirs[d][5][rows[d], hs[h]] = (ra[c_len:2 * c_len] + rb[0:c_len]).astype(BF16)
        s = [s_ * dirs[d][4][sub[d], h:h + 1, :] + rb[c_len:2 * c_len] for (d, h), s_, rb in zip(chains, s, r2)]
    for (d, h), s_ in zip(chains, s):
        s_ref[d * GDN_HEADS + h] = s_


def _gdn_scan_call(u, w, qd, a2, gt, *, ctx_len):
    la = u.shape[1]
    blk = SCAN_CHUNKS * GDN_CHUNK
    assert la % blk == 0 and ctx_len % blk == 0
    nblk = la // blk
    ncb = ctx_len // blk

    def bwd(n):
        return jnp.where(n < ncb, ncb - 1 - n, nblk + ncb - 1 - n)

    def both(make):
        return [make(0, lambda n: n), make(1, bwd)]

    r_specs = both(lambda d, f: pl.BlockSpec((None, blk, GDN_W), lambda n: (d, f(n), 0)))
    a_specs = both(lambda d, f: pl.BlockSpec((None, SCAN_CHUNKS, 2 * GDN_CHUNK, GDN_W), lambda n: (d, f(n), 0, 0)))
    g_specs = both(lambda d, f: pl.BlockSpec((None, SCAN_CHUNKS, GDN_HEADS, LANES), lambda n: (d, f(n), 0, 0)))
    return pl.pallas_call(
        _gdn_scan_kernel,
        grid=(nblk,),
        in_specs=r_specs + r_specs + r_specs + a_specs + g_specs,
        out_specs=[pl.BlockSpec((blk, GDN_W), lambda n: (n, 0)),
                   pl.BlockSpec((blk, GDN_W), lambda n: (bwd(n), 0))],
        out_shape=[jax.ShapeDtypeStruct((la, GDN_W), BF16), jax.ShapeDtypeStruct((la, GDN_W), BF16)],
        scratch_shapes=[pltpu.VMEM((2 * GDN_HEADS, GDN_HEAD_DIM, GDN_HEAD_DIM), F32)],
        compiler_params=_params(("arbitrary",)),
        name="gdn_scan",
    )(u, u, w, w, qd, qd, a2, a2, gt, gt)


def _merge_ffn_kernel(x_ref, attn_ref, of_ref, ob_ref, z_ref, pprev_ref, pcur_ref, pnext_ref, gate_ref,
                      modm_ref, gnorm_ref, poolw_ref, pscale_ref, wba_ref, wbg_ref, wbp_ref, wout_ref,
                      modf_ref, gainf_ref, win_ref, wo2_ref, fgain_ref, out_ref, ps_ref, xn_ref,
                      *, tm, ctx_len, total, ntiles, d_ff, fc, final):
    i = pl.program_id(0)
    d = x_ref.shape[1]
    slot = i % 2

    @pl.when(i == 0)
    def _():
        xn_ref[1] = jnp.zeros(xn_ref.shape[1:], F32)

    row0 = jnp.minimum(i, ntiles - 1) * tm
    first = jnp.logical_or(row0 == 0, row0 == ctx_len)
    last = jnp.logical_or(row0 + tm == ctx_len, row0 + tm == total)
    ps_ref[0:HALO, :] = jnp.where(first, 0.0, pprev_ref[...])
    ps_ref[HALO:HALO + tm, :] = pcur_ref[...]
    ps_ref[HALO + tm:HALO + tm + HALO, :] = jnp.where(last, 0.0, pnext_ref[...])
    rows = _row_ids(row0, tm)
    is_ctx = rows < ctx_len
    seg_pos = jnp.where(is_ctx, rows, rows - ctx_len)
    seg_len = jnp.where(is_ctx, ctx_len, total - ctx_len)
    m = {}

    def merge_vector_prep():
        m["pool_r"] = []
        for gi, win in enumerate(POOL_SIZES):
            ls = slice(gi * POOL_GROUP, (gi + 1) * POOL_GROUP)
            s = ps_ref[pl.ds(HALO - win // 2, tm), ls]
            for o in range(1, win):
                s = s + ps_ref[pl.ds(HALO - win // 2 + o, tm), ls]
            cnt = (jnp.minimum(seg_pos + win // 2, seg_len) - jnp.maximum(seg_pos - win // 2, 0)).astype(F32)
            m["pool_r"].append((s / cnt - pcur_ref[:, ls]).astype(BF16))
        gdn = []
        for h in range(GDN_HEADS):
            hs = slice(h * GDN_HEAD_DIM, (h + 1) * GDN_HEAD_DIM)
            o = of_ref[:, hs].astype(F32) + ob_ref[:, hs].astype(F32)
            o = o * lax.rsqrt(jnp.mean(o * o, axis=-1, keepdims=True) + EPS) * gnorm_ref[...]
            gdn.append(o * _silu(z_ref[:, hs].astype(F32)))
        m["gdn_o"] = jnp.concatenate(gdn, axis=1).astype(BF16)

    def merge_branches():
        pooled = [_dot(r, poolw_ref[gi]) * pscale_ref[:, gi * POOL_GROUP:(gi + 1) * POOL_GROUP]
                  for gi, r in enumerate(m["pool_r"])]
        pool_o = jnp.concatenate(pooled, axis=1).astype(BF16)
        m["merged"] = (gate_ref[:, 0:d].astype(F32) * _dot(attn_ref[...], wba_ref[...])
                       + gate_ref[:, d:2 * d].astype(F32) * _dot(m["gdn_o"], wbg_ref[...])
                       + gate_ref[:, 2 * d:3 * d].astype(F32) * _dot(pool_o, wbp_ref[...])).astype(BF16)

    def merge_out():
        y = _dot(m["merged"], wout_ref[...])
        xn_ref[slot] = x_ref[...] + _mod_gate(modm_ref, rows, ctx_len, d) * y

    xp = xn_ref[1 - slot]
    rows_p = _row_ids((i - 1) * tm, tm)
    hb = _norm_mod(xp, gainf_ref[...], modf_ref, rows_p, ctx_len).astype(BF16)
    acc = jnp.zeros(xp.shape, F32)
    n_chunks = d_ff // fc
    for j in range(n_chunks):
        gt = _dot(hb, win_ref[:, j * fc:(j + 1) * fc])
        up = _dot(hb, win_ref[:, d_ff + j * fc:d_ff + (j + 1) * fc])
        if j == 0:
            merge_vector_prep()
        acc = acc + _dot((_silu(gt) * up).astype(BF16), wo2_ref[j * fc:(j + 1) * fc, :])
    merge_branches()
    merge_out()
    y = xp + (0.5 * _mod_gate(modf_ref, rows_p, ctx_len, d)) * acc
    if final:
        y = y * lax.rsqrt(jnp.mean(y * y, axis=-1, keepdims=True) + EPS) * fgain_ref[...]
    out_ref[...] = y


def _merge_ffn_call(xa, attn_o, o_f, o_b, z, pin, gates, mods, gnorm, pool_w, pool_scale, wba, wbg, wbp, wout,
                    gainf, w_in, w_out, fgain, *, l, tm, ctx_len, final):
    la, d = xa.shape
    hb = tm // HALO
    nhalo = la // HALO
    pw = pin.shape[1]
    ntiles = la // tm
    d_ff = w_out.shape[1]
    fc = d_ff // 2
    assert d_ff // fc == 2
    kern = functools.partial(_merge_ffn_kernel, tm=tm, ctx_len=ctx_len, total=la, ntiles=ntiles,
                             d_ff=d_ff, fc=fc, final=final)
    cur = lambda i: jnp.minimum(i, ntiles - 1)
    row = lambda width: pl.BlockSpec((tm, width), lambda i: (cur(i), 0))
    if final:
        skip = ctx_len // tm
        out_spec = pl.BlockSpec((tm, d), lambda i: (jnp.maximum(i - 1 - skip, 0), 0))
        out_shape = jax.ShapeDtypeStruct((la - ctx_len, d), F32)
        aliases = {}
    else:
        out_spec = pl.BlockSpec((tm, d), lambda i: (jnp.maximum(i - 1, 0), 0))
        out_shape = jax.ShapeDtypeStruct((la, d), F32)
        aliases = {0: 0}
    return pl.pallas_call(
        kern,
        grid=(ntiles + 1,),
        in_specs=[row(d), row(attn_o.shape[1]), row(GDN_W), row(GDN_W), row(GDN_W),
                  pl.BlockSpec((HALO, pw), lambda i: (jnp.maximum(cur(i) * hb - 1, 0), 0)),
                  row(pw),
                  pl.BlockSpec((HALO, pw), lambda i: (jnp.minimum((cur(i) + 1) * hb, nhalo - 1), 0)),
                  row(gates.shape[1]),
                  _mod_spec(d, l, 1), _layer_spec(gnorm, l), _layer_spec(pool_w, l), _layer_spec(pool_scale, l),
                  _layer_spec(wba, l), _layer_spec(wbg, l), _layer_spec(wbp, l), _layer_spec(wout, l),
                  _mod_spec(d, l, 2), _layer_spec(gainf, l), _layer_spec(w_in, l), _layer_spec(w_out, l),
                  _const_spec((1, d))],
        out_specs=out_spec,
        out_shape=out_shape,
        input_output_aliases=aliases,
        scratch_shapes=[pltpu.VMEM((tm + 2 * HALO, pw), F32), pltpu.VMEM((2, tm, d), F32)],
        compiler_params=_params(("arbitrary",)),
        name="merge_ffn",
    )(xa, attn_o, o_f, o_b, z, pin, pin, pin, gates, mods, gnorm, pool_w, pool_scale, wba, wbg, wbp, wout,
      mods, gainf, w_in, w_out, fgain)


def _rope_tables(seq, ctx_len):
    quarter = ATTN_HEAD_DIM // 4
    rows = seq // GRID_W
    inv = ROPE_THETA ** (-jnp.arange(quarter, dtype=F32) / quarter)

    def axis_tables(n):
        ang = jnp.arange(n, dtype=F32)[:, None] * inv[None, :]
        c, s = jnp.cos(ang), jnp.sin(ang)
        return jnp.concatenate([c, c], axis=1), jnp.concatenate([-s, s], axis=1)

    cr, sr = axis_tables(rows)
    cc, sc = axis_tables(GRID_W)

    def full(tr, tc):
        t = jnp.concatenate([jnp.broadcast_to(tr[:, None, :], (rows, GRID_W, 2 * quarter)),
                             jnp.broadcast_to(tc[None, :, :], (rows, GRID_W, 2 * quarter))], axis=-1)
        t = t.reshape(seq, ATTN_HEAD_DIM)
        return jnp.concatenate([t] * (LANES // ATTN_HEAD_DIM), axis=1)

    cos = jnp.concatenate([jnp.ones((ctx_len, LANES), F32), full(cr, cc)], axis=0)
    sin = jnp.concatenate([jnp.zeros((ctx_len, LANES), F32), full(sr, sc)], axis=0)
    return cos, sin


def _pack_w_in(w_in):
    pieces, o, start = [], 0, 0
    for sz, (_, width) in zip(_SEC_SRC, _SEC):
        o += sz
        if width != sz:
            pieces += [w_in[:, :, start:o], jnp.zeros(w_in.shape[:2] + (width - sz,), w_in.dtype)]
            start = o
    pieces.append(w_in[:, :, start:])
    return jnp.concatenate(pieces, axis=2).astype(BF16)


def kernel(x, c, ctx, c_ctx, w_ada, b_ada, norm_ffn1, w_ffn1_in, w_ffn1_out, norm_mix, w_in, attn_sink,
           gdn_conv, gdn_a_log, gdn_dt_bias, gdn_norm, pool_w, pool_scale, w_branch_attn, w_branch_gdn,
           w_branch_pool, w_out, norm_ffn2, w_ffn2_in, w_ffn2_out, final_norm):
    batch, seq, d = x.shape
    assert batch == 1, "single-sequence kernel"
    ctx_len = ctx.shape[1]
    depth = w_ada.shape[0]
    la = ctx_len + seq
    tm = 256
    tm_big = 640
    assert ctx_len % tm == 0 and seq % tm == 0 and seq % GRID_W == 0 and la % tm_big == 0

    xa = jnp.concatenate([ctx[0], x[0]], axis=0)
    cvec = jnp.zeros((8, d), F32).at[0].set(c[0]).at[1].set(c_ctx)
    mods = _ada_call(cvec, w_ada, b_ada)
    cos, sin = _rope_tables(seq, ctx_len)

    bf = lambda w: w.astype(BF16)
    unit = lambda p: p[:, None, :]
    w1i, w1o, w2i, w2o = bf(w_ffn1_in), bf(w_ffn1_out), bf(w_ffn2_in), bf(w_ffn2_out)
    wproj = _pack_w_in(w_in)
    wba, wbg, wbp, wo, pw = bf(w_branch_attn), bf(w_branch_gdn), bf(w_branch_pool), bf(w_out), bf(pool_w)
    n1, nm, n2, gn, psc = unit(norm_ffn1), unit(norm_mix), unit(norm_ffn2), unit(gdn_norm), unit(pool_scale)
    abp = jnp.stack([gdn_dt_bias.reshape(depth, -1), gdn_a_log.reshape(depth, -1)], axis=1)
    abp = jnp.pad(abp, ((0, 0), (0, 6), (0, LANES - 2 * GDN_HEADS)))

    for l in range(depth):
        xa = _ffn_call(xa, mods, n1, w1i, w1o, l=l, j=0, tm=tm_big, ctx_len=ctx_len)
        q, k, v, gqkv, z, ab, pin, gates = _proj_call(xa, mods, nm, wproj, cos, sin,
                                                      l=l, tm=tm_big, ctx_len=ctx_len)
        attn_o = _attn_call(attn_sink, q, k, v, l=l, ctx_len=ctx_len)
        u, w, qd, a2, gt = _gdn_prep_call(gqkv, ab, gdn_conv, abp, l=l, tg=tm, ctx_len=ctx_len)
        o_f, o_b = _gdn_scan_call(u, w, qd, a2, gt, ctx_len=ctx_len)
        xa = _merge_ffn_call(xa, attn_o, o_f, o_b, z, pin, gates, mods, gn, pw, psc, wba, wbg, wbp, wo,
                             n2, w2i, w2o, final_norm[None], l=l, tm=tm, ctx_len=ctx_len, final=l == depth - 1)
    return xa[None]
```

```python
import functools

import jax
import jax.numpy as jnp
from jax import lax
from jax.experimental import pallas as pl
from jax.experimental.pallas import tpu as pltpu

F32 = jnp.float32
BF16 = jnp.bfloat16
HIGHEST = lax.Precision.HIGHEST

GRID_W = 64
ATTN_HEADS = 8
ATTN_KV_HEADS = 2
ATTN_HEAD_DIM = 64
ATTN_BLOCK = 128
ATTN_STEP_BLOCKS = 2
ROPE_THETA = 10000.0
GDN_HEADS = 4
GDN_HEAD_DIM = 128
GDN_W = GDN_HEADS * GDN_HEAD_DIM
GDN_CONV = 5
GDN_CHUNK = 128
GDN_BASE = 16
SCAN_CHUNKS = 2
POOL_SIZES = (2, 4, 8, 16)
POOL_GROUP = 128
POOL_WIDTH = POOL_GROUP * len(POOL_SIZES)
N_MOD = 9
N_BRANCH = 3
EPS = 1e-6

LANES = 128
HALO = 8
VMEM_LIMIT = 56 * 1024 * 1024
COL_CHUNK = 512


def _dot(a, b, precision=None):
    return jnp.dot(a, b, preferred_element_type=F32, precision=precision)


def _dot_nt(a, b):
    return lax.dot_general(a, b, (((1,), (1,)), ((), ())), preferred_element_type=F32)


def _silu(x):
    return x * jax.nn.sigmoid(x)


def _params(sem):
    return pltpu.CompilerParams(dimension_semantics=sem, vmem_limit_bytes=VMEM_LIMIT)


def _const_spec(shape):
    nd = len(shape)
    return pl.BlockSpec(shape, lambda *_: (0,) * nd, pipeline_mode=pl.Buffered(1))


def _layer_spec(arr, l):
    nd = arr.ndim - 1
    return pl.BlockSpec((None,) + arr.shape[1:], lambda *_: (l,) + (0,) * nd, pipeline_mode=pl.Buffered(1))


def _mod_spec(d, l, j):
    return pl.BlockSpec((None, 8, 3 * d), lambda *_: (l, 0, j), pipeline_mode=pl.Buffered(1))


def _norm_mod(x, gain, mod_ref, rows, ctx_len):
    d = x.shape[1]
    y = x * lax.rsqrt(jnp.mean(x * x, axis=-1, keepdims=True) + EPS) * gain
    is_ctx = rows < ctx_len
    shift = jnp.where(is_ctx, mod_ref[1:2, 0:d], mod_ref[0:1, 0:d])
    scale = jnp.where(is_ctx, mod_ref[1:2, d:2 * d], mod_ref[0:1, d:2 * d])
    return y * (1.0 + scale) + shift


def _mod_gate(mod_ref, rows, ctx_len, d):
    return jnp.where(rows < ctx_len, mod_ref[1:2, 2 * d:3 * d], mod_ref[0:1, 2 * d:3 * d])


def _row_ids(row0, n):
    return row0 + lax.broadcasted_iota(jnp.int32, (n, 1), 0)


def _ada_kernel(c_ref, w_ref, b_ref, out_ref):
    sc = _silu(c_ref[...])
    out_ref[...] = _dot(sc, w_ref[...], HIGHEST) + b_ref[...]


def _ada_call(cvec, w_ada, b_ada):
    depth, d, n = w_ada.shape
    tn = n // 4
    return pl.pallas_call(
        _ada_kernel,
        grid=(depth, n // tn),
        in_specs=[pl.BlockSpec((8, d), lambda l, j: (0, 0)),
                  pl.BlockSpec((None, d, tn), lambda l, j: (l, 0, j)),
                  pl.BlockSpec((None, 1, tn), lambda l, j: (l, 0, j))],
        out_specs=pl.BlockSpec((None, 8, tn), lambda l, j: (l, 0, j)),
        out_shape=jax.ShapeDtypeStruct((depth, 8, n), F32),
        compiler_params=_params(("arbitrary", "arbitrary")),
        name="ada_mod",
    )(cvec, w_ada, b_ada.reshape(depth, 1, n))


def _swiglu_half_step(x, rows, mod_ref, gain_ref, win_ref, wout_ref, *, ctx_len, d_ff, fc):
    hb = _norm_mod(x, gain_ref[...], mod_ref, rows, ctx_len).astype(BF16)
    acc = jnp.zeros(x.shape, F32)
    for j in range(d_ff // fc):
        gt = _dot(hb, win_ref[:, j * fc:(j + 1) * fc])
        up = _dot(hb, win_ref[:, d_ff + j * fc:d_ff + (j + 1) * fc])
        acc = acc + _dot((_silu(gt) * up).astype(BF16), wout_ref[j * fc:(j + 1) * fc, :])
    return x + (0.5 * _mod_gate(mod_ref, rows, ctx_len, x.shape[1])) * acc


_SEC = (("q", 512), ("k", 128), ("v", 128), ("gqkv", 3 * GDN_W), ("z", GDN_W), ("ab", LANES),
        ("pin", POOL_WIDTH), ("gl", N_BRANCH * 1024))
_SEC_SRC = (512, 128, 128, 3 * GDN_W, GDN_W, 4 * GDN_HEADS, POOL_WIDTH, N_BRANCH * 1024)


def _sec_offsets():
    offs, o = {}, 0
    for name, width in _SEC:
        offs[name] = (o, width)
        o += width
    return offs, o


def _rope(t, cos, sin):
    n = t.shape[-1]
    lane = lax.broadcasted_iota(jnp.int32, t.shape, 1)
    swapped = jnp.where(lane % 32 < 16, pltpu.roll(t, n - 16, 1), pltpu.roll(t, 16, 1))
    reps = n // LANES
    c = jnp.concatenate([cos] * reps, axis=1) if reps > 1 else cos
    s = jnp.concatenate([sin] * reps, axis=1) if reps > 1 else sin
    return t * c + swapped * s


def _dup_halves(t):
    lo = lax.broadcasted_iota(jnp.int32, t.shape, 1) < LANES // 2
    r = pltpu.roll(t, LANES // 2, 1)
    return jnp.concatenate([jnp.where(lo, t, r), jnp.where(lo, r, t)], axis=1)


def _ffn_proj_kernel(*refs, tm, ctx_len, d_ff, fc, first_layer):
    if first_layer:
        ctx_ref, refs = refs[0], refs[1:]
    (x_ref, modf_ref, gainf_ref, win_ref, wout_ref, modp_ref, gainp_ref, w_ref, cos_ref, sin_ref,
     xo_ref, q_ref, k_ref, v_ref, gqkv_ref, z_ref, ab_ref, pin_ref, gate_ref) = refs
    offs, _ = _sec_offsets()
    i = pl.program_id(0)
    rows = _row_ids(i * tm, tm)
    x = x_ref[...]
    if first_layer:
        x = jnp.where(i == 0, ctx_ref[...], x)
    x = _swiglu_half_step(x, rows, modf_ref, gainf_ref, win_ref, wout_ref, ctx_len=ctx_len, d_ff=d_ff, fc=fc)
    xo_ref[...] = x
    hb = _norm_mod(x, gainp_ref[...], modp_ref, rows, ctx_len).astype(BF16)

    def sec(name, out_ref, fn):
        o, width = offs[name]
        step = min(width, COL_CHUNK)
        for a in range(0, width, step):
            out_ref[:, a:a + step] = fn(_dot(hb, w_ref[:, o + a:o + a + step])).astype(out_ref.dtype)

    cos, sin = cos_ref[...], sin_ref[...]
    sec("q", q_ref, lambda t: _rope(t, cos, sin) * (ATTN_HEAD_DIM ** -0.5))
    o, width = offs["k"]
    k_ref[...] = _dup_halves(_rope(_dot(hb, w_ref[:, o:o + width]), cos, sin)).astype(BF16)
    o, width = offs["v"]
    v_ref[...] = _dup_halves(_dot(hb, w_ref[:, o:o + width])).astype(BF16)
    ident = lambda t: t
    sec("gqkv", gqkv_ref, ident)
    sec("z", z_ref, ident)
    sec("ab", ab_ref, ident)
    sec("pin", pin_ref, ident)
    sec("gl", gate_ref, jax.nn.sigmoid)


def _ffn_proj_call(x_in, ctx, mods, gainf, w_in, w_out, gainp, w, cos, sin, *, l, tm, ctx_len):
    first_layer = ctx is not None
    d = x_in.shape[1]
    la = x_in.shape[0] + (ctx_len if first_layer else 0)
    offs, _ = _sec_offsets()
    d_ff = w_out.shape[1]
    fc = d_ff // 2
    kern = functools.partial(_ffn_proj_kernel, tm=tm, ctx_len=ctx_len, d_ff=d_ff, fc=fc, first_layer=first_layer)
    row = lambda width: pl.BlockSpec((tm, width), lambda i: (i, 0))
    if first_layer:
        assert tm == ctx_len
        lead_specs = [_const_spec((tm, d)), pl.BlockSpec((tm, d), lambda i: (jnp.maximum(i - 1, 0), 0))]
        lead_args = (ctx, x_in)
        aliases = {}
    else:
        lead_specs = [row(d)]
        lead_args = (x_in,)
        aliases = {0: 0}
    outs = ((d, F32), (512, BF16), (2 * LANES, BF16), (2 * LANES, BF16), (3 * GDN_W, F32), (GDN_W, BF16),
            (LANES, F32), (POOL_WIDTH, F32), (offs["gl"][1], BF16))
    return pl.pallas_call(
        kern,
        grid=(la // tm,),
        in_specs=lead_specs + [_mod_spec(d, l, 0), _layer_spec(gainf, l), _layer_spec(w_in, l),
                               _layer_spec(w_out, l), _mod_spec(d, l, 1), _layer_spec(gainp, l),
                               _layer_spec(w, l), row(LANES), row(LANES)],
        out_specs=[row(width) for width, _ in outs],
        out_shape=[jax.ShapeDtypeStruct((la, width), dt) for width, dt in outs],
        input_output_aliases=aliases,
        compiler_params=_params(("parallel",)),
        name="ffn_proj",
    )(*lead_args, mods, gainf, w_in, w_out, mods, gainp, w, cos, sin)


def _attn_kernel(sink_ref, q_ref, kp_ref, kc_ref, kn_ref, vp_ref, vc_ref, vn_ref, kx_ref, vx_ref,
                 o_ref, *, l, ctx_blocks, n_blocks):
    b0 = ATTN_STEP_BLOCKS * pl.program_id(0)
    blk = ATTN_BLOCK
    is_lat = b0 >= ctx_blocks
    ri = lax.broadcasted_iota(jnp.int32, (blk, blk), 0)
    ci = lax.broadcasted_iota(jnp.int32, (blk, blk), 1)
    ninf = -jnp.inf
    lo = lax.broadcasted_iota(jnp.int32, (blk, LANES), 1) < LANES // 2
    pairs = ATTN_HEADS // ATTN_KV_HEADS // 2
    groups = [(sb, hk) for sb in range(ATTN_STEP_BLOCKS) for hk in range(ATTN_KV_HEADS)]

    def window(ref_p, ref_c, ref_n, sb, ks):
        blocks = [ref_p[:, ks]] + [ref_c[j * blk:(j + 1) * blk, ks] for j in range(ATTN_STEP_BLOCKS)] + [ref_n[:, ks]]
        return jnp.concatenate(blocks[sb:sb + 3], axis=0)

    bias = []
    for sb in range(ATTN_STEP_BLOCKS):
        prev_ok = b0 + sb - 1 >= ctx_blocks
        next_ok = jnp.logical_and(is_lat, b0 + sb + 1 < n_blocks)
        bias.append(jnp.concatenate(
            [jnp.where(jnp.logical_and(ci >= ri, prev_ok), 0.0, ninf),
             jnp.where(jnp.broadcast_to(is_lat, (blk, blk)), 0.0, ninf),
             jnp.where(jnp.logical_and(ci <= ri, next_ok), 0.0, ninf)], axis=1))

    scores, values, sinks = [], [], []
    for sb, hk in groups:
        ks = slice(hk * LANES, (hk + 1) * LANES)
        rows = slice(sb * blk, (sb + 1) * blk)
        q_parts, sk = [], []
        for j in range(pairs):
            col = (hk * pairs + j) * LANES
            qp = q_ref[rows, col:col + LANES]
            q_parts += [jnp.where(lo, qp, jnp.zeros_like(qp)), jnp.where(lo, jnp.zeros_like(qp), qp)]
            sk += [sink_ref[l, (hk * pairs + j) * 2], sink_ref[l, (hk * pairs + j) * 2 + 1]]
        q4 = jnp.concatenate(q_parts, axis=0)
        scores.append((_dot_nt(q4, window(kp_ref, kc_ref, kn_ref, sb, ks)), _dot_nt(q4, kx_ref[:, ks])))
        values.append((window(vp_ref, vc_ref, vn_ref, sb, ks), vx_ref[:, ks]))
        sinks.append(sk)
    probs = []
    for g, (sb, hk) in enumerate(groups):
        s_win, s_ctx = scores[g]
        p_win, p_ctx, inv = [], [], []
        for j, sk in enumerate(sinks[g]):
            rs = slice(j * blk, (j + 1) * blk)
            sw = s_win[rs] + bias[sb]
            sx = s_ctx[rs]
            m = jnp.maximum(jnp.maximum(jnp.max(sw, axis=-1, keepdims=True),
                                        jnp.max(sx, axis=-1, keepdims=True)), sk)
            pw = jnp.exp(sw - m)
            px = jnp.exp(sx - m)
            denom = jnp.sum(pw, axis=-1, keepdims=True) + jnp.sum(px, axis=-1, keepdims=True) + jnp.exp(sk - m)
            p_win.append(pw.astype(BF16))
            p_ctx.append(px.astype(BF16))
            inv.append(1.0 / denom)
        probs.append((jnp.concatenate(p_win, axis=0), jnp.concatenate(p_ctx, axis=0), inv))
    for g, (sb, hk) in enumerate(groups):
        p_win, p_ctx, inv = probs[g]
        vw, vx = values[g]
        r = _dot(p_win, vw) + _dot(p_ctx, vx)
        for j in range(pairs):
            r0 = r[(2 * j) * blk:(2 * j + 1) * blk] * inv[2 * j]
            r1 = r[(2 * j + 1) * blk:(2 * j + 2) * blk] * inv[2 * j + 1]
            col = (hk * pairs + j) * LANES
            o_ref[sb * blk:(sb + 1) * blk, col:col + LANES] = jnp.where(lo, r0, r1).astype(BF16)


def _attn_call(sink, q, k, v, *, l, ctx_len):
    la = q.shape[0]
    blk = ATTN_BLOCK
    step = ATTN_STEP_BLOCKS
    nb = la // blk
    cb = ctx_len // blk
    assert nb % step == 0 and cb % step == 0
    kern = functools.partial(_attn_kernel, l=l, ctx_blocks=cb, n_blocks=nb)
    kvw = k.shape[1]
    prev = pl.BlockSpec((blk, kvw), lambda p: (jnp.maximum(step * p - 1, 0), 0))
    cur = pl.BlockSpec((step * blk, kvw), lambda p: (p, 0))
    nxt = pl.BlockSpec((blk, kvw), lambda p: (jnp.minimum(step * (p + 1), nb - 1), 0))
    cx = pl.BlockSpec((ctx_len, kvw), lambda p: (0, 0))
    return pl.pallas_call(
        kern,
        grid=(nb // step,),
        in_specs=[pl.BlockSpec(memory_space=pltpu.SMEM),
                  pl.BlockSpec((step * blk, q.shape[1]), lambda p: (p, 0)),
                  prev, cur, nxt, prev, cur, nxt, cx, cx],
        out_specs=pl.BlockSpec((step * blk, q.shape[1]), lambda p: (p, 0)),
        out_shape=jax.ShapeDtypeStruct(q.shape, BF16),
        compiler_params=_params(("parallel",)),
        name="attn",
    )(sink, q, k, k, k, v, v, v, k, v)


def _tri_inverse_masks(n):
    ri = lax.broadcasted_iota(jnp.int32, (n, n), 0)
    ci = lax.broadcasted_iota(jnp.int32, (n, n), 1)
    eye = (ri == ci).astype(F32)
    same = lambda s: (ri // s) == (ci // s)
    offs, s = [], GDN_BASE
    while s < n:
        offs.append(jnp.logical_and(same(2 * s), jnp.logical_not(same(s))))
        s *= 2
    return eye, same(GDN_BASE), offs


def _tri_inverse(a_list, masks, between=None):
    eye, diag, offs = masks
    a0 = [jnp.where(diag, a, 0.0) for a in a_list]
    a0b = [x.astype(BF16) for x in a0]
    t = [eye - x for x in a0]
    p = [_dot(x, x) for x in a0b]
    s = 2
    while True:
        pb = [x.astype(BF16) for x in p]
        t = [ti + _dot(ti.astype(BF16), pi) for ti, pi in zip(t, pb)]
        s *= 2
        if s >= GDN_BASE:
            break
        p = [_dot(x, x) for x in pb]
    if between is not None:
        between()
    for off in offs:
        tb = [x.astype(BF16) for x in t]
        mid = [_dot(ti, jnp.where(off, a, 0.0).astype(BF16)).astype(BF16) for ti, a in zip(tb, a_list)]
        t = [ti - _dot(mi, tbi) for ti, mi, tbi in zip(t, mid, tb)]
    return t


def _gdn_prep_kernel(prev_ref, cur_ref, next_ref, ab_ref, convw_ref, abp_ref,
                     u_ref, w_ref, qd_ref, a2_ref, gt_ref, xs_ref, a_s, rhs_s, *, tg, ctx_len, total, ntiles):
    i = pl.program_id(0)
    dk = GDN_HEAD_DIM
    c_len = GDN_CHUNK
    heads = range(GDN_HEADS)
    chunks = range(tg // c_len)
    units = [(c, h, d) for c in chunks for h in heads for d in range(2)]
    hsl = [slice(h * dk, (h + 1) * dk) for h in heads]

    @pl.when(i == 0)
    def _():
        a_s[1] = jnp.zeros(a_s.shape[1:], F32)
        rhs_s[1] = jnp.zeros(rhs_s.shape[1:], BF16)

    slot = i % 2
    row0 = jnp.minimum(i, ntiles - 1) * tg
    first = jnp.logical_or(row0 == 0, row0 == ctx_len)
    last = jnp.logical_or(row0 + tg == ctx_len, row0 + tg == total)
    xs_ref[0:HALO, :] = jnp.where(first, 0.0, prev_ref[...])
    xs_ref[HALO:HALO + tg, :] = cur_ref[...]
    xs_ref[HALO + tg:HALO + tg + HALO, :] = jnp.where(last, 0.0, next_ref[...])
    pad = (GDN_CONV - 1) // 2

    def conv_silu(col, r0):
        ls = slice(col * dk, (col + 1) * dk)
        acc = xs_ref[pl.ds(HALO - pad + r0, c_len), ls] * convw_ref[0:1, ls]
        for j in range(1, GDN_CONV):
            acc = acc + xs_ref[pl.ds(HALO - pad + j + r0, c_len), ls] * convw_ref[j:j + 1, ls]
        return _silu(acc)

    def l2n(t):
        return t * lax.rsqrt(jnp.sum(t * t, axis=-1, keepdims=True) + EPS)

    ri = lax.broadcasted_iota(jnp.int32, (c_len, c_len), 0)
    ci = lax.broadcasted_iota(jnp.int32, (c_len, c_len), 1)
    strict = (ci < ri, ci > ri)
    incl = (ci <= ri, ci >= ri)
    tri = (incl[0].astype(F32), incl[1].astype(F32))
    last_row = (c_len - 1, 0)
    masks = _tri_inverse_masks(c_len)
    f = {}

    def front_matmuls():
        g_all = [-jnp.exp(abp_ref[1:2, :]) * jax.nn.softplus(ab_ref[c * c_len:(c + 1) * c_len, :] + abp_ref[0:1, :])
                 for c in chunks]
        f["b_all"] = [jax.nn.sigmoid(ab_ref[c * c_len:(c + 1) * c_len, :]) for c in chunks]
        f["gcs"] = [[_dot(tri[d], g_all[c], HIGHEST) for d in range(2)] for c in chunks]
        f["gct"] = [[jnp.transpose(x) for x in f["gcs"][c]] for c in chunks]
        pairs = [(c, h) for c in chunks for h in heads]
        f["qs"] = {ch: l2n(conv_silu(ch[1], ch[0] * c_len)) * (dk ** -0.5) for ch in pairs}
        f["ks"] = {ch: l2n(conv_silu(GDN_HEADS + ch[1], ch[0] * c_len)) for ch in pairs}
        f["vs"] = {ch: conv_silu(2 * GDN_HEADS + ch[1], ch[0] * c_len) for ch in pairs}
        f["kts"] = {ch: jnp.transpose(f["ks"][ch]) for ch in pairs}
        ktb = {ch: f["kts"][ch].astype(BF16) for ch in pairs}
        f["kk"] = {ch: _dot(f["ks"][ch].astype(BF16), ktb[ch]) for ch in pairs}
        f["qkt"] = {ch: _dot(f["qs"][ch].astype(BF16), ktb[ch]) for ch in pairs}

    a_prev = [a_s[1 - slot, n] for n in range(len(units))]
    t_prev = _tri_inverse(a_prev, masks, between=front_matmuls)
    uw = [_dot(t_.astype(BF16), rhs_s[1 - slot, n]) for n, t_ in enumerate(t_prev)]
    for n, (c, h, d) in enumerate(units):
        rs = slice(c * c_len, (c + 1) * c_len)
        u_ref[d, rs, hsl[h]] = uw[n][:, 0:dk].astype(BF16)
        w_ref[d, rs, hsl[h]] = uw[n][:, dk:2 * dk].astype(BF16)

    for n, (c, h, d) in enumerate(units):
        rs = slice(c * c_len, (c + 1) * c_len)
        lane = d * GDN_HEADS + h
        gc = f["gcs"][c][d][:, lane:lane + 1]
        gr = f["gct"][c][d][lane:lane + 1, :]
        beta = f["b_all"][c][:, 2 * GDN_HEADS + lane:2 * GDN_HEADS + lane + 1]
        e = jnp.exp(jnp.where(incl[d], gc - gr, -jnp.inf))
        egc = jnp.exp(gc)
        g_last = gc[last_row[d]:last_row[d] + 1, :]
        a_s[slot, n] = jnp.where(strict[d], beta * f["kk"][(c, h)] * e, 0.0)
        rhs_s[slot, n, :, 0:dk] = (beta * f["vs"][(c, h)]).astype(BF16)
        rhs_s[slot, n, :, dk:2 * dk] = (beta * egc * f["ks"][(c, h)]).astype(BF16)
        qd_ref[d, rs, hsl[h]] = (f["qs"][(c, h)] * egc).astype(BF16)
        a2_ref[d, c, 0:c_len, hsl[h]] = (f["qkt"][(c, h)] * e).astype(BF16)
        a2_ref[d, c, c_len:2 * c_len, hsl[h]] = (f["kts"][(c, h)] * jnp.exp(g_last - gr)).astype(BF16)
        gt_ref[d, c, h:h + 1, :] = jnp.broadcast_to(jnp.exp(g_last), (1, LANES))


def _gdn_prep_call(gqkv, ab, conv_w, abp, *, l, tg, ctx_len):
    la, wq = gqkv.shape
    c_len = GDN_CHUNK
    nch = la // c_len
    cpt = tg // c_len
    hb = tg // HALO
    nhalo = la // HALO
    ntiles = la // tg
    n_units = cpt * GDN_HEADS * 2
    kern = functools.partial(_gdn_prep_kernel, tg=tg, ctx_len=ctx_len, total=la, ntiles=ntiles)
    cur = lambda i: jnp.minimum(i, ntiles - 1)
    prv = lambda i: jnp.maximum(i - 1, 0)
    rows_cur = pl.BlockSpec((2, tg, GDN_W), lambda i: (0, cur(i), 0))
    rows_prv = pl.BlockSpec((2, tg, GDN_W), lambda i: (0, prv(i), 0))
    return pl.pallas_call(
        kern,
        grid=(ntiles + 1,),
        in_specs=[pl.BlockSpec((HALO, wq), lambda i: (jnp.maximum(cur(i) * hb - 1, 0), 0)),
                  pl.BlockSpec((tg, wq), lambda i: (cur(i), 0)),
                  pl.BlockSpec((HALO, wq), lambda i: (jnp.minimum((cur(i) + 1) * hb, nhalo - 1), 0)),
                  pl.BlockSpec((tg, LANES), lambda i: (cur(i), 0)),
                  _layer_spec(conv_w, l), _layer_spec(abp, l)],
        out_specs=[rows_prv, rows_prv, rows_cur,
                   pl.BlockSpec((2, cpt, 2 * c_len, GDN_W), lambda i: (0, cur(i), 0, 0)),
                   pl.BlockSpec((2, cpt, GDN_HEADS, LANES), lambda i: (0, cur(i), 0, 0))],
        out_shape=[jax.ShapeDtypeStruct((2, la, GDN_W), BF16),
                   jax.ShapeDtypeStruct((2, la, GDN_W), BF16),
                   jax.ShapeDtypeStruct((2, la, GDN_W), BF16),
                   jax.ShapeDtypeStruct((2, nch, 2 * c_len, GDN_W), BF16),
                   jax.ShapeDtypeStruct((2, nch, GDN_HEADS, LANES), F32)],
        scratch_shapes=[pltpu.VMEM((tg + 2 * HALO, wq), F32),
                        pltpu.VMEM((2, n_units, c_len, c_len), F32),
                        pltpu.VMEM((2, n_units, c_len, 2 * GDN_HEAD_DIM), BF16)],
        compiler_params=_params(("arbitrary",)),
        name="gdn_prep",
    )(gqkv, gqkv, gqkv, ab, conv_w, abp)


def _gdn_scan_kernel(uf, ub, wf, wb, qdf, qdb, a2f, a2b, gtf, gtb, of_ref, ob_ref, s_ref):
    @pl.when(pl.program_id(0) == 0)
    def _():
        s_ref[...] = jnp.zeros(s_ref.shape, F32)

    dk = GDN_HEAD_DIM
    c_len = GDN_CHUNK
    dirs = ((uf, wf, qdf, a2f, gtf, of_ref), (ub, wb, qdb, a2b, gtb, ob_ref))
    chains = [(d, h) for d in range(2) for h in range(GDN_HEADS)]
    hs = [slice(h * dk, (h + 1) * dk) for h in range(GDN_HEADS)]
    s = [s_ref[d * GDN_HEADS + h] for d, h in chains]
    for step in range(SCAN_CHUNKS):
        sub = (step, SCAN_CHUNKS - 1 - step)
        rows = [slice(sub[d] * c_len, (sub[d] + 1) * c_len) for d in range(2)]
        r1 = [_dot(jnp.concatenate([dirs[d][1][rows[d], hs[h]], dirs[d][2][rows[d], hs[h]]], axis=0),
                   s_.astype(BF16)) for (d, h), s_ in zip(chains, s)]
        vb = [(dirs[d][0][rows[d], hs[h]].astype(F32) - r[0:c_len]).astype(BF16) for (d, h), r in zip(chains, r1)]
        r2 = [_dot(dirs[d][3][sub[d], :, hs[h]], v_) for (d, h), v_ in zip(chains, vb)]
        for (d, h), ra, rb in zip(chains, r1, r2):
            dirs[d][5][rows[d], hs[h]] = (ra[c_len:2 * c_len] + rb[0:c_len]).astype(BF16)
        s = [s_ * dirs[d][4][sub[d], h:h + 1, :] + rb[c_len:2 * c_len] for (d, h), s_, rb in zip(chains, s, r2)]
    for (d, h), s_ in zip(chains, s):
        s_ref[d * GDN_HEADS + h] = s_


def _gdn_scan_call(u, w, qd, a2, gt, *, ctx_len):
    la = u.shape[1]
    blk = SCAN_CHUNKS * GDN_CHUNK
    assert la % blk == 0 and ctx_len % blk == 0
    nblk = la // blk
    ncb = ctx_len // blk

    def bwd(n):
        return jnp.where(n < ncb, ncb - 1 - n, nblk + ncb - 1 - n)

    def both(make):
        return [make(0, lambda n: n), make(1, bwd)]

    r_specs = both(lambda d, f: pl.BlockSpec((None, blk, GDN_W), lambda n: (d, f(n), 0)))
    a_specs = both(lambda d, f: pl.BlockSpec((None, SCAN_CHUNKS, 2 * GDN_CHUNK, GDN_W), lambda n: (d, f(n), 0, 0)))
    g_specs = both(lambda d, f: pl.BlockSpec((None, SCAN_CHUNKS, GDN_HEADS, LANES), lambda n: (d, f(n), 0, 0)))
    return pl.pallas_call(
        _gdn_scan_kernel,
        grid=(nblk,),
        in_specs=r_specs + r_specs + r_specs + a_specs + g_specs,
        out_specs=[pl.BlockSpec((blk, GDN_W), lambda n: (n, 0)),
                   pl.BlockSpec((blk, GDN_W), lambda n: (bwd(n), 0))],
        out_shape=[jax.ShapeDtypeStruct((la, GDN_W), BF16), jax.ShapeDtypeStruct((la, GDN_W), BF16)],
        scratch_shapes=[pltpu.VMEM((2 * GDN_HEADS, GDN_HEAD_DIM, GDN_HEAD_DIM), F32)],
        compiler_params=_params(("arbitrary",)),
        name="gdn_scan",
    )(u, u, w, w, qd, qd, a2, a2, gt, gt)


def _merge_ffn_kernel(x_ref, attn_ref, of_ref, ob_ref, z_ref, pprev_ref, pcur_ref, pnext_ref, gate_ref,
                      modm_ref, gnorm_ref, poolw_ref, pscale_ref, wba_ref, wbg_ref, wbp_ref, wout_ref,
                      modf_ref, gainf_ref, win_ref, wo2_ref, fgain_ref, out_ref, ps_ref, xn_ref,
                      *, tm, ctx_len, total, ntiles, d_ff, fc, final):
    i = pl.program_id(0)
    d = x_ref.shape[1]
    slot = i % 2

    @pl.when(i == 0)
    def _():
        xn_ref[1] = jnp.zeros(xn_ref.shape[1:], F32)

    row0 = jnp.minimum(i, ntiles - 1) * tm
    first = jnp.logical_or(row0 == 0, row0 == ctx_len)
    last = jnp.logical_or(row0 + tm == ctx_len, row0 + tm == total)
    ps_ref[0:HALO, :] = jnp.where(first, 0.0, pprev_ref[...])
    ps_ref[HALO:HALO + tm, :] = pcur_ref[...]
    ps_ref[HALO + tm:HALO + tm + HALO, :] = jnp.where(last, 0.0, pnext_ref[...])
    rows = _row_ids(row0, tm)
    is_ctx = rows < ctx_len
    seg_pos = jnp.where(is_ctx, rows, rows - ctx_len)
    seg_len = jnp.where(is_ctx, ctx_len, total - ctx_len)
    m = {}

    def merge_vector_prep():
        m["pool_r"] = []
        for gi, win in enumerate(POOL_SIZES):
            ls = slice(gi * POOL_GROUP, (gi + 1) * POOL_GROUP)
            s = ps_ref[pl.ds(HALO - win // 2, tm), ls]
            for o in range(1, win):
                s = s + ps_ref[pl.ds(HALO - win // 2 + o, tm), ls]
            cnt = (jnp.minimum(seg_pos + win // 2, seg_len) - jnp.maximum(seg_pos - win // 2, 0)).astype(F32)
            m["pool_r"].append((s / cnt - pcur_ref[:, ls]).astype(BF16))
        gdn = []
        for h in range(GDN_HEADS):
            hs = slice(h * GDN_HEAD_DIM, (h + 1) * GDN_HEAD_DIM)
            o = of_ref[:, hs].astype(F32) + ob_ref[:, hs].astype(F32)
            o = o * lax.rsqrt(jnp.mean(o * o, axis=-1, keepdims=True) + EPS) * gnorm_ref[...]
            gdn.append(o * _silu(z_ref[:, hs].astype(F32)))
        m["gdn_o"] = jnp.concatenate(gdn, axis=1).astype(BF16)

    def merge_branches():
        pooled = [_dot(r, poolw_ref[gi]) * pscale_ref[:, gi * POOL_GROUP:(gi + 1) * POOL_GROUP]
                  for gi, r in enumerate(m["pool_r"])]
        pool_o = jnp.concatenate(pooled, axis=1).astype(BF16)
        m["merged"] = (gate_ref[:, 0:d].astype(F32) * _dot(attn_ref[...], wba_ref[...])
                       + gate_ref[:, d:2 * d].astype(F32) * _dot(m["gdn_o"], wbg_ref[...])
                       + gate_ref[:, 2 * d:3 * d].astype(F32) * _dot(pool_o, wbp_ref[...])).astype(BF16)

    def merge_out():
        y = _dot(m["merged"], wout_ref[...])
        xn_ref[slot] = x_ref[...] + _mod_gate(modm_ref, rows, ctx_len, d) * y

    xp = xn_ref[1 - slot]
    rows_p = _row_ids((i - 1) * tm, tm)
    hb = _norm_mod(xp, gainf_ref[...], modf_ref, rows_p, ctx_len).astype(BF16)
    acc = jnp.zeros(xp.shape, F32)
    n_chunks = d_ff // fc
    for j in range(n_chunks):
        gt = _dot(hb, win_ref[:, j * fc:(j + 1) * fc])
        up = _dot(hb, win_ref[:, d_ff + j * fc:d_ff + (j + 1) * fc])
        if j == 0:
            merge_vector_prep()
        acc = acc + _dot((_silu(gt) * up).astype(BF16), wo2_ref[j * fc:(j + 1) * fc, :])
    merge_branches()
    merge_out()
    y = xp + (0.5 * _mod_gate(modf_ref, rows_p, ctx_len, d)) * acc
    if final:
        y = y * lax.rsqrt(jnp.mean(y * y, axis=-1, keepdims=True) + EPS) * fgain_ref[...]
    out_ref[...] = y


def _merge_ffn_call(xa, attn_o, o_f, o_b, z, pin, gates, mods, gnorm, pool_w, pool_scale, wba, wbg, wbp, wout,
                    gainf, w_in, w_out, fgain, *, l, tm, ctx_len, final):
    la, d = xa.shape
    hb = tm // HALO
    nhalo = la // HALO
    pw = pin.shape[1]
    ntiles = la // tm
    d_ff = w_out.shape[1]
    fc = d_ff // 2
    assert d_ff // fc == 2
    kern = functools.partial(_merge_ffn_kernel, tm=tm, ctx_len=ctx_len, total=la, ntiles=ntiles,
                             d_ff=d_ff, fc=fc, final=final)
    cur = lambda i: jnp.minimum(i, ntiles - 1)
    row = lambda width: pl.BlockSpec((tm, width), lambda i: (cur(i), 0))
    if final:
        skip = ctx_len // tm
        out_spec = pl.BlockSpec((tm, d), lambda i: (jnp.maximum(i - 1 - skip, 0), 0))
        out_shape = jax.ShapeDtypeStruct((la - ctx_len, d), F32)
        aliases = {}
    else:
        out_spec = pl.BlockSpec((tm, d), lambda i: (jnp.maximum(i - 1, 0), 0))
        out_shape = jax.ShapeDtypeStruct((la, d), F32)
        aliases = {0: 0}
    return pl.pallas_call(
        kern,
        grid=(ntiles + 1,),
        in_specs=[row(d), row(attn_o.shape[1]), row(GDN_W), row(GDN_W), row(GDN_W),
                  pl.BlockSpec((HALO, pw), lambda i: (jnp.maximum(cur(i) * hb - 1, 0), 0)),
                  row(pw),
                  pl.BlockSpec((HALO, pw), lambda i: (jnp.minimum((cur(i) + 1) * hb, nhalo - 1), 0)),
                  row(gates.shape[1]),
                  _mod_spec(d, l, 1), _layer_spec(gnorm, l), _layer_spec(pool_w, l), _layer_spec(pool_scale, l),
                  _layer_spec(wba, l), _layer_spec(wbg, l), _layer_spec(wbp, l), _layer_spec(wout, l),
                  _mod_spec(d, l, 2), _layer_spec(gainf, l), _layer_spec(w_in, l), _layer_spec(w_out, l),
                  _const_spec((1, d))],
        out_specs=out_spec,
        out_shape=out_shape,
        input_output_aliases=aliases,
        scratch_shapes=[pltpu.VMEM((tm + 2 * HALO, pw), F32), pltpu.VMEM((2, tm, d), F32)],
        compiler_params=_params(("arbitrary",)),
        name="merge_ffn",
    )(xa, attn_o, o_f, o_b, z, pin, pin, pin, gates, mods, gnorm, pool_w, pool_scale, wba, wbg, wbp, wout,
      mods, gainf, w_in, w_out, fgain)


def _rope_tables(seq, ctx_len):
    quarter = ATTN_HEAD_DIM // 4
    rows = seq // GRID_W
    inv = ROPE_THETA ** (-jnp.arange(quarter, dtype=F32) / quarter)

    def axis_tables(n):
        ang = jnp.arange(n, dtype=F32)[:, None] * inv[None, :]
        c, s = jnp.cos(ang), jnp.sin(ang)
        return jnp.concatenate([c, c], axis=1), jnp.concatenate([-s, s], axis=1)

    cr, sr = axis_tables(rows)
    cc, sc = axis_tables(GRID_W)

    def full(tr, tc):
        t = jnp.concatenate([jnp.broadcast_to(tr[:, None, :], (rows, GRID_W, 2 * quarter)),
                             jnp.broadcast_to(tc[None, :, :], (rows, GRID_W, 2 * quarter))], axis=-1)
        t = t.reshape(seq, ATTN_HEAD_DIM)
        return jnp.concatenate([t] * (LANES // ATTN_HEAD_DIM), axis=1)

    cos = jnp.concatenate([jnp.ones((ctx_len, LANES), F32), full(cr, cc)], axis=0)
    sin = jnp.concatenate([jnp.zeros((ctx_len, LANES), F32), full(sr, sc)], axis=0)
    return cos, sin


def _pack_kernel(w_ref, out_ref):
    src, dst, start = 0, 0, 0
    for sz, (_, width) in zip(_SEC_SRC, _SEC):
        src += sz
        dst += width
        if width != sz:
            run = src - start
            out_ref[:, dst - width - (run - sz):dst - width + sz] = w_ref[:, start:src].astype(BF16)
            out_ref[:, dst - width + sz:dst] = jnp.zeros((w_ref.shape[0], width - sz), BF16)
            start = src
    tail = w_ref.shape[1] - start
    out_ref[:, dst - tail:dst] = w_ref[:, start:].astype(BF16)


def _pack_w_in(w_in):
    depth, d, n = w_in.shape
    _, n_packed = _sec_offsets()
    tr = 256
    return pl.pallas_call(
        _pack_kernel,
        grid=(depth, d // tr),
        in_specs=[pl.BlockSpec((None, tr, n), lambda l, i: (l, i, 0))],
        out_specs=pl.BlockSpec((None, tr, n_packed), lambda l, i: (l, i, 0)),
        out_shape=jax.ShapeDtypeStruct((depth, d, n_packed), BF16),
        compiler_params=_params(("parallel", "parallel")),
        name="pack_w_in",
    )(w_in)


def kernel(x, c, ctx, c_ctx, w_ada, b_ada, norm_ffn1, w_ffn1_in, w_ffn1_out, norm_mix, w_in, attn_sink,
           gdn_conv, gdn_a_log, gdn_dt_bias, gdn_norm, pool_w, pool_scale, w_branch_attn, w_branch_gdn,
           w_branch_pool, w_out, norm_ffn2, w_ffn2_in, w_ffn2_out, final_norm):
    batch, seq, d = x.shape
    assert batch == 1, "single-sequence kernel"
    ctx_len = ctx.shape[1]
    depth = w_ada.shape[0]
    tm = ctx_len
    assert tm % (SCAN_CHUNKS * GDN_CHUNK) == 0 and seq % tm == 0 and seq % GRID_W == 0

    cvec = jnp.zeros((8, d), F32).at[0].set(c[0]).at[1].set(c_ctx)
    mods = _ada_call(cvec, w_ada, b_ada)
    cos, sin = _rope_tables(seq, ctx_len)

    bf = lambda w: w.astype(BF16)
    unit = lambda p: p[:, None, :]
    w1i, w1o, w2i, w2o = bf(w_ffn1_in), bf(w_ffn1_out), bf(w_ffn2_in), bf(w_ffn2_out)
    wproj = _pack_w_in(w_in)
    wba, wbg, wbp, wo, pw = bf(w_branch_attn), bf(w_branch_gdn), bf(w_branch_pool), bf(w_out), bf(pool_w)
    n1, nm, n2, gn, psc = unit(norm_ffn1), unit(norm_mix), unit(norm_ffn2), unit(gdn_norm), unit(pool_scale)
    abp = jnp.stack([gdn_dt_bias.reshape(depth, -1), gdn_a_log.reshape(depth, -1)], axis=1)
    abp = jnp.pad(abp, ((0, 0), (0, 6), (0, LANES - 2 * GDN_HEADS)))

    for l in range(depth):
        xa, q, k, v, gqkv, z, ab, pin, gates = _ffn_proj_call(
            x[0] if l == 0 else xa, ctx[0] if l == 0 else None, mods, n1, w1i, w1o, nm, wproj, cos, sin,
            l=l, tm=tm, ctx_len=ctx_len)
        attn_o = _attn_call(attn_sink, q, k, v, l=l, ctx_len=ctx_len)
        u, w, qd, a2, gt = _gdn_prep_call(gqkv, ab, gdn_conv, abp, l=l, tg=tm, ctx_len=ctx_len)
        o_f, o_b = _gdn_scan_call(u, w, qd, a2, gt, ctx_len=ctx_len)
        xa = _merge_ffn_call(xa, attn_o, o_f, o_b, z, pin, gates, mods, gn, pw, psc, wba, wbg, wbp, wo,
                             n2, w2i, w2o, final_norm[None], l=l, tm=tm, ctx_len=ctx_len, final=l == depth - 1)
    return xa[None]
```

```python
import functools

import jax
import jax.numpy as jnp
from jax import lax
from jax.experimental import pallas as pl
from jax.experimental.pallas import tpu as pltpu

F32 = jnp.float32
BF16 = jnp.bfloat16
HIGHEST = lax.Precision.HIGHEST

GRID_W = 64
ATTN_HEADS = 8
ATTN_KV_HEADS = 2
ATTN_HEAD_DIM = 64
ATTN_BLOCK = 128
ATTN_STEP_BLOCKS = 2
ROPE_THETA = 10000.0
GDN_HEADS = 4
GDN_HEAD_DIM = 128
GDN_W = GDN_HEADS * GDN_HEAD_DIM
GDN_CONV = 5
GDN_CHUNK = 128
GDN_BASE = 16
SCAN_CHUNKS = 2
POOL_SIZES = (2, 4, 8, 16)
POOL_GROUP = 128
POOL_WIDTH = POOL_GROUP * len(POOL_SIZES)
N_MOD = 9
N_BRANCH = 3
EPS = 1e-6

LANES = 128
HALO = 8
VMEM_LIMIT = 56 * 1024 * 1024
COL_CHUNK = 512


def _dot(a, b, precision=None):
    return jnp.dot(a, b, preferred_element_type=F32, precision=precision)


def _dot_nt(a, b):
    return lax.dot_general(a, b, (((1,), (1,)), ((), ())), preferred_element_type=F32)


def _silu(x):
    return x * jax.nn.sigmoid(x)


def _params(sem):
    return pltpu.CompilerParams(dimension_semantics=sem, vmem_limit_bytes=VMEM_LIMIT)


def _const_spec(shape):
    nd = len(shape)
    return pl.BlockSpec(shape, lambda *_: (0,) * nd, pipeline_mode=pl.Buffered(1))


def _layer_spec(arr, l):
    nd = arr.ndim - 1
    return pl.BlockSpec((None,) + arr.shape[1:], lambda *_: (l,) + (0,) * nd, pipeline_mode=pl.Buffered(1))


def _mod_spec(d, l, j):
    return pl.BlockSpec((None, 8, 3 * d), lambda *_: (l, 0, j), pipeline_mode=pl.Buffered(1))


def _norm_mod(x, gain, mod_ref, rows, ctx_len):
    d = x.shape[1]
    y = x * lax.rsqrt(jnp.mean(x * x, axis=-1, keepdims=True) + EPS) * gain
    is_ctx = rows < ctx_len
    shift = jnp.where(is_ctx, mod_ref[1:2, 0:d], mod_ref[0:1, 0:d])
    scale = jnp.where(is_ctx, mod_ref[1:2, d:2 * d], mod_ref[0:1, d:2 * d])
    return y * (1.0 + scale) + shift


def _mod_gate(mod_ref, rows, ctx_len, d):
    return jnp.where(rows < ctx_len, mod_ref[1:2, 2 * d:3 * d], mod_ref[0:1, 2 * d:3 * d])


def _row_ids(row0, n):
    return row0 + lax.broadcasted_iota(jnp.int32, (n, 1), 0)


def _ada_kernel(c_ref, w_ref, b_ref, out_ref):
    sc = _silu(c_ref[...])
    out_ref[...] = _dot(sc, w_ref[...], HIGHEST) + b_ref[...]


def _ada_call(cvec, w_ada, b_ada):
    depth, d, n = w_ada.shape
    tn = n // 4
    return pl.pallas_call(
        _ada_kernel,
        grid=(depth, n // tn),
        in_specs=[pl.BlockSpec((8, d), lambda l, j: (0, 0)),
                  pl.BlockSpec((None, d, tn), lambda l, j: (l, 0, j)),
                  pl.BlockSpec((None, 1, tn), lambda l, j: (l, 0, j))],
        out_specs=pl.BlockSpec((None, 8, tn), lambda l, j: (l, 0, j)),
        out_shape=jax.ShapeDtypeStruct((depth, 8, n), F32),
        compiler_params=_params(("arbitrary", "arbitrary")),
        name="ada_mod",
    )(cvec, w_ada, b_ada.reshape(depth, 1, n))


def _swiglu_half_step(x, rows, mod_ref, gain_ref, win_ref, wout_ref, *, ctx_len, d_ff, fc):
    hb = _norm_mod(x, gain_ref[...], mod_ref, rows, ctx_len).astype(BF16)
    acc = jnp.zeros(x.shape, F32)
    for j in range(d_ff // fc):
        gt = _dot(hb, win_ref[:, j * fc:(j + 1) * fc])
        up = _dot(hb, win_ref[:, d_ff + j * fc:d_ff + (j + 1) * fc])
        acc = acc + _dot((_silu(gt) * up).astype(BF16), wout_ref[j * fc:(j + 1) * fc, :])
    return x + (0.5 * _mod_gate(mod_ref, rows, ctx_len, x.shape[1])) * acc


_SEC = (("q", 512), ("k", 128), ("v", 128), ("gqkv", 3 * GDN_W), ("z", GDN_W), ("ab", LANES),
        ("pin", POOL_WIDTH), ("gl", N_BRANCH * 1024))
_SEC_SRC = (512, 128, 128, 3 * GDN_W, GDN_W, 4 * GDN_HEADS, POOL_WIDTH, N_BRANCH * 1024)


def _sec_offsets():
    offs, o = {}, 0
    for name, width in _SEC:
        offs[name] = (o, width)
        o += width
    return offs, o


def _rope(t, cos, sin):
    n = t.shape[-1]
    lane = lax.broadcasted_iota(jnp.int32, t.shape, 1)
    swapped = jnp.where(lane % 32 < 16, pltpu.roll(t, n - 16, 1), pltpu.roll(t, 16, 1))
    reps = n // LANES
    c = jnp.concatenate([cos] * reps, axis=1) if reps > 1 else cos
    s = jnp.concatenate([sin] * reps, axis=1) if reps > 1 else sin
    return t * c + swapped * s


def _dup_halves(t):
    lo = lax.broadcasted_iota(jnp.int32, t.shape, 1) < LANES // 2
    r = pltpu.roll(t, LANES // 2, 1)
    return jnp.concatenate([jnp.where(lo, t, r), jnp.where(lo, r, t)], axis=1)


def _ffn_proj_kernel(*refs, tm, ctx_len, d_ff, fc, first_layer):
    if first_layer:
        ctx_ref, refs = refs[0], refs[1:]
    (x_ref, modf_ref, gainf_ref, win_ref, wout_ref, modp_ref, gainp_ref, w_ref, cos_ref, sin_ref,
     xo_ref, q_ref, k_ref, v_ref, gqkv_ref, z_ref, ab_ref, pin_ref, gate_ref) = refs
    offs, _ = _sec_offsets()
    i = pl.program_id(0)
    rows = _row_ids(i * tm, tm)
    x = x_ref[...]
    if first_layer:
        x = jnp.where(i == 0, ctx_ref[...], x)
    x = _swiglu_half_step(x, rows, modf_ref, gainf_ref, win_ref, wout_ref, ctx_len=ctx_len, d_ff=d_ff, fc=fc)
    xo_ref[...] = x
    hb = _norm_mod(x, gainp_ref[...], modp_ref, rows, ctx_len).astype(BF16)

    def sec(name, out_ref, fn):
        o, width = offs[name]
        step = min(width, COL_CHUNK)
        for a in range(0, width, step):
            out_ref[:, a:a + step] = fn(_dot(hb, w_ref[:, o + a:o + a + step])).astype(out_ref.dtype)

    cos, sin = cos_ref[...], sin_ref[...]
    sec("q", q_ref, lambda t: _rope(t, cos, sin) * (ATTN_HEAD_DIM ** -0.5))
    o, width = offs["k"]
    k_ref[...] = _dup_halves(_rope(_dot(hb, w_ref[:, o:o + width]), cos, sin)).astype(BF16)
    o, width = offs["v"]
    v_ref[...] = _dup_halves(_dot(hb, w_ref[:, o:o + width])).astype(BF16)
    ident = lambda t: t
    sec("gqkv", gqkv_ref, ident)
    sec("z", z_ref, ident)
    sec("ab", ab_ref, ident)
    sec("pin", pin_ref, ident)
    sec("gl", gate_ref, jax.nn.sigmoid)


def _ffn_proj_call(x_in, ctx, mods, gainf, w_in, w_out, gainp, w, cos, sin, *, l, tm, ctx_len):
    first_layer = ctx is not None
    d = x_in.shape[1]
    la = x_in.shape[0] + (ctx_len if first_layer else 0)
    offs, _ = _sec_offsets()
    d_ff = w_out.shape[1]
    fc = d_ff // 2
    kern = functools.partial(_ffn_proj_kernel, tm=tm, ctx_len=ctx_len, d_ff=d_ff, fc=fc, first_layer=first_layer)
    row = lambda width: pl.BlockSpec((tm, width), lambda i: (i, 0))
    if first_layer:
        assert tm == ctx_len
        lead_specs = [_const_spec((tm, d)), pl.BlockSpec((tm, d), lambda i: (jnp.maximum(i - 1, 0), 0))]
        lead_args = (ctx, x_in)
        aliases = {}
    else:
        lead_specs = [row(d)]
        lead_args = (x_in,)
        aliases = {0: 0}
    outs = ((d, F32), (512, BF16), (2 * LANES, BF16), (2 * LANES, BF16), (3 * GDN_W, F32), (GDN_W, BF16),
            (LANES, F32), (POOL_WIDTH, F32), (offs["gl"][1], BF16))
    return pl.pallas_call(
        kern,
        grid=(la // tm,),
        in_specs=lead_specs + [_mod_spec(d, l, 0), _layer_spec(gainf, l), _layer_spec(w_in, l),
                               _layer_spec(w_out, l), _mod_spec(d, l, 1), _layer_spec(gainp, l),
                               _layer_spec(w, l), row(LANES), row(LANES)],
        out_specs=[row(width) for width, _ in outs],
        out_shape=[jax.ShapeDtypeStruct((la, width), dt) for width, dt in outs],
        input_output_aliases=aliases,
        compiler_params=_params(("parallel",)),
        name="ffn_proj",
    )(*lead_args, mods, gainf, w_in, w_out, mods, gainp, w, cos, sin)


def _attn_stages(sink_ref, q_ref, kp_ref, kc_ref, kn_ref, vp_ref, vc_ref, vn_ref, kx_ref, vx_ref,
                 o_ref, *, l, ctx_blocks, n_blocks):
    b0 = ATTN_STEP_BLOCKS * pl.program_id(0)
    blk = ATTN_BLOCK
    is_lat = b0 >= ctx_blocks
    ri = lax.broadcasted_iota(jnp.int32, (blk, blk), 0)
    ci = lax.broadcasted_iota(jnp.int32, (blk, blk), 1)
    ninf = -jnp.inf
    lo = lax.broadcasted_iota(jnp.int32, (blk, LANES), 1) < LANES // 2
    pairs = ATTN_HEADS // ATTN_KV_HEADS // 2
    groups = [(sb, hk) for sb in range(ATTN_STEP_BLOCKS) for hk in range(ATTN_KV_HEADS)]

    def window(ref_p, ref_c, ref_n, sb, ks):
        blocks = [ref_p[:, ks]] + [ref_c[j * blk:(j + 1) * blk, ks] for j in range(ATTN_STEP_BLOCKS)] + [ref_n[:, ks]]
        return jnp.concatenate(blocks[sb:sb + 3], axis=0)

    bias = []
    for sb in range(ATTN_STEP_BLOCKS):
        prev_ok = b0 + sb - 1 >= ctx_blocks
        next_ok = jnp.logical_and(is_lat, b0 + sb + 1 < n_blocks)
        bias.append(jnp.concatenate(
            [jnp.where(jnp.logical_and(ci >= ri, prev_ok), 0.0, ninf),
             jnp.where(jnp.broadcast_to(is_lat, (blk, blk)), 0.0, ninf),
             jnp.where(jnp.logical_and(ci <= ri, next_ok), 0.0, ninf)], axis=1))

    scores, values, sinks = [], [], []
    for sb, hk in groups:
        ks = slice(hk * LANES, (hk + 1) * LANES)
        rows = slice(sb * blk, (sb + 1) * blk)
        q_parts, sk = [], []
        for j in range(pairs):
            col = (hk * pairs + j) * LANES
            qp = q_ref[rows, col:col + LANES]
            q_parts += [jnp.where(lo, qp, jnp.zeros_like(qp)), jnp.where(lo, jnp.zeros_like(qp), qp)]
            sk += [sink_ref[l, (hk * pairs + j) * 2], sink_ref[l, (hk * pairs + j) * 2 + 1]]
        q4 = jnp.concatenate(q_parts, axis=0)
        scores.append((_dot_nt(q4, window(kp_ref, kc_ref, kn_ref, sb, ks)), _dot_nt(q4, kx_ref[:, ks])))
        values.append((window(vp_ref, vc_ref, vn_ref, sb, ks), vx_ref[:, ks]))
        sinks.append(sk)
        yield
    probs = []
    for g, (sb, hk) in enumerate(groups):
        s_win, s_ctx = scores[g]
        p_win, p_ctx, inv = [], [], []
        for j, sk in enumerate(sinks[g]):
            rs = slice(j * blk, (j + 1) * blk)
            sw = s_win[rs] + bias[sb]
            sx = s_ctx[rs]
            m = jnp.maximum(jnp.maximum(jnp.max(sw, axis=-1, keepdims=True),
                                        jnp.max(sx, axis=-1, keepdims=True)), sk)
            pw = jnp.exp(sw - m)
            px = jnp.exp(sx - m)
            denom = jnp.sum(pw, axis=-1, keepdims=True) + jnp.sum(px, axis=-1, keepdims=True) + jnp.exp(sk - m)
            p_win.append(pw.astype(BF16))
            p_ctx.append(px.astype(BF16))
            inv.append(1.0 / denom)
        probs.append((jnp.concatenate(p_win, axis=0), jnp.concatenate(p_ctx, axis=0), inv))
    for g, (sb, hk) in enumerate(groups):
        p_win, p_ctx, inv = probs[g]
        vw, vx = values[g]
        r = _dot(p_win, vw) + _dot(p_ctx, vx)
        for j in range(pairs):
            r0 = r[(2 * j) * blk:(2 * j + 1) * blk] * inv[2 * j]
            r1 = r[(2 * j + 1) * blk:(2 * j + 2) * blk] * inv[2 * j + 1]
            col = (hk * pairs + j) * LANES
            o_ref[sb * blk:(sb + 1) * blk, col:col + LANES] = jnp.where(lo, r0, r1).astype(BF16)
        yield


def _tri_inverse_masks(n):
    ri = lax.broadcasted_iota(jnp.int32, (n, n), 0)
    ci = lax.broadcasted_iota(jnp.int32, (n, n), 1)
    eye = (ri == ci).astype(F32)
    same = lambda s: (ri // s) == (ci // s)
    offs, s = [], GDN_BASE
    while s < n:
        offs.append(jnp.logical_and(same(2 * s), jnp.logical_not(same(s))))
        s *= 2
    return eye, same(GDN_BASE), offs


def _tri_inverse(a_list, masks, between=None):
    eye, diag, offs = masks
    a0 = [jnp.where(diag, a, 0.0) for a in a_list]
    a0b = [x.astype(BF16) for x in a0]
    t = [eye - x for x in a0]
    p = [_dot(x, x) for x in a0b]
    s = 2
    while True:
        pb = [x.astype(BF16) for x in p]
        t = [ti + _dot(ti.astype(BF16), pi) for ti, pi in zip(t, pb)]
        s *= 2
        if s >= GDN_BASE:
            break
        p = [_dot(x, x) for x in pb]
    if between is not None:
        between()
    for off in offs:
        tb = [x.astype(BF16) for x in t]
        mid = [_dot(ti, jnp.where(off, a, 0.0).astype(BF16)).astype(BF16) for ti, a in zip(tb, a_list)]
        t = [ti - _dot(mi, tbi) for ti, mi, tbi in zip(t, mid, tb)]
    return t


def _gdn_prep_kernel(prev_ref, cur_ref, next_ref, ab_ref, convw_ref, abp_ref,
                     u_ref, w_ref, qd_ref, a2_ref, gt_ref, xs_ref, a_s, rhs_s, *, tg, ctx_len, total, ntiles):
    i = pl.program_id(0)
    dk = GDN_HEAD_DIM
    c_len = GDN_CHUNK
    heads = range(GDN_HEADS)
    chunks = range(tg // c_len)
    units = [(c, h, d) for c in chunks for h in heads for d in range(2)]
    hsl = [slice(h * dk, (h + 1) * dk) for h in heads]

    @pl.when(i == 0)
    def _():
        a_s[1] = jnp.zeros(a_s.shape[1:], F32)
        rhs_s[1] = jnp.zeros(rhs_s.shape[1:], BF16)

    slot = i % 2
    row0 = jnp.minimum(i, ntiles - 1) * tg
    first = jnp.logical_or(row0 == 0, row0 == ctx_len)
    last = jnp.logical_or(row0 + tg == ctx_len, row0 + tg == total)
    xs_ref[0:HALO, :] = jnp.where(first, 0.0, prev_ref[...])
    xs_ref[HALO:HALO + tg, :] = cur_ref[...]
    xs_ref[HALO + tg:HALO + tg + HALO, :] = jnp.where(last, 0.0, next_ref[...])
    pad = (GDN_CONV - 1) // 2

    def conv_silu(col, r0):
        ls = slice(col * dk, (col + 1) * dk)
        acc = xs_ref[pl.ds(HALO - pad + r0, c_len), ls] * convw_ref[0:1, ls]
        for j in range(1, GDN_CONV):
            acc = acc + xs_ref[pl.ds(HALO - pad + j + r0, c_len), ls] * convw_ref[j:j + 1, ls]
        return _silu(acc)

    def l2n(t):
        return t * lax.rsqrt(jnp.sum(t * t, axis=-1, keepdims=True) + EPS)

    ri = lax.broadcasted_iota(jnp.int32, (c_len, c_len), 0)
    ci = lax.broadcasted_iota(jnp.int32, (c_len, c_len), 1)
    strict = (ci < ri, ci > ri)
    incl = (ci <= ri, ci >= ri)
    tri = (incl[0].astype(F32), incl[1].astype(F32))
    last_row = (c_len - 1, 0)
    masks = _tri_inverse_masks(c_len)
    f = {}

    def front_matmuls():
        g_all = [-jnp.exp(abp_ref[1:2, :]) * jax.nn.softplus(ab_ref[c * c_len:(c + 1) * c_len, :] + abp_ref[0:1, :])
                 for c in chunks]
        f["b_all"] = [jax.nn.sigmoid(ab_ref[c * c_len:(c + 1) * c_len, :]) for c in chunks]
        f["gcs"] = [[_dot(tri[d], g_all[c], HIGHEST) for d in range(2)] for c in chunks]
        f["gct"] = [[jnp.transpose(x) for x in f["gcs"][c]] for c in chunks]
        pairs = [(c, h) for c in chunks for h in heads]
        f["qs"] = {ch: l2n(conv_silu(ch[1], ch[0] * c_len)) * (dk ** -0.5) for ch in pairs}
        f["ks"] = {ch: l2n(conv_silu(GDN_HEADS + ch[1], ch[0] * c_len)) for ch in pairs}
        f["vs"] = {ch: conv_silu(2 * GDN_HEADS + ch[1], ch[0] * c_len) for ch in pairs}
        f["kts"] = {ch: jnp.transpose(f["ks"][ch]) for ch in pairs}
        ktb = {ch: f["kts"][ch].astype(BF16) for ch in pairs}
        f["kk"] = {ch: _dot(f["ks"][ch].astype(BF16), ktb[ch]) for ch in pairs}
        f["qkt"] = {ch: _dot(f["qs"][ch].astype(BF16), ktb[ch]) for ch in pairs}

    a_prev = [a_s[1 - slot, n] for n in range(len(units))]
    t_prev = _tri_inverse(a_prev, masks, between=front_matmuls)
    uw = [_dot(t_.astype(BF16), rhs_s[1 - slot, n]) for n, t_ in enumerate(t_prev)]
    for n, (c, h, d) in enumerate(units):
        rs = slice(c * c_len, (c + 1) * c_len)
        u_ref[d, rs, hsl[h]] = uw[n][:, 0:dk].astype(BF16)
        w_ref[d, rs, hsl[h]] = uw[n][:, dk:2 * dk].astype(BF16)

    for n, (c, h, d) in enumerate(units):
        rs = slice(c * c_len, (c + 1) * c_len)
        lane = d * GDN_HEADS + h
        gc = f["gcs"][c][d][:, lane:lane + 1]
        gr = f["gct"][c][d][lane:lane + 1, :]
        beta = f["b_all"][c][:, 2 * GDN_HEADS + lane:2 * GDN_HEADS + lane + 1]
        e = jnp.exp(jnp.where(incl[d], gc - gr, -jnp.inf))
        egc = jnp.exp(gc)
        g_last = gc[last_row[d]:last_row[d] + 1, :]
        a_s[slot, n] = jnp.where(strict[d], beta * f["kk"][(c, h)] * e, 0.0)
        rhs_s[slot, n, :, 0:dk] = (beta * f["vs"][(c, h)]).astype(BF16)
        rhs_s[slot, n, :, dk:2 * dk] = (beta * egc * f["ks"][(c, h)]).astype(BF16)
        qd_ref[d, rs, hsl[h]] = (f["qs"][(c, h)] * egc).astype(BF16)
        a2_ref[d, c, 0:c_len, hsl[h]] = (f["qkt"][(c, h)] * e).astype(BF16)
        a2_ref[d, c, c_len:2 * c_len, hsl[h]] = (f["kts"][(c, h)] * jnp.exp(g_last - gr)).astype(BF16)
        gt_ref[d, c, h:h + 1, :] = jnp.broadcast_to(jnp.exp(g_last), (1, LANES))


def _gdn_prep_call(gqkv, ab, conv_w, abp, *, l, tg, ctx_len):
    la, wq = gqkv.shape
    c_len = GDN_CHUNK
    nch = la // c_len
    cpt = tg // c_len
    hb = tg // HALO
    nhalo = la // HALO
    ntiles = la // tg
    n_units = cpt * GDN_HEADS * 2
    kern = functools.partial(_gdn_prep_kernel, tg=tg, ctx_len=ctx_len, total=la, ntiles=ntiles)
    cur = lambda i: jnp.minimum(i, ntiles - 1)
    prv = lambda i: jnp.maximum(i - 1, 0)
    rows_cur = pl.BlockSpec((2, tg, GDN_W), lambda i: (0, cur(i), 0))
    rows_prv = pl.BlockSpec((2, tg, GDN_W), lambda i: (0, prv(i), 0))
    return pl.pallas_call(
        kern,
        grid=(ntiles + 1,),
        in_specs=[pl.BlockSpec((HALO, wq), lambda i: (jnp.maximum(cur(i) * hb - 1, 0), 0)),
                  pl.BlockSpec((tg, wq), lambda i: (cur(i), 0)),
                  pl.BlockSpec((HALO, wq), lambda i: (jnp.minimum((cur(i) + 1) * hb, nhalo - 1), 0)),
                  pl.BlockSpec((tg, LANES), lambda i: (cur(i), 0)),
                  _layer_spec(conv_w, l), _layer_spec(abp, l)],
        out_specs=[rows_prv, rows_prv, rows_cur,
                   pl.BlockSpec((2, cpt, 2 * c_len, GDN_W), lambda i: (0, cur(i), 0, 0)),
                   pl.BlockSpec((2, cpt, GDN_HEADS, LANES), lambda i: (0, cur(i), 0, 0))],
        out_shape=[jax.ShapeDtypeStruct((2, la, GDN_W), BF16),
                   jax.ShapeDtypeStruct((2, la, GDN_W), BF16),
                   jax.ShapeDtypeStruct((2, la, GDN_W), BF16),
                   jax.ShapeDtypeStruct((2, nch, 2 * c_len, GDN_W), BF16),
                   jax.ShapeDtypeStruct((2, nch, GDN_HEADS, LANES), F32)],
        scratch_shapes=[pltpu.VMEM((tg + 2 * HALO, wq), F32),
                        pltpu.VMEM((2, n_units, c_len, c_len), F32),
                        pltpu.VMEM((2, n_units, c_len, 2 * GDN_HEAD_DIM), BF16)],
        compiler_params=_params(("arbitrary",)),
        name="gdn_prep",
    )(gqkv, gqkv, gqkv, ab, conv_w, abp)


def _scan_stages(uf, ub, wf, wb, qdf, qdb, a2f, a2b, gtf, gtb, of_ref, ob_ref, s_ref):
    dk = GDN_HEAD_DIM
    c_len = GDN_CHUNK
    dirs = ((uf, wf, qdf, a2f, gtf, of_ref), (ub, wb, qdb, a2b, gtb, ob_ref))
    chains = [(d, h) for d in range(2) for h in range(GDN_HEADS)]
    hs = [slice(h * dk, (h + 1) * dk) for h in range(GDN_HEADS)]
    s = [s_ref[d * GDN_HEADS + h] for d, h in chains]
    for step in range(SCAN_CHUNKS):
        sub = (step, SCAN_CHUNKS - 1 - step)
        rows = [slice(sub[d] * c_len, (sub[d] + 1) * c_len) for d in range(2)]
        r1 = [_dot(jnp.concatenate([dirs[d][1][rows[d], hs[h]], dirs[d][2][rows[d], hs[h]]], axis=0),
                   s_.astype(BF16)) for (d, h), s_ in zip(chains, s)]
        yield
        vb = [(dirs[d][0][rows[d], hs[h]].astype(F32) - r[0:c_len]).astype(BF16) for (d, h), r in zip(chains, r1)]
        r2 = [_dot(dirs[d][3][sub[d], :, hs[h]], v_) for (d, h), v_ in zip(chains, vb)]
        yield
        for (d, h), ra, rb in zip(chains, r1, r2):
            dirs[d][5][rows[d], hs[h]] = (ra[c_len:2 * c_len] + rb[0:c_len]).astype(BF16)
        s = [s_ * dirs[d][4][sub[d], h:h + 1, :] + rb[c_len:2 * c_len] for (d, h), s_, rb in zip(chains, s, r2)]
    for (d, h), s_ in zip(chains, s):
        s_ref[d * GDN_HEADS + h] = s_


def _attn_scan_kernel(*refs, l, ctx_blocks, n_blocks):
    attn_in, scan_in = refs[0:10], refs[10:20]
    o_attn, of_ref, ob_ref, s_ref = refs[20:24]

    @pl.when(pl.program_id(0) == 0)
    def _():
        s_ref[...] = jnp.zeros(s_ref.shape, F32)

    stages = [_scan_stages(*scan_in, of_ref, ob_ref, s_ref),
              _attn_stages(*attn_in, o_attn, l=l, ctx_blocks=ctx_blocks, n_blocks=n_blocks)]
    while stages:
        for g in list(stages):
            try:
                next(g)
            except StopIteration:
                stages.remove(g)


def _attn_scan_call(sink, q, k, v, u, w, qd, a2, gt, *, l, ctx_len):
    la = q.shape[0]
    blk = ATTN_BLOCK
    step = ATTN_STEP_BLOCKS
    nb = la // blk
    cb = ctx_len // blk
    rows = step * blk
    assert rows == SCAN_CHUNKS * GDN_CHUNK and la % rows == 0 and ctx_len % rows == 0
    nsteps = la // rows
    ncs = ctx_len // rows
    kern = functools.partial(_attn_scan_kernel, l=l, ctx_blocks=cb, n_blocks=nb)
    kvw = k.shape[1]
    prev = pl.BlockSpec((blk, kvw), lambda p: (jnp.maximum(step * p - 1, 0), 0))
    cur = pl.BlockSpec((rows, kvw), lambda p: (p, 0))
    nxt = pl.BlockSpec((blk, kvw), lambda p: (jnp.minimum(step * (p + 1), nb - 1), 0))
    cx = pl.BlockSpec((ctx_len, kvw), lambda p: (0, 0))

    def bwd(n):
        return jnp.where(n < ncs, ncs - 1 - n, nsteps + ncs - 1 - n)

    def both(make):
        return [make(0, lambda n: n), make(1, bwd)]

    r_specs = both(lambda d, f: pl.BlockSpec((None, rows, GDN_W), lambda n: (d, f(n), 0)))
    a_specs = both(lambda d, f: pl.BlockSpec((None, SCAN_CHUNKS, 2 * GDN_CHUNK, GDN_W), lambda n: (d, f(n), 0, 0)))
    g_specs = both(lambda d, f: pl.BlockSpec((None, SCAN_CHUNKS, GDN_HEADS, LANES), lambda n: (d, f(n), 0, 0)))
    return pl.pallas_call(
        kern,
        grid=(nsteps,),
        in_specs=[pl.BlockSpec(memory_space=pltpu.SMEM),
                  pl.BlockSpec((rows, q.shape[1]), lambda p: (p, 0)),
                  prev, cur, nxt, prev, cur, nxt, cx, cx] + r_specs + r_specs + r_specs + a_specs + g_specs,
        out_specs=[pl.BlockSpec((rows, q.shape[1]), lambda p: (p, 0)),
                   pl.BlockSpec((rows, GDN_W), lambda n: (n, 0)),
                   pl.BlockSpec((rows, GDN_W), lambda n: (bwd(n), 0))],
        out_shape=[jax.ShapeDtypeStruct(q.shape, BF16),
                   jax.ShapeDtypeStruct((la, GDN_W), BF16), jax.ShapeDtypeStruct((la, GDN_W), BF16)],
        scratch_shapes=[pltpu.VMEM((2 * GDN_HEADS, GDN_HEAD_DIM, GDN_HEAD_DIM), F32)],
        compiler_params=_params(("arbitrary",)),
        name="attn_scan",
    )(sink, q, k, k, k, v, v, v, k, v, u, u, w, w, qd, qd, a2, a2, gt, gt)


def _merge_ffn_kernel(x_ref, attn_ref, of_ref, ob_ref, z_ref, pprev_ref, pcur_ref, pnext_ref, gate_ref,
                      modm_ref, gnorm_ref, poolw_ref, pscale_ref, wba_ref, wbg_ref, wbp_ref, wout_ref,
                      modf_ref, gainf_ref, win_ref, wo2_ref, fgain_ref, out_ref, ps_ref, xn_ref,
                      *, tm, ctx_len, total, ntiles, d_ff, fc, final):
    i = pl.program_id(0)
    d = x_ref.shape[1]
    slot = i % 2

    @pl.when(i == 0)
    def _():
        xn_ref[1] = jnp.zeros(xn_ref.shape[1:], F32)

    row0 = jnp.minimum(i, ntiles - 1) * tm
    first = jnp.logical_or(row0 == 0, row0 == ctx_len)
    last = jnp.logical_or(row0 + tm == ctx_len, row0 + tm == total)
    ps_ref[0:HALO, :] = jnp.where(first, 0.0, pprev_ref[...])
    ps_ref[HALO:HALO + tm, :] = pcur_ref[...]
    ps_ref[HALO + tm:HALO + tm + HALO, :] = jnp.where(last, 0.0, pnext_ref[...])
    rows = _row_ids(row0, tm)
    is_ctx = rows < ctx_len
    seg_pos = jnp.where(is_ctx, rows, rows - ctx_len)
    seg_len = jnp.where(is_ctx, ctx_len, total - ctx_len)
    m = {}

    def merge_vector_prep():
        m["pool_r"] = []
        for gi, win in enumerate(POOL_SIZES):
            ls = slice(gi * POOL_GROUP, (gi + 1) * POOL_GROUP)
            s = ps_ref[pl.ds(HALO - win // 2, tm), ls]
            for o in range(1, win):
                s = s + ps_ref[pl.ds(HALO - win // 2 + o, tm), ls]
            cnt = (jnp.minimum(seg_pos + win // 2, seg_len) - jnp.maximum(seg_pos - win // 2, 0)).astype(F32)
            m["pool_r"].append((s / cnt - pcur_ref[:, ls]).astype(BF16))
        gdn = []
        for h in range(GDN_HEADS):
            hs = slice(h * GDN_HEAD_DIM, (h + 1) * GDN_HEAD_DIM)
            o = of_ref[:, hs].astype(F32) + ob_ref[:, hs].astype(F32)
            o = o * lax.rsqrt(jnp.mean(o * o, axis=-1, keepdims=True) + EPS) * gnorm_ref[...]
            gdn.append(o * _silu(z_ref[:, hs].astype(F32)))
        m["gdn_o"] = jnp.concatenate(gdn, axis=1).astype(BF16)

    def merge_branches():
        pooled = [_dot(r, poolw_ref[gi]) * pscale_ref[:, gi * POOL_GROUP:(gi + 1) * POOL_GROUP]
                  for gi, r in enumerate(m["pool_r"])]
        pool_o = jnp.concatenate(pooled, axis=1).astype(BF16)
        m["merged"] = (gate_ref[:, 0:d].astype(F32) * _dot(attn_ref[...], wba_ref[...])
                       + gate_ref[:, d:2 * d].astype(F32) * _dot(m["gdn_o"], wbg_ref[...])
                       + gate_ref[:, 2 * d:3 * d].astype(F32) * _dot(pool_o, wbp_ref[...])).astype(BF16)

    def merge_out():
        y = _dot(m["merged"], wout_ref[...])
        xn_ref[slot] = x_ref[...] + _mod_gate(modm_ref, rows, ctx_len, d) * y

    xp = xn_ref[1 - slot]
    rows_p = _row_ids((i - 1) * tm, tm)
    hb = _norm_mod(xp, gainf_ref[...], modf_ref, rows_p, ctx_len).astype(BF16)
    acc = jnp.zeros(xp.shape, F32)
    n_chunks = d_ff // fc
    for j in range(n_chunks):
        gt = _dot(hb, win_ref[:, j * fc:(j + 1) * fc])
        up = _dot(hb, win_ref[:, d_ff + j * fc:d_ff + (j + 1) * fc])
        if j == 0:
            merge_vector_prep()
        acc = acc + _dot((_silu(gt) * up).astype(BF16), wo2_ref[j * fc:(j + 1) * fc, :])
    merge_branches()
    merge_out()
    y = xp + (0.5 * _mod_gate(modf_ref, rows_p, ctx_len, d)) * acc
    if final:
        y = y * lax.rsqrt(jnp.mean(y * y, axis=-1, keepdims=True) + EPS) * fgain_ref[...]
    out_ref[...] = y


def _merge_ffn_call(xa, attn_o, o_f, o_b, z, pin, gates, mods, gnorm, pool_w, pool_scale, wba, wbg, wbp, wout,
                    gainf, w_in, w_out, fgain, *, l, tm, ctx_len, final):
    la, d = xa.shape
    hb = tm // HALO
    nhalo = la // HALO
    pw = pin.shape[1]
    ntiles = la // tm
    d_ff = w_out.shape[1]
    fc = d_ff // 2
    assert d_ff // fc == 2
    kern = functools.partial(_merge_ffn_kernel, tm=tm, ctx_len=ctx_len, total=la, ntiles=ntiles,
                             d_ff=d_ff, fc=fc, final=final)
    cur = lambda i: jnp.minimum(i, ntiles - 1)
    row = lambda width: pl.BlockSpec((tm, width), lambda i: (cur(i), 0))
    if final:
        skip = ctx_len // tm
        out_spec = pl.BlockSpec((tm, d), lambda i: (jnp.maximum(i - 1 - skip, 0), 0))
        out_shape = jax.ShapeDtypeStruct((la - ctx_len, d), F32)
        aliases = {}
    else:
        out_spec = pl.BlockSpec((tm, d), lambda i: (jnp.maximum(i - 1, 0), 0))
        out_shape = jax.ShapeDtypeStruct((la, d), F32)
        aliases = {0: 0}
    return pl.pallas_call(
        kern,
        grid=(ntiles + 1,),
        in_specs=[row(d), row(attn_o.shape[1]), row(GDN_W), row(GDN_W), row(GDN_W),
                  pl.BlockSpec((HALO, pw), lambda i: (jnp.maximum(cur(i) * hb - 1, 0), 0)),
                  row(pw),
                  pl.BlockSpec((HALO, pw), lambda i: (jnp.minimum((cur(i) + 1) * hb, nhalo - 1), 0)),
                  row(gates.shape[1]),
                  _mod_spec(d, l, 1), _layer_spec(gnorm, l), _layer_spec(pool_w, l), _layer_spec(pool_scale, l),
                  _layer_spec(wba, l), _layer_spec(wbg, l), _layer_spec(wbp, l), _layer_spec(wout, l),
                  _mod_spec(d, l, 2), _layer_spec(gainf, l), _layer_spec(w_in, l), _layer_spec(w_out, l),
                  _const_spec((1, d))],
        out_specs=out_spec,
        out_shape=out_shape,
        input_output_aliases=aliases,
        scratch_shapes=[pltpu.VMEM((tm + 2 * HALO, pw), F32), pltpu.VMEM((2, tm, d), F32)],
        compiler_params=_params(("arbitrary",)),
        name="merge_ffn",
    )(xa, attn_o, o_f, o_b, z, pin, pin, pin, gates, mods, gnorm, pool_w, pool_scale, wba, wbg, wbp, wout,
      mods, gainf, w_in, w_out, fgain)


def _rope_tables(seq, ctx_len):
    quarter = ATTN_HEAD_DIM // 4
    rows = seq // GRID_W
    inv = ROPE_THETA ** (-jnp.arange(quarter, dtype=F32) / quarter)

    def axis_tables(n):
        ang = jnp.arange(n, dtype=F32)[:, None] * inv[None, :]
        c, s = jnp.cos(ang), jnp.sin(ang)
        return jnp.concatenate([c, c], axis=1), jnp.concatenate([-s, s], axis=1)

    cr, sr = axis_tables(rows)
    cc, sc = axis_tables(GRID_W)

    def full(tr, tc):
        t = jnp.concatenate([jnp.broadcast_to(tr[:, None, :], (rows, GRID_W, 2 * quarter)),
                             jnp.broadcast_to(tc[None, :, :], (rows, GRID_W, 2 * quarter))], axis=-1)
        t = t.reshape(seq, ATTN_HEAD_DIM)
        return jnp.concatenate([t] * (LANES // ATTN_HEAD_DIM), axis=1)

    cos = jnp.concatenate([jnp.ones((ctx_len, LANES), F32), full(cr, cc)], axis=0)
    sin = jnp.concatenate([jnp.zeros((ctx_len, LANES), F32), full(sr, sc)], axis=0)
    return cos, sin


def _pack_kernel(w_ref, out_ref):
    src, dst, start = 0, 0, 0
    for sz, (_, width) in zip(_SEC_SRC, _SEC):
        src += sz
        dst += width
        if width != sz:
            run = src - start
            out_ref[:, dst - width - (run - sz):dst - width + sz] = w_ref[:, start:src].astype(BF16)
            out_ref[:, dst - width + sz:dst] = jnp.zeros((w_ref.shape[0], width - sz), BF16)
            start = src
    tail = w_ref.shape[1] - start
    out_ref[:, dst - tail:dst] = w_ref[:, start:].astype(BF16)


def _pack_w_in(w_in):
    depth, d, n = w_in.shape
    _, n_packed = _sec_offsets()
    tr = 256
    return pl.pallas_call(
        _pack_kernel,
        grid=(depth, d // tr),
        in_specs=[pl.BlockSpec((None, tr, n), lambda l, i: (l, i, 0))],
        out_specs=pl.BlockSpec((None, tr, n_packed), lambda l, i: (l, i, 0)),
        out_shape=jax.ShapeDtypeStruct((depth, d, n_packed), BF16),
        compiler_params=_params(("parallel", "parallel")),
        name="pack_w_in",
    )(w_in)


def kernel(x, c, ctx, c_ctx, w_ada, b_ada, norm_ffn1, w_ffn1_in, w_ffn1_out, norm_mix, w_in, attn_sink,
           gdn_conv, gdn_a_log, gdn_dt_bias, gdn_norm, pool_w, pool_scale, w_branch_attn, w_branch_gdn,
           w_branch_pool, w_out, norm_ffn2, w_ffn2_in, w_ffn2_out, final_norm):
    batch, seq, d = x.shape
    assert batch == 1, "single-sequence kernel"
    ctx_len = ctx.shape[1]
    depth = w_ada.shape[0]
    tm = ctx_len
    assert tm % (SCAN_CHUNKS * GDN_CHUNK) == 0 and seq % tm == 0 and seq % GRID_W == 0

    cvec = jnp.zeros((8, d), F32).at[0].set(c[0]).at[1].set(c_ctx)
    mods = _ada_call(cvec, w_ada, b_ada)
    cos, sin = _rope_tables(seq, ctx_len)

    bf = lambda w: w.astype(BF16)
    unit = lambda p: p[:, None, :]
    w1i, w1o, w2i, w2o = bf(w_ffn1_in), bf(w_ffn1_out), bf(w_ffn2_in), bf(w_ffn2_out)
    wproj = _pack_w_in(w_in)
    wba, wbg, wbp, wo, pw = bf(w_branch_attn), bf(w_branch_gdn), bf(w_branch_pool), bf(w_out), bf(pool_w)
    n1, nm, n2, gn, psc = unit(norm_ffn1), unit(norm_mix), unit(norm_ffn2), unit(gdn_norm), unit(pool_scale)
    abp = jnp.stack([gdn_dt_bias.reshape(depth, -1), gdn_a_log.reshape(depth, -1)], axis=1)
    abp = jnp.pad(abp, ((0, 0), (0, 6), (0, LANES - 2 * GDN_HEADS)))

    for l in range(depth):
        xa, q, k, v, gqkv, z, ab, pin, gates = _ffn_proj_call(
            x[0] if l == 0 else xa, ctx[0] if l == 0 else None, mods, n1, w1i, w1o, nm, wproj, cos, sin,
            l=l, tm=tm, ctx_len=ctx_len)
        u, w, qd, a2, gt = _gdn_prep_call(gqkv, ab, gdn_conv, abp, l=l, tg=tm, ctx_len=ctx_len)
        attn_o, o_f, o_b = _attn_scan_call(attn_sink, q, k, v, u, w, qd, a2, gt, l=l, ctx_len=ctx_len)
        xa = _merge_ffn_call(xa, attn_o, o_f, o_b, z, pin, gates, mods, gn, pw, psc, wba, wbg, wbp, wo,
                             n2, w2i, w2o, final_norm[None], l=l, tm=tm, ctx_len=ctx_len, final=l == depth - 1)
    return xa[None]
```

```python
import functools

import jax
import jax.numpy as jnp
from jax import lax
from jax.experimental import pallas as pl
from jax.experimental.pallas import tpu as pltpu

F32 = jnp.float32
BF16 = jnp.bfloat16
HIGHEST = lax.Precision.HIGHEST

GRID_W = 64
ATTN_HEADS = 8
ATTN_KV_HEADS = 2
ATTN_HEAD_DIM = 64
ATTN_BLOCK = 128
ATTN_STEP_BLOCKS = 2
ROPE_THETA = 10000.0
GDN_HEADS = 4
GDN_HEAD_DIM = 128
GDN_W = GDN_HEADS * GDN_HEAD_DIM
GDN_CONV = 5
GDN_CHUNK = 128
GDN_BASE = 16
SCAN_CHUNKS = 2
POOL_SIZES = (2, 4, 8, 16)
POOL_GROUP = 128
POOL_WIDTH = POOL_GROUP * len(POOL_SIZES)
N_MOD = 9
N_BRANCH = 3
EPS = 1e-6

LANES = 128
HALO = 8
VMEM_LIMIT = 56 * 1024 * 1024
COL_CHUNK = 512


def _dot(a, b, precision=None):
    return jnp.dot(a, b, preferred_element_type=F32, precision=precision)


def _dot_nt(a, b):
    return lax.dot_general(a, b, (((1,), (1,)), ((), ())), preferred_element_type=F32)


def _silu(x):
    return x * jax.nn.sigmoid(x)


def _params(sem):
    return pltpu.CompilerParams(dimension_semantics=sem, vmem_limit_bytes=VMEM_LIMIT)


def _const_spec(shape):
    nd = len(shape)
    return pl.BlockSpec(shape, lambda *_: (0,) * nd, pipeline_mode=pl.Buffered(1))


def _layer_spec(arr, l):
    nd = arr.ndim - 1
    return pl.BlockSpec((None,) + arr.shape[1:], lambda *_: (l,) + (0,) * nd, pipeline_mode=pl.Buffered(1))


def _mod_spec(d, l, j):
    return pl.BlockSpec((None, 8, 3 * d), lambda *_: (l, 0, j), pipeline_mode=pl.Buffered(1))


def _norm_mod(x, gain, mod_ref, rows, ctx_len):
    d = x.shape[1]
    y = x * lax.rsqrt(jnp.mean(x * x, axis=-1, keepdims=True) + EPS) * gain
    is_ctx = rows < ctx_len
    shift = jnp.where(is_ctx, mod_ref[1:2, 0:d], mod_ref[0:1, 0:d])
    scale = jnp.where(is_ctx, mod_ref[1:2, d:2 * d], mod_ref[0:1, d:2 * d])
    return y * (1.0 + scale) + shift


def _mod_gate(mod_ref, rows, ctx_len, d):
    return jnp.where(rows < ctx_len, mod_ref[1:2, 2 * d:3 * d], mod_ref[0:1, 2 * d:3 * d])


def _row_ids(row0, n):
    return row0 + lax.broadcasted_iota(jnp.int32, (n, 1), 0)


def _ada_kernel(c_ref, w_ref, b_ref, out_ref):
    w = w_ref[...]
    rows = [jnp.sum(w * _silu(c_ref[:, r:r + 1]), axis=0, keepdims=True) for r in range(2)]
    out_ref[...] = jnp.concatenate(rows + [jnp.zeros((6, w.shape[1]), F32)], axis=0) + b_ref[...]


def _ada_call(c_cols, w_ada, b_ada):
    depth, d, n = w_ada.shape
    tn = n // 8
    return pl.pallas_call(
        _ada_kernel,
        grid=(depth, n // tn),
        in_specs=[pl.BlockSpec((d, LANES), lambda l, j: (0, 0)),
                  pl.BlockSpec((None, d, tn), lambda l, j: (l, 0, j)),
                  pl.BlockSpec((None, 1, tn), lambda l, j: (l, 0, j))],
        out_specs=pl.BlockSpec((None, 8, tn), lambda l, j: (l, 0, j)),
        out_shape=jax.ShapeDtypeStruct((depth, 8, n), F32),
        compiler_params=_params(("parallel", "parallel")),
        name="ada_mod",
    )(c_cols, w_ada, b_ada.reshape(depth, 1, n))


def _swiglu_half_step(x, rows, mod_ref, gain_ref, win_ref, wout_ref, *, ctx_len, d_ff, fc):
    hb = _norm_mod(x, gain_ref[...], mod_ref, rows, ctx_len).astype(BF16)
    acc = jnp.zeros(x.shape, F32)
    for j in range(d_ff // fc):
        gt = _dot(hb, win_ref[:, j * fc:(j + 1) * fc])
        up = _dot(hb, win_ref[:, d_ff + j * fc:d_ff + (j + 1) * fc])
        acc = acc + _dot((_silu(gt) * up).astype(BF16), wout_ref[j * fc:(j + 1) * fc, :])
    return x + (0.5 * _mod_gate(mod_ref, rows, ctx_len, x.shape[1])) * acc


_SEC = (("q", 512), ("k", 128), ("v", 128), ("gqkv", 3 * GDN_W), ("z", GDN_W), ("ab", LANES),
        ("pin", POOL_WIDTH), ("gl", N_BRANCH * 1024))
_SEC_SRC = (512, 128, 128, 3 * GDN_W, GDN_W, 4 * GDN_HEADS, POOL_WIDTH, N_BRANCH * 1024)


def _sec_offsets():
    offs, o = {}, 0
    for name, width in _SEC:
        offs[name] = (o, width)
        o += width
    return offs, o


def _rope(t, cos, sin):
    n = t.shape[-1]
    lane = lax.broadcasted_iota(jnp.int32, t.shape, 1)
    swapped = jnp.where(lane % 32 < 16, pltpu.roll(t, n - 16, 1), pltpu.roll(t, 16, 1))
    reps = n // LANES
    c = jnp.concatenate([cos] * reps, axis=1) if reps > 1 else cos
    s = jnp.concatenate([sin] * reps, axis=1) if reps > 1 else sin
    return t * c + swapped * s


def _dup_halves(t):
    lo = lax.broadcasted_iota(jnp.int32, t.shape, 1) < LANES // 2
    r = pltpu.roll(t, LANES // 2, 1)
    return jnp.concatenate([jnp.where(lo, t, r), jnp.where(lo, r, t)], axis=1)


def _ffn_proj_kernel(*refs, tm, ctx_len, d_ff, fc, first_layer):
    if first_layer:
        ctx_ref, refs = refs[0], refs[1:]
    (x_ref, modf_ref, gainf_ref, win_ref, wout_ref, modp_ref, gainp_ref, w_ref, cos_ref, sin_ref,
     xo_ref, q_ref, k_ref, v_ref, gqkv_ref, z_ref, ab_ref, pin_ref, gate_ref) = refs
    offs, _ = _sec_offsets()
    i = pl.program_id(0)
    rows = _row_ids(i * tm, tm)
    x = x_ref[...]
    if first_layer:
        x = jnp.where(i == 0, ctx_ref[...], x)
    x = _swiglu_half_step(x, rows, modf_ref, gainf_ref, win_ref, wout_ref, ctx_len=ctx_len, d_ff=d_ff, fc=fc)
    xo_ref[...] = x
    hb = _norm_mod(x, gainp_ref[...], modp_ref, rows, ctx_len).astype(BF16)

    def sec(name, out_ref, fn):
        o, width = offs[name]
        step = min(width, COL_CHUNK)
        for a in range(0, width, step):
            out_ref[:, a:a + step] = fn(_dot(hb, w_ref[:, o + a:o + a + step])).astype(out_ref.dtype)

    cos, sin = cos_ref[...], sin_ref[...]
    sec("q", q_ref, lambda t: _rope(t, cos, sin) * (ATTN_HEAD_DIM ** -0.5))
    o, width = offs["k"]
    k_ref[...] = _dup_halves(_rope(_dot(hb, w_ref[:, o:o + width]), cos, sin)).astype(BF16)
    o, width = offs["v"]
    v_ref[...] = _dup_halves(_dot(hb, w_ref[:, o:o + width])).astype(BF16)
    ident = lambda t: t
    sec("gqkv", gqkv_ref, ident)
    sec("z", z_ref, ident)
    sec("ab", ab_ref, ident)
    sec("pin", pin_ref, ident)
    sec("gl", gate_ref, jax.nn.sigmoid)


def _ffn_proj_call(x_in, ctx, mods, gainf, w_in, w_out, gainp, w, cos, sin, *, l, tm, ctx_len):
    first_layer = ctx is not None
    d = x_in.shape[1]
    la = x_in.shape[0] + (ctx_len if first_layer else 0)
    offs, _ = _sec_offsets()
    d_ff = w_out.shape[1]
    fc = d_ff // 2
    kern = functools.partial(_ffn_proj_kernel, tm=tm, ctx_len=ctx_len, d_ff=d_ff, fc=fc, first_layer=first_layer)
    row = lambda width: pl.BlockSpec((tm, width), lambda i: (i, 0))
    if first_layer:
        assert tm == ctx_len
        lead_specs = [_const_spec((tm, d)), pl.BlockSpec((tm, d), lambda i: (jnp.maximum(i - 1, 0), 0))]
        lead_args = (ctx, x_in)
        aliases = {}
    else:
        lead_specs = [row(d)]
        lead_args = (x_in,)
        aliases = {0: 0}
    outs = ((d, F32), (512, BF16), (2 * LANES, BF16), (2 * LANES, BF16), (3 * GDN_W, F32), (GDN_W, BF16),
            (LANES, F32), (POOL_WIDTH, F32), (offs["gl"][1], BF16))
    return pl.pallas_call(
        kern,
        grid=(la // tm,),
        in_specs=lead_specs + [_mod_spec(d, l, 0), _layer_spec(gainf, l), _layer_spec(w_in, l),
                               _layer_spec(w_out, l), _mod_spec(d, l, 1), _layer_spec(gainp, l),
                               _layer_spec(w, l), row(LANES), row(LANES)],
        out_specs=[row(width) for width, _ in outs],
        out_shape=[jax.ShapeDtypeStruct((la, width), dt) for width, dt in outs],
        input_output_aliases=aliases,
        compiler_params=_params(("parallel",)),
        name="ffn_proj",
    )(*lead_args, mods, gainf, w_in, w_out, mods, gainp, w, cos, sin)


def _attn_stages(sink_ref, q_ref, kp_ref, kc_ref, kn_ref, vp_ref, vc_ref, vn_ref, kx_ref, vx_ref,
                 o_ref, *, l, ctx_blocks, n_blocks):
    b0 = ATTN_STEP_BLOCKS * pl.program_id(0)
    blk = ATTN_BLOCK
    is_lat = b0 >= ctx_blocks
    ri = lax.broadcasted_iota(jnp.int32, (blk, blk), 0)
    ci = lax.broadcasted_iota(jnp.int32, (blk, blk), 1)
    ninf = -jnp.inf
    lo = lax.broadcasted_iota(jnp.int32, (blk, LANES), 1) < LANES // 2
    pairs = ATTN_HEADS // ATTN_KV_HEADS // 2
    groups = [(sb, hk) for sb in range(ATTN_STEP_BLOCKS) for hk in range(ATTN_KV_HEADS)]

    def window(ref_p, ref_c, ref_n, sb, ks):
        blocks = [ref_p[:, ks]] + [ref_c[j * blk:(j + 1) * blk, ks] for j in range(ATTN_STEP_BLOCKS)] + [ref_n[:, ks]]
        return jnp.concatenate(blocks[sb:sb + 3], axis=0)

    bias = []
    for sb in range(ATTN_STEP_BLOCKS):
        prev_ok = b0 + sb - 1 >= ctx_blocks
        next_ok = jnp.logical_and(is_lat, b0 + sb + 1 < n_blocks)
        bias.append(jnp.concatenate(
            [jnp.where(jnp.logical_and(ci >= ri, prev_ok), 0.0, ninf),
             jnp.where(jnp.broadcast_to(is_lat, (blk, blk)), 0.0, ninf),
             jnp.where(jnp.logical_and(ci <= ri, next_ok), 0.0, ninf)], axis=1))

    scores, values, sinks = [], [], []
    for sb, hk in groups:
        ks = slice(hk * LANES, (hk + 1) * LANES)
        rows = slice(sb * blk, (sb + 1) * blk)
        q_parts, sk = [], []
        for j in range(pairs):
            col = (hk * pairs + j) * LANES
            qp = q_ref[rows, col:col + LANES]
            q_parts += [jnp.where(lo, qp, jnp.zeros_like(qp)), jnp.where(lo, jnp.zeros_like(qp), qp)]
            sk += [sink_ref[l, (hk * pairs + j) * 2], sink_ref[l, (hk * pairs + j) * 2 + 1]]
        q4 = jnp.concatenate(q_parts, axis=0)
        scores.append((_dot_nt(q4, window(kp_ref, kc_ref, kn_ref, sb, ks)), _dot_nt(q4, kx_ref[:, ks])))
        values.append((window(vp_ref, vc_ref, vn_ref, sb, ks), vx_ref[:, ks]))
        sinks.append(sk)
        yield
    probs = []
    for g, (sb, hk) in enumerate(groups):
        s_win, s_ctx = scores[g]
        p_win, p_ctx, inv = [], [], []
        for j, sk in enumerate(sinks[g]):
            rs = slice(j * blk, (j + 1) * blk)
            sw = s_win[rs] + bias[sb]
            sx = s_ctx[rs]
            m = jnp.maximum(jnp.maximum(jnp.max(sw, axis=-1, keepdims=True),
                                        jnp.max(sx, axis=-1, keepdims=True)), sk)
            pw = jnp.exp(sw - m)
            px = jnp.exp(sx - m)
            denom = jnp.sum(pw, axis=-1, keepdims=True) + jnp.sum(px, axis=-1, keepdims=True) + jnp.exp(sk - m)
            p_win.append(pw.astype(BF16))
            p_ctx.append(px.astype(BF16))
            inv.append(1.0 / denom)
        probs.append((jnp.concatenate(p_win, axis=0), jnp.concatenate(p_ctx, axis=0), inv))
    for g, (sb, hk) in enumerate(groups):
        p_win, p_ctx, inv = probs[g]
        vw, vx = values[g]
        r = _dot(p_win, vw) + _dot(p_ctx, vx)
        for j in range(pairs):
            r0 = r[(2 * j) * blk:(2 * j + 1) * blk] * inv[2 * j]
            r1 = r[(2 * j + 1) * blk:(2 * j + 2) * blk] * inv[2 * j + 1]
            col = (hk * pairs + j) * LANES
            o_ref[sb * blk:(sb + 1) * blk, col:col + LANES] = jnp.where(lo, r0, r1).astype(BF16)
        yield


def _tri_inverse_masks(n):
    ri = lax.broadcasted_iota(jnp.int32, (n, n), 0)
    ci = lax.broadcasted_iota(jnp.int32, (n, n), 1)
    eye = (ri == ci).astype(F32)
    same = lambda s: (ri // s) == (ci // s)
    offs, s = [], GDN_BASE
    while s < n:
        offs.append(jnp.logical_and(same(2 * s), jnp.logical_not(same(s))))
        s *= 2
    return eye, same(GDN_BASE), offs


def _tri_inverse(a_list, masks, between=None):
    eye, diag, offs = masks
    a0 = [jnp.where(diag, a, 0.0) for a in a_list]
    a0b = [x.astype(BF16) for x in a0]
    t = [eye - x for x in a0]
    p = [_dot(x, x) for x in a0b]
    s = 2
    while True:
        pb = [x.astype(BF16) for x in p]
        t = [ti + _dot(ti.astype(BF16), pi) for ti, pi in zip(t, pb)]
        s *= 2
        if s >= GDN_BASE:
            break
        p = [_dot(x, x) for x in pb]
    if between is not None:
        between()
    for off in offs:
        tb = [x.astype(BF16) for x in t]
        mid = [_dot(ti, jnp.where(off, a, 0.0).astype(BF16)).astype(BF16) for ti, a in zip(tb, a_list)]
        t = [ti - _dot(mi, tbi) for ti, mi, tbi in zip(t, mid, tb)]
    return t


def _gdn_prep_kernel(prev_ref, cur_ref, next_ref, ab_ref, convw_ref, abp_ref,
                     u_ref, w_ref, qd_ref, a2_ref, gt_ref, xs_ref, a_s, rhs_s, *, tg, ctx_len, total, ntiles):
    i = pl.program_id(0)
    dk = GDN_HEAD_DIM
    c_len = GDN_CHUNK
    heads = range(GDN_HEADS)
    chunks = range(tg // c_len)
    units = [(c, h, d) for c in chunks for h in heads for d in range(2)]
    hsl = [slice(h * dk, (h + 1) * dk) for h in heads]

    @pl.when(i == 0)
    def _():
        a_s[1] = jnp.zeros(a_s.shape[1:], F32)
        rhs_s[1] = jnp.zeros(rhs_s.shape[1:], BF16)

    slot = i % 2
    row0 = jnp.minimum(i, ntiles - 1) * tg
    first = jnp.logical_or(row0 == 0, row0 == ctx_len)
    last = jnp.logical_or(row0 + tg == ctx_len, row0 + tg == total)
    xs_ref[0:HALO, :] = jnp.where(first, 0.0, prev_ref[...])
    xs_ref[HALO:HALO + tg, :] = cur_ref[...]
    xs_ref[HALO + tg:HALO + tg + HALO, :] = jnp.where(last, 0.0, next_ref[...])
    pad = (GDN_CONV - 1) // 2

    def conv_silu(col, r0):
        ls = slice(col * dk, (col + 1) * dk)
        acc = xs_ref[pl.ds(HALO - pad + r0, c_len), ls] * convw_ref[0:1, ls]
        for j in range(1, GDN_CONV):
            acc = acc + xs_ref[pl.ds(HALO - pad + j + r0, c_len), ls] * convw_ref[j:j + 1, ls]
        return _silu(acc)

    def l2n(t):
        return t * lax.rsqrt(jnp.sum(t * t, axis=-1, keepdims=True) + EPS)

    ri = lax.broadcasted_iota(jnp.int32, (c_len, c_len), 0)
    ci = lax.broadcasted_iota(jnp.int32, (c_len, c_len), 1)
    strict = (ci < ri, ci > ri)
    incl = (ci <= ri, ci >= ri)
    tri = (incl[0].astype(F32), incl[1].astype(F32))
    last_row = (c_len - 1, 0)
    masks = _tri_inverse_masks(c_len)
    f = {}

    def front_matmuls():
        g_all = [-jnp.exp(abp_ref[1:2, :]) * jax.nn.softplus(ab_ref[c * c_len:(c + 1) * c_len, :] + abp_ref[0:1, :])
                 for c in chunks]
        f["b_all"] = [jax.nn.sigmoid(ab_ref[c * c_len:(c + 1) * c_len, :]) for c in chunks]
        f["gcs"] = [[_dot(tri[d], g_all[c], HIGHEST) for d in range(2)] for c in chunks]
        f["gct"] = [[jnp.transpose(x) for x in f["gcs"][c]] for c in chunks]
        pairs = [(c, h) for c in chunks for h in heads]
        f["qs"] = {ch: l2n(conv_silu(ch[1], ch[0] * c_len)) * (dk ** -0.5) for ch in pairs}
        f["ks"] = {ch: l2n(conv_silu(GDN_HEADS + ch[1], ch[0] * c_len)) for ch in pairs}
        f["vs"] = {ch: conv_silu(2 * GDN_HEADS + ch[1], ch[0] * c_len) for ch in pairs}
        f["kts"] = {ch: jnp.transpose(f["ks"][ch]) for ch in pairs}
        ktb = {ch: f["kts"][ch].astype(BF16) for ch in pairs}
        f["kk"] = {ch: _dot(f["ks"][ch].astype(BF16), ktb[ch]) for ch in pairs}
        f["qkt"] = {ch: _dot(f["qs"][ch].astype(BF16), ktb[ch]) for ch in pairs}

    a_prev = [a_s[1 - slot, n] for n in range(len(units))]
    t_prev = _tri_inverse(a_prev, masks, between=front_matmuls)
    uw = [_dot(t_.astype(BF16), rhs_s[1 - slot, n]) for n, t_ in enumerate(t_prev)]
    for n, (c, h, d) in enumerate(units):
        rs = slice(c * c_len, (c + 1) * c_len)
        u_ref[d, rs, hsl[h]] = uw[n][:, 0:dk].astype(BF16)
        w_ref[d, rs, hsl[h]] = uw[n][:, dk:2 * dk].astype(BF16)

    for n, (c, h, d) in enumerate(units):
        rs = slice(c * c_len, (c + 1) * c_len)
        lane = d * GDN_HEADS + h
        gc = f["gcs"][c][d][:, lane:lane + 1]
        gr = f["gct"][c][d][lane:lane + 1, :]
        beta = f["b_all"][c][:, 2 * GDN_HEADS + lane:2 * GDN_HEADS + lane + 1]
        e = jnp.exp(jnp.where(incl[d], gc - gr, -jnp.inf))
        egc = jnp.exp(gc)
        g_last = gc[last_row[d]:last_row[d] + 1, :]
        a_s[slot, n] = jnp.where(strict[d], beta * f["kk"][(c, h)] * e, 0.0)
        rhs_s[slot, n, :, 0:dk] = (beta * f["vs"][(c, h)]).astype(BF16)
        rhs_s[slot, n, :, dk:2 * dk] = (beta * egc * f["ks"][(c, h)]).astype(BF16)
        qd_ref[d, rs, hsl[h]] = (f["qs"][(c, h)] * egc).astype(BF16)
        a2_ref[d, c, 0:c_len, hsl[h]] = (f["qkt"][(c, h)] * e).astype(BF16)
        a2_ref[d, c, c_len:2 * c_len, hsl[h]] = (f["kts"][(c, h)] * jnp.exp(g_last - gr)).astype(BF16)
        gt_ref[d, c, h:h + 1, :] = jnp.broadcast_to(jnp.exp(g_last), (1, LANES))


def _gdn_prep_call(gqkv, ab, conv_w, abp, *, l, tg, ctx_len):
    la, wq = gqkv.shape
    c_len = GDN_CHUNK
    nch = la // c_len
    cpt = tg // c_len
    hb = tg // HALO
    nhalo = la // HALO
    ntiles = la // tg
    n_units = cpt * GDN_HEADS * 2
    kern = functools.partial(_gdn_prep_kernel, tg=tg, ctx_len=ctx_len, total=la, ntiles=ntiles)
    cur = lambda i: jnp.minimum(i, ntiles - 1)
    prv = lambda i: jnp.maximum(i - 1, 0)
    rows_cur = pl.BlockSpec((2, tg, GDN_W), lambda i: (0, cur(i), 0))
    rows_prv = pl.BlockSpec((2, tg, GDN_W), lambda i: (0, prv(i), 0))
    return pl.pallas_call(
        kern,
        grid=(ntiles + 1,),
        in_specs=[pl.BlockSpec((HALO, wq), lambda i: (jnp.maximum(cur(i) * hb - 1, 0), 0)),
                  pl.BlockSpec((tg, wq), lambda i: (cur(i), 0)),
                  pl.BlockSpec((HALO, wq), lambda i: (jnp.minimum((cur(i) + 1) * hb, nhalo - 1), 0)),
                  pl.BlockSpec((tg, LANES), lambda i: (cur(i), 0)),
                  _layer_spec(conv_w, l), _layer_spec(abp, l)],
        out_specs=[rows_prv, rows_prv, rows_cur,
                   pl.BlockSpec((2, cpt, 2 * c_len, GDN_W), lambda i: (0, cur(i), 0, 0)),
                   pl.BlockSpec((2, cpt, GDN_HEADS, LANES), lambda i: (0, cur(i), 0, 0))],
        out_shape=[jax.ShapeDtypeStruct((2, la, GDN_W), BF16),
                   jax.ShapeDtypeStruct((2, la, GDN_W), BF16),
                   jax.ShapeDtypeStruct((2, la, GDN_W), BF16),
                   jax.ShapeDtypeStruct((2, nch, 2 * c_len, GDN_W), BF16),
                   jax.ShapeDtypeStruct((2, nch, GDN_HEADS, LANES), F32)],
        scratch_shapes=[pltpu.VMEM((tg + 2 * HALO, wq), F32),
                        pltpu.VMEM((2, n_units, c_len, c_len), F32),
                        pltpu.VMEM((2, n_units, c_len, 2 * GDN_HEAD_DIM), BF16)],
        compiler_params=_params(("arbitrary",)),
        name="gdn_prep",
    )(gqkv, gqkv, gqkv, ab, conv_w, abp)


def _scan_stages(uf, ub, wf, wb, qdf, qdb, a2f, a2b, gtf, gtb, of_ref, ob_ref, s_ref):
    dk = GDN_HEAD_DIM
    c_len = GDN_CHUNK
    dirs = ((uf, wf, qdf, a2f, gtf, of_ref), (ub, wb, qdb, a2b, gtb, ob_ref))
    chains = [(d, h) for d in range(2) for h in range(GDN_HEADS)]
    hs = [slice(h * dk, (h + 1) * dk) for h in range(GDN_HEADS)]
    s = [s_ref[d * GDN_HEADS + h] for d, h in chains]
    for step in range(SCAN_CHUNKS):
        sub = (step, SCAN_CHUNKS - 1 - step)
        rows = [slice(sub[d] * c_len, (sub[d] + 1) * c_len) for d in range(2)]
        r1 = [_dot(jnp.concatenate([dirs[d][1][rows[d], hs[h]], dirs[d][2][rows[d], hs[h]]], axis=0),
                   s_.astype(BF16)) for (d, h), s_ in zip(chains, s)]
        yield
        vb = [(dirs[d][0][rows[d], hs[h]].astype(F32) - r[0:c_len]).astype(BF16) for (d, h), r in zip(chains, r1)]
        r2 = [_dot(dirs[d][3][sub[d], :, hs[h]], v_) for (d, h), v_ in zip(chains, vb)]
        yield
        for (d, h), ra, rb in zip(chains, r1, r2):
            dirs[d][5][rows[d], hs[h]] = (ra[c_len:2 * c_len] + rb[0:c_len]).astype(BF16)
        s = [s_ * dirs[d][4][sub[d], h:h + 1, :] + rb[c_len:2 * c_len] for (d, h), s_, rb in zip(chains, s, r2)]
    for (d, h), s_ in zip(chains, s):
        s_ref[d * GDN_HEADS + h] = s_


def _attn_scan_kernel(*refs, l, ctx_blocks, n_blocks):
    attn_in, scan_in = refs[0:10], refs[10:20]
    o_attn, of_ref, ob_ref, s_ref = refs[20:24]

    @pl.when(pl.program_id(0) == 0)
    def _():
        s_ref[...] = jnp.zeros(s_ref.shape, F32)

    stages = [_scan_stages(*scan_in, of_ref, ob_ref, s_ref),
              _attn_stages(*attn_in, o_attn, l=l, ctx_blocks=ctx_blocks, n_blocks=n_blocks)]
    while stages:
        for g in list(stages):
            try:
                next(g)
            except StopIteration:
                stages.remove(g)


def _attn_scan_call(sink, q, k, v, u, w, qd, a2, gt, *, l, ctx_len):
    la = q.shape[0]
    blk = ATTN_BLOCK
    step = ATTN_STEP_BLOCKS
    nb = la // blk
    cb = ctx_len // blk
    rows = step * blk
    assert rows == SCAN_CHUNKS * GDN_CHUNK and la % rows == 0 and ctx_len % rows == 0
    nsteps = la // rows
    ncs = ctx_len // rows
    kern = functools.partial(_attn_scan_kernel, l=l, ctx_blocks=cb, n_blocks=nb)
    kvw = k.shape[1]
    prev = pl.BlockSpec((blk, kvw), lambda p: (jnp.maximum(step * p - 1, 0), 0))
    cur = pl.BlockSpec((rows, kvw), lambda p: (p, 0))
    nxt = pl.BlockSpec((blk, kvw), lambda p: (jnp.minimum(step * (p + 1), nb - 1), 0))
    cx = pl.BlockSpec((ctx_len, kvw), lambda p: (0, 0))

    def bwd(n):
        return jnp.where(n < ncs, ncs - 1 - n, nsteps + ncs - 1 - n)

    def both(make):
        return [make(0, lambda n: n), make(1, bwd)]

    r_specs = both(lambda d, f: pl.BlockSpec((None, rows, GDN_W), lambda n: (d, f(n), 0)))
    a_specs = both(lambda d, f: pl.BlockSpec((None, SCAN_CHUNKS, 2 * GDN_CHUNK, GDN_W), lambda n: (d, f(n), 0, 0)))
    g_specs = both(lambda d, f: pl.BlockSpec((None, SCAN_CHUNKS, GDN_HEADS, LANES), lambda n: (d, f(n), 0, 0)))
    return pl.pallas_call(
        kern,
        grid=(nsteps,),
        in_specs=[pl.BlockSpec(memory_space=pltpu.SMEM),
                  pl.BlockSpec((rows, q.shape[1]), lambda p: (p, 0)),
                  prev, cur, nxt, prev, cur, nxt, cx, cx] + r_specs + r_specs + r_specs + a_specs + g_specs,
        out_specs=[pl.BlockSpec((rows, q.shape[1]), lambda p: (p, 0)),
                   pl.BlockSpec((rows, GDN_W), lambda n: (n, 0)),
                   pl.BlockSpec((rows, GDN_W), lambda n: (bwd(n), 0))],
        out_shape=[jax.ShapeDtypeStruct(q.shape, BF16),
                   jax.ShapeDtypeStruct((la, GDN_W), BF16), jax.ShapeDtypeStruct((la, GDN_W), BF16)],
        scratch_shapes=[pltpu.VMEM((2 * GDN_HEADS, GDN_HEAD_DIM, GDN_HEAD_DIM), F32)],
        compiler_params=_params(("arbitrary",)),
        name="attn_scan",
    )(sink, q, k, k, k, v, v, v, k, v, u, u, w, w, qd, qd, a2, a2, gt, gt)


def _merge_ffn_kernel(x_ref, attn_ref, of_ref, ob_ref, z_ref, pprev_ref, pcur_ref, pnext_ref, gate_ref,
                      modm_ref, gnorm_ref, poolw_ref, pscale_ref, wba_ref, wbg_ref, wbp_ref, wout_ref,
                      modf_ref, gainf_ref, win_ref, wo2_ref, fgain_ref, out_ref, ps_ref, xn_ref,
                      *, tm, ctx_len, total, ntiles, d_ff, fc, final):
    i = pl.program_id(0)
    d = x_ref.shape[1]
    slot = i % 2

    @pl.when(i == 0)
    def _():
        xn_ref[1] = jnp.zeros(xn_ref.shape[1:], F32)

    row0 = jnp.minimum(i, ntiles - 1) * tm
    first = jnp.logical_or(row0 == 0, row0 == ctx_len)
    last = jnp.logical_or(row0 + tm == ctx_len, row0 + tm == total)
    ps_ref[0:HALO, :] = jnp.where(first, 0.0, pprev_ref[...])
    ps_ref[HALO:HALO + tm, :] = pcur_ref[...]
    ps_ref[HALO + tm:HALO + tm + HALO, :] = jnp.where(last, 0.0, pnext_ref[...])
    rows = _row_ids(row0, tm)
    is_ctx = rows < ctx_len
    seg_pos = jnp.where(is_ctx, rows, rows - ctx_len)
    seg_len = jnp.where(is_ctx, ctx_len, total - ctx_len)
    m = {}

    def merge_vector_prep():
        m["pool_r"] = []
        for gi, win in enumerate(POOL_SIZES):
            ls = slice(gi * POOL_GROUP, (gi + 1) * POOL_GROUP)
            s = ps_ref[pl.ds(HALO - win // 2, tm), ls]
            for o in range(1, win):
                s = s + ps_ref[pl.ds(HALO - win // 2 + o, tm), ls]
            cnt = (jnp.minimum(seg_pos + win // 2, seg_len) - jnp.maximum(seg_pos - win // 2, 0)).astype(F32)
            m["pool_r"].append((s / cnt - pcur_ref[:, ls]).astype(BF16))
        gdn = []
        for h in range(GDN_HEADS):
            hs = slice(h * GDN_HEAD_DIM, (h + 1) * GDN_HEAD_DIM)
            o = of_ref[:, hs].astype(F32) + ob_ref[:, hs].astype(F32)
            o = o * lax.rsqrt(jnp.mean(o * o, axis=-1, keepdims=True) + EPS) * gnorm_ref[...]
            gdn.append(o * _silu(z_ref[:, hs].astype(F32)))
        m["gdn_o"] = jnp.concatenate(gdn, axis=1).astype(BF16)

    def merge_branches():
        pooled = [_dot(r, poolw_ref[gi]) * pscale_ref[:, gi * POOL_GROUP:(gi + 1) * POOL_GROUP]
                  for gi, r in enumerate(m["pool_r"])]
        pool_o = jnp.concatenate(pooled, axis=1).astype(BF16)
        m["merged"] = (gate_ref[:, 0:d].astype(F32) * _dot(attn_ref[...], wba_ref[...])
                       + gate_ref[:, d:2 * d].astype(F32) * _dot(m["gdn_o"], wbg_ref[...])
                       + gate_ref[:, 2 * d:3 * d].astype(F32) * _dot(pool_o, wbp_ref[...])).astype(BF16)

    def merge_out():
        y = _dot(m["merged"], wout_ref[...])
        xn_ref[slot] = x_ref[...] + _mod_gate(modm_ref, rows, ctx_len, d) * y

    xp = xn_ref[1 - slot]
    rows_p = _row_ids((i - 1) * tm, tm)
    hb = _norm_mod(xp, gainf_ref[...], modf_ref, rows_p, ctx_len).astype(BF16)
    acc = jnp.zeros(xp.shape, F32)
    n_chunks = d_ff // fc
    for j in range(n_chunks):
        gt = _dot(hb, win_ref[:, j * fc:(j + 1) * fc])
        up = _dot(hb, win_ref[:, d_ff + j * fc:d_ff + (j + 1) * fc])
        if j == 0:
            merge_vector_prep()
        acc = acc + _dot((_silu(gt) * up).astype(BF16), wo2_ref[j * fc:(j + 1) * fc, :])
    merge_branches()
    merge_out()
    y = xp + (0.5 * _mod_gate(modf_ref, rows_p, ctx_len, d)) * acc
    if final:
        y = y * lax.rsqrt(jnp.mean(y * y, axis=-1, keepdims=True) + EPS) * fgain_ref[...]
    out_ref[...] = y


def _merge_ffn_call(xa, attn_o, o_f, o_b, z, pin, gates, mods, gnorm, pool_w, pool_scale, wba, wbg, wbp, wout,
                    gainf, w_in, w_out, fgain, *, l, tm, ctx_len, final):
    la, d = xa.shape
    hb = tm // HALO
    nhalo = la // HALO
    pw = pin.shape[1]
    ntiles = la // tm
    d_ff = w_out.shape[1]
    fc = d_ff // 2
    assert d_ff // fc == 2
    kern = functools.partial(_merge_ffn_kernel, tm=tm, ctx_len=ctx_len, total=la, ntiles=ntiles,
                             d_ff=d_ff, fc=fc, final=final)
    cur = lambda i: jnp.minimum(i, ntiles - 1)
    row = lambda width: pl.BlockSpec((tm, width), lambda i: (cur(i), 0))
    if final:
        skip = ctx_len // tm
        out_spec = pl.BlockSpec((tm, d), lambda i: (jnp.maximum(i - 1 - skip, 0), 0))
        out_shape = jax.ShapeDtypeStruct((la - ctx_len, d), F32)
        aliases = {}
    else:
        out_spec = pl.BlockSpec((tm, d), lambda i: (jnp.maximum(i - 1, 0), 0))
        out_shape = jax.ShapeDtypeStruct((la, d), F32)
        aliases = {0: 0}
    return pl.pallas_call(
        kern,
        grid=(ntiles + 1,),
        in_specs=[row(d), row(attn_o.shape[1]), row(GDN_W), row(GDN_W), row(GDN_W),
                  pl.BlockSpec((HALO, pw), lambda i: (jnp.maximum(cur(i) * hb - 1, 0), 0)),
                  row(pw),
                  pl.BlockSpec((HALO, pw), lambda i: (jnp.minimum((cur(i) + 1) * hb, nhalo - 1), 0)),
                  row(gates.shape[1]),
                  _mod_spec(d, l, 1), _layer_spec(gnorm, l), _layer_spec(pool_w, l), _layer_spec(pool_scale, l),
                  _layer_spec(wba, l), _layer_spec(wbg, l), _layer_spec(wbp, l), _layer_spec(wout, l),
                  _mod_spec(d, l, 2), _layer_spec(gainf, l), _layer_spec(w_in, l), _layer_spec(w_out, l),
                  _const_spec((1, d))],
        out_specs=out_spec,
        out_shape=out_shape,
        input_output_aliases=aliases,
        scratch_shapes=[pltpu.VMEM((tm + 2 * HALO, pw), F32), pltpu.VMEM((2, tm, d), F32)],
        compiler_params=_params(("arbitrary",)),
        name="merge_ffn",
    )(xa, attn_o, o_f, o_b, z, pin, pin, pin, gates, mods, gnorm, pool_w, pool_scale, wba, wbg, wbp, wout,
      mods, gainf, w_in, w_out, fgain)


def _rope_tables(seq, ctx_len):
    quarter = ATTN_HEAD_DIM // 4
    rows = seq // GRID_W
    inv = ROPE_THETA ** (-jnp.arange(quarter, dtype=F32) / quarter)

    def axis_tables(n):
        ang = jnp.arange(n, dtype=F32)[:, None] * inv[None, :]
        c, s = jnp.cos(ang), jnp.sin(ang)
        return jnp.concatenate([c, c], axis=1), jnp.concatenate([-s, s], axis=1)

    cr, sr = axis_tables(rows)
    cc, sc = axis_tables(GRID_W)

    def full(tr, tc):
        t = jnp.concatenate([jnp.broadcast_to(tr[:, None, :], (rows, GRID_W, 2 * quarter)),
                             jnp.broadcast_to(tc[None, :, :], (rows, GRID_W, 2 * quarter))], axis=-1)
        t = t.reshape(seq, ATTN_HEAD_DIM)
        return jnp.concatenate([t] * (LANES // ATTN_HEAD_DIM), axis=1)

    cos = jnp.concatenate([jnp.ones((ctx_len, LANES), F32), full(cr, cc)], axis=0)
    sin = jnp.concatenate([jnp.zeros((ctx_len, LANES), F32), full(sr, sc)], axis=0)
    return cos, sin


def _pack_kernel(a_ref, b_ref, out_ref, *, short_tile, short_rows):
    t = pl.program_id(1)
    a = a_ref[...]
    shifted = jnp.concatenate([a[short_rows:], b_ref[0:short_rows]], axis=0)
    rows = lax.broadcasted_iota(jnp.int32, a.shape, 0)
    src = jnp.where(t < short_tile, a, jnp.where(t == short_tile, jnp.where(rows < short_rows, a, 0.0), shifted))
    out_ref[...] = jnp.transpose(src).astype(BF16)


def _pack_w_in(w_in):
    depth, d, n = w_in.shape
    offs, n_packed = _sec_offsets()
    (short_off, _), short_rows = offs["ab"], _SEC_SRC[[name for name, _ in _SEC].index("ab")]
    assert short_off % LANES == 0 and sum(w != s for (_, w), s in zip(_SEC, _SEC_SRC)) == 1
    short_tile = short_off // LANES
    last = (n - 1) // LANES
    blk_a = lambda l, t: (l, jnp.where(t <= short_tile, t, t - 1), 0)
    blk_b = lambda l, t: (l, jnp.minimum(jnp.where(t <= short_tile, t, t - 1) + 1, last), 0)
    kern = functools.partial(_pack_kernel, short_tile=short_tile, short_rows=short_rows)
    wt = jnp.swapaxes(w_in, 1, 2)
    return pl.pallas_call(
        kern,
        grid=(depth, n_packed // LANES),
        in_specs=[pl.BlockSpec((None, LANES, d), blk_a), pl.BlockSpec((None, LANES, d), blk_b)],
        out_specs=pl.BlockSpec((None, d, LANES), lambda l, t: (l, 0, t)),
        out_shape=jax.ShapeDtypeStruct((depth, d, n_packed), BF16),
        compiler_params=_params(("parallel", "parallel")),
        name="pack_w_in",
    )(wt, wt)


def kernel(x, c, ctx, c_ctx, w_ada, b_ada, norm_ffn1, w_ffn1_in, w_ffn1_out, norm_mix, w_in, attn_sink,
           gdn_conv, gdn_a_log, gdn_dt_bias, gdn_norm, pool_w, pool_scale, w_branch_attn, w_branch_gdn,
           w_branch_pool, w_out, norm_ffn2, w_ffn2_in, w_ffn2_out, final_norm):
    batch, seq, d = x.shape
    assert batch == 1, "single-sequence kernel"
    ctx_len = ctx.shape[1]
    depth = w_ada.shape[0]
    tm = ctx_len
    assert tm % (SCAN_CHUNKS * GDN_CHUNK) == 0 and seq % tm == 0 and seq % GRID_W == 0

    c_cols = jnp.zeros((d, LANES), F32).at[:, 0].set(c[0]).at[:, 1].set(c_ctx)
    mods = _ada_call(c_cols, w_ada, b_ada)
    cos, sin = _rope_tables(seq, ctx_len)

    bf = lambda w: w.astype(BF16)
    unit = lambda p: p[:, None, :]
    w1i, w1o, w2i, w2o = bf(w_ffn1_in), bf(w_ffn1_out), bf(w_ffn2_in), bf(w_ffn2_out)
    wproj = _pack_w_in(w_in)
    wba, wbg, wbp, wo, pw = bf(w_branch_attn), bf(w_branch_gdn), bf(w_branch_pool), bf(w_out), bf(pool_w)
    n1, nm, n2, gn, psc = unit(norm_ffn1), unit(norm_mix), unit(norm_ffn2), unit(gdn_norm), unit(pool_scale)
    abp = jnp.stack([gdn_dt_bias.reshape(depth, -1), gdn_a_log.reshape(depth, -1)], axis=1)
    abp = jnp.pad(abp, ((0, 0), (0, 6), (0, LANES - 2 * GDN_HEADS)))

    for l in range(depth):
        xa, q, k, v, gqkv, z, ab, pin, gates = _ffn_proj_call(
            x[0] if l == 0 else xa, ctx[0] if l == 0 else None, mods, n1, w1i, w1o, nm, wproj, cos, sin,
            l=l, tm=tm, ctx_len=ctx_len)
        u, w, qd, a2, gt = _gdn_prep_call(gqkv, ab, gdn_conv, abp, l=l, tg=tm, ctx_len=ctx_len)
        attn_o, o_f, o_b = _attn_scan_call(attn_sink, q, k, v, u, w, qd, a2, gt, l=l, ctx_len=ctx_len)
        xa = _merge_ffn_call(xa, attn_o, o_f, o_b, z, pin, gates, mods, gn, pw, psc, wba, wbg, wbp, wo,
                             n2, w2i, w2o, final_norm[None], l=l, tm=tm, ctx_len=ctx_len, final=l == depth - 1)
    return xa[None]
```

```python
import functools

import jax
import jax.numpy as jnp
from jax import lax
from jax.experimental import pallas as pl
from jax.experimental.pallas import tpu as pltpu

F32 = jnp.float32
BF16 = jnp.bfloat16
HIGHEST = lax.Precision.HIGHEST

GRID_W = 64
ATTN_HEADS = 8
ATTN_KV_HEADS = 2
ATTN_HEAD_DIM = 64
ATTN_BLOCK = 128
ATTN_STEP_BLOCKS = 2
ROPE_THETA = 10000.0
GDN_HEADS = 4
GDN_HEAD_DIM = 128
GDN_W = GDN_HEADS * GDN_HEAD_DIM
GDN_CONV = 5
GDN_CHUNK = 128
GDN_BASE = 16
SCAN_CHUNKS = 2
POOL_SIZES = (2, 4, 8, 16)
POOL_GROUP = 128
POOL_WIDTH = POOL_GROUP * len(POOL_SIZES)
N_MOD = 9
N_BRANCH = 3
EPS = 1e-6

LANES = 128
HALO = 8
VMEM_LIMIT = 56 * 1024 * 1024
COL_CHUNK = 512


def _dot(a, b, precision=None):
    return jnp.dot(a, b, preferred_element_type=F32, precision=precision)


def _dot_nt(a, b):
    return lax.dot_general(a, b, (((1,), (1,)), ((), ())), preferred_element_type=F32)


def _silu(x):
    return x * jax.nn.sigmoid(x)


def _params(sem):
    return pltpu.CompilerParams(dimension_semantics=sem, vmem_limit_bytes=VMEM_LIMIT)


def _const_spec(shape):
    nd = len(shape)
    return pl.BlockSpec(shape, lambda *_: (0,) * nd, pipeline_mode=pl.Buffered(1))


def _layer_spec(arr, l):
    nd = arr.ndim - 1
    return pl.BlockSpec((None,) + arr.shape[1:], lambda *_: (l,) + (0,) * nd, pipeline_mode=pl.Buffered(1))


def _mod_spec(d, l, j):
    return pl.BlockSpec((None, 8, 3 * d), lambda *_: (l, 0, j), pipeline_mode=pl.Buffered(1))


def _norm_mod(x, gain, mod_ref, rows, ctx_len):
    d = x.shape[1]
    y = x * lax.rsqrt(jnp.mean(x * x, axis=-1, keepdims=True) + EPS) * gain
    is_ctx = rows < ctx_len
    shift = jnp.where(is_ctx, mod_ref[1:2, 0:d], mod_ref[0:1, 0:d])
    scale = jnp.where(is_ctx, mod_ref[1:2, d:2 * d], mod_ref[0:1, d:2 * d])
    return y * (1.0 + scale) + shift


def _mod_gate(mod_ref, rows, ctx_len, d):
    return jnp.where(rows < ctx_len, mod_ref[1:2, 2 * d:3 * d], mod_ref[0:1, 2 * d:3 * d])


def _row_ids(row0, n):
    return row0 + lax.broadcasted_iota(jnp.int32, (n, 1), 0)


def _ada_kernel(c_ref, w_ref, b_ref, out_ref):
    w = w_ref[...]
    rows = [jnp.sum(w * _silu(c_ref[:, r:r + 1]), axis=0, keepdims=True) for r in range(2)]
    out_ref[...] = jnp.concatenate(rows + [jnp.zeros((6, w.shape[1]), F32)], axis=0) + b_ref[...]


def _ada_call(c_cols, w_ada, b_ada):
    depth, d, n = w_ada.shape
    tn = n // 8
    return pl.pallas_call(
        _ada_kernel,
        grid=(depth, n // tn),
        in_specs=[pl.BlockSpec((d, LANES), lambda l, j: (0, 0)),
                  pl.BlockSpec((None, d, tn), lambda l, j: (l, 0, j)),
                  pl.BlockSpec((None, 1, tn), lambda l, j: (l, 0, j))],
        out_specs=pl.BlockSpec((None, 8, tn), lambda l, j: (l, 0, j)),
        out_shape=jax.ShapeDtypeStruct((depth, 8, n), F32),
        compiler_params=_params(("parallel", "parallel")),
        name="ada_mod",
    )(c_cols, w_ada, b_ada.reshape(depth, 1, n))


def _swiglu_half_step(x, rows, mod_ref, gain_ref, win_ref, wout_ref, *, ctx_len, d_ff, fc):
    hb = _norm_mod(x, gain_ref[...], mod_ref, rows, ctx_len).astype(BF16)
    acc = jnp.zeros(x.shape, F32)
    for j in range(d_ff // fc):
        gt = _dot(hb, win_ref[:, j * fc:(j + 1) * fc])
        up = _dot(hb, win_ref[:, d_ff + j * fc:d_ff + (j + 1) * fc])
        acc = acc + _dot((_silu(gt) * up).astype(BF16), wout_ref[j * fc:(j + 1) * fc, :])
    return x + (0.5 * _mod_gate(mod_ref, rows, ctx_len, x.shape[1])) * acc


_SEC = (("q", 512), ("k", 128), ("v", 128), ("gqkv", 3 * GDN_W), ("z", GDN_W), ("ab", LANES),
        ("pin", POOL_WIDTH), ("gl", N_BRANCH * 1024))
_SEC_SRC = (512, 128, 128, 3 * GDN_W, GDN_W, 4 * GDN_HEADS, POOL_WIDTH, N_BRANCH * 1024)


def _sec_offsets():
    offs, o = {}, 0
    for name, width in _SEC:
        offs[name] = (o, width)
        o += width
    return offs, o


def _rope(t, cos, sin):
    n = t.shape[-1]
    lane = lax.broadcasted_iota(jnp.int32, t.shape, 1)
    swapped = jnp.where(lane % 32 < 16, pltpu.roll(t, n - 16, 1), pltpu.roll(t, 16, 1))
    reps = n // LANES
    c = jnp.concatenate([cos] * reps, axis=1) if reps > 1 else cos
    s = jnp.concatenate([sin] * reps, axis=1) if reps > 1 else sin
    return t * c + swapped * s


def _dup_halves(t):
    lo = lax.broadcasted_iota(jnp.int32, t.shape, 1) < LANES // 2
    r = pltpu.roll(t, LANES // 2, 1)
    return jnp.concatenate([jnp.where(lo, t, r), jnp.where(lo, r, t)], axis=1)


def _ffn_proj_kernel(*refs, tm, ctx_len, d_ff, fc, first_layer):
    if first_layer:
        ctx_ref, refs = refs[0], refs[1:]
    (x_ref, modf_ref, gainf_ref, win_ref, wout_ref, modp_ref, gainp_ref, w_ref, cos_ref, sin_ref,
     xo_ref, q_ref, k_ref, v_ref, gqkv_ref, z_ref, ab_ref, pin_ref, gate_ref) = refs
    offs, _ = _sec_offsets()
    i = pl.program_id(0)
    rows = _row_ids(i * tm, tm)
    x = x_ref[...]
    if first_layer:
        x = jnp.where(i == 0, ctx_ref[...], x)
    x = _swiglu_half_step(x, rows, modf_ref, gainf_ref, win_ref, wout_ref, ctx_len=ctx_len, d_ff=d_ff, fc=fc)
    xo_ref[...] = x
    hb = _norm_mod(x, gainp_ref[...], modp_ref, rows, ctx_len).astype(BF16)

    def sec(name, out_ref, fn):
        o, width = offs[name]
        step = min(width, COL_CHUNK)
        for a in range(0, width, step):
            out_ref[:, a:a + step] = fn(_dot(hb, w_ref[:, o + a:o + a + step])).astype(out_ref.dtype)

    cos, sin = cos_ref[...], sin_ref[...]
    sec("q", q_ref, lambda t: _rope(t, cos, sin) * (ATTN_HEAD_DIM ** -0.5))
    o, width = offs["k"]
    k_ref[...] = _dup_halves(_rope(_dot(hb, w_ref[:, o:o + width]), cos, sin)).astype(BF16)
    o, width = offs["v"]
    v_ref[...] = _dup_halves(_dot(hb, w_ref[:, o:o + width])).astype(BF16)
    ident = lambda t: t
    sec("gqkv", gqkv_ref, ident)
    sec("z", z_ref, ident)
    sec("ab", ab_ref, ident)
    sec("pin", pin_ref, ident)
    sec("gl", gate_ref, jax.nn.sigmoid)


def _ffn_proj_call(x_in, ctx, mods, gainf, w_in, w_out, gainp, w, cos, sin, *, l, tm, ctx_len):
    first_layer = ctx is not None
    d = x_in.shape[1]
    la = x_in.shape[0] + (ctx_len if first_layer else 0)
    offs, _ = _sec_offsets()
    d_ff = w_out.shape[1]
    fc = d_ff // 2
    kern = functools.partial(_ffn_proj_kernel, tm=tm, ctx_len=ctx_len, d_ff=d_ff, fc=fc, first_layer=first_layer)
    row = lambda width: pl.BlockSpec((tm, width), lambda i: (i, 0))
    if first_layer:
        assert tm == ctx_len
        lead_specs = [_const_spec((tm, d)), pl.BlockSpec((tm, d), lambda i: (jnp.maximum(i - 1, 0), 0))]
        lead_args = (ctx, x_in)
        aliases = {}
    else:
        lead_specs = [row(d)]
        lead_args = (x_in,)
        aliases = {0: 0}
    outs = ((d, F32), (512, BF16), (2 * LANES, BF16), (2 * LANES, BF16), (3 * GDN_W, F32), (GDN_W, BF16),
            (LANES, F32), (POOL_WIDTH, F32), (offs["gl"][1], BF16))
    return pl.pallas_call(
        kern,
        grid=(la // tm,),
        in_specs=lead_specs + [_mod_spec(d, l, 0), _layer_spec(gainf, l), _layer_spec(w_in, l),
                               _layer_spec(w_out, l), _mod_spec(d, l, 1), _layer_spec(gainp, l),
                               _layer_spec(w, l), row(LANES), row(LANES)],
        out_specs=[row(width) for width, _ in outs],
        out_shape=[jax.ShapeDtypeStruct((la, width), dt) for width, dt in outs],
        input_output_aliases=aliases,
        compiler_params=_params(("parallel",)),
        name="ffn_proj",
    )(*lead_args, mods, gainf, w_in, w_out, mods, gainp, w, cos, sin)


def _attn_stages(sink_ref, q_ref, kp_ref, kc_ref, kn_ref, vp_ref, vc_ref, vn_ref, kx_ref, vx_ref,
                 o_ref, *, l, ctx_blocks, n_blocks):
    b0 = ATTN_STEP_BLOCKS * pl.program_id(0)
    blk = ATTN_BLOCK
    is_lat = b0 >= ctx_blocks
    ri = lax.broadcasted_iota(jnp.int32, (blk, blk), 0)
    ci = lax.broadcasted_iota(jnp.int32, (blk, blk), 1)
    ninf = -jnp.inf
    lo = lax.broadcasted_iota(jnp.int32, (blk, LANES), 1) < LANES // 2
    pairs = ATTN_HEADS // ATTN_KV_HEADS // 2
    groups = [(sb, hk) for sb in range(ATTN_STEP_BLOCKS) for hk in range(ATTN_KV_HEADS)]

    def window(ref_p, ref_c, ref_n, sb, ks):
        blocks = [ref_p[:, ks]] + [ref_c[j * blk:(j + 1) * blk, ks] for j in range(ATTN_STEP_BLOCKS)] + [ref_n[:, ks]]
        return jnp.concatenate(blocks[sb:sb + 3], axis=0)

    bias = []
    for sb in range(ATTN_STEP_BLOCKS):
        prev_ok = b0 + sb - 1 >= ctx_blocks
        next_ok = jnp.logical_and(is_lat, b0 + sb + 1 < n_blocks)
        bias.append(jnp.concatenate(
            [jnp.where(jnp.logical_and(ci >= ri, prev_ok), 0.0, ninf),
             jnp.where(jnp.broadcast_to(is_lat, (blk, blk)), 0.0, ninf),
             jnp.where(jnp.logical_and(ci <= ri, next_ok), 0.0, ninf)], axis=1))

    scores, values, sinks = [], [], []
    for sb, hk in groups:
        ks = slice(hk * LANES, (hk + 1) * LANES)
        rows = slice(sb * blk, (sb + 1) * blk)
        q_parts, sk = [], []
        for j in range(pairs):
            col = (hk * pairs + j) * LANES
            qp = q_ref[rows, col:col + LANES]
            q_parts += [jnp.where(lo, qp, jnp.zeros_like(qp)), jnp.where(lo, jnp.zeros_like(qp), qp)]
            sk += [sink_ref[l, (hk * pairs + j) * 2], sink_ref[l, (hk * pairs + j) * 2 + 1]]
        q4 = jnp.concatenate(q_parts, axis=0)
        scores.append((_dot_nt(q4, window(kp_ref, kc_ref, kn_ref, sb, ks)), _dot_nt(q4, kx_ref[:, ks])))
        values.append((window(vp_ref, vc_ref, vn_ref, sb, ks), vx_ref[:, ks]))
        sinks.append(sk)
        yield
    probs = []
    for g, (sb, hk) in enumerate(groups):
        s_win, s_ctx = scores[g]
        p_win, p_ctx, inv = [], [], []
        for j, sk in enumerate(sinks[g]):
            rs = slice(j * blk, (j + 1) * blk)
            sw = s_win[rs] + bias[sb]
            sx = s_ctx[rs]
            m = jnp.maximum(jnp.maximum(jnp.max(sw, axis=-1, keepdims=True),
                                        jnp.max(sx, axis=-1, keepdims=True)), sk)
            pw = jnp.exp(sw - m)
            px = jnp.exp(sx - m)
            denom = jnp.sum(pw, axis=-1, keepdims=True) + jnp.sum(px, axis=-1, keepdims=True) + jnp.exp(sk - m)
            p_win.append(pw.astype(BF16))
            p_ctx.append(px.astype(BF16))
            inv.append(1.0 / denom)
        probs.append((jnp.concatenate(p_win, axis=0), jnp.concatenate(p_ctx, axis=0), inv))
    for g, (sb, hk) in enumerate(groups):
        p_win, p_ctx, inv = probs[g]
        vw, vx = values[g]
        r = _dot(p_win, vw) + _dot(p_ctx, vx)
        for j in range(pairs):
            r0 = r[(2 * j) * blk:(2 * j + 1) * blk] * inv[2 * j]
            r1 = r[(2 * j + 1) * blk:(2 * j + 2) * blk] * inv[2 * j + 1]
            col = (hk * pairs + j) * LANES
            o_ref[sb * blk:(sb + 1) * blk, col:col + LANES] = jnp.where(lo, r0, r1).astype(BF16)
        yield


def _tri_inverse_masks(n):
    ri = lax.broadcasted_iota(jnp.int32, (n, n), 0)
    ci = lax.broadcasted_iota(jnp.int32, (n, n), 1)
    eye = (ri == ci).astype(F32)
    same = lambda s: (ri // s) == (ci // s)
    offs, s = [], GDN_BASE
    while s < n:
        offs.append(jnp.logical_and(same(2 * s), jnp.logical_not(same(s))))
        s *= 2
    return eye, same(GDN_BASE), offs


def _tri_inverse(a_list, masks, between=None):
    eye, diag, offs = masks
    a0 = [jnp.where(diag, a, 0.0) for a in a_list]
    a0b = [x.astype(BF16) for x in a0]
    t = [eye - x for x in a0]
    p = [_dot(x, x) for x in a0b]
    s = 2
    while True:
        pb = [x.astype(BF16) for x in p]
        t = [ti + _dot(ti.astype(BF16), pi) for ti, pi in zip(t, pb)]
        s *= 2
        if s >= GDN_BASE:
            break
        p = [_dot(x, x) for x in pb]
    if between is not None:
        between()
    for off in offs:
        tb = [x.astype(BF16) for x in t]
        mid = [_dot(ti, jnp.where(off, a, 0.0).astype(BF16)).astype(BF16) for ti, a in zip(tb, a_list)]
        t = [ti - _dot(mi, tbi) for ti, mi, tbi in zip(t, mid, tb)]
    return t


def _gdn_prep_kernel(prev_ref, cur_ref, next_ref, ab_ref, convw_ref, abp_ref,
                     u_ref, w_ref, qd_ref, a2_ref, gt_ref, xs_ref, a_s, rhs_s, *, tg, ctx_len, total, ntiles):
    i = pl.program_id(0)
    dk = GDN_HEAD_DIM
    c_len = GDN_CHUNK
    heads = range(GDN_HEADS)
    chunks = range(tg // c_len)
    units = [(c, h, d) for c in chunks for h in heads for d in range(2)]
    hsl = [slice(h * dk, (h + 1) * dk) for h in heads]

    @pl.when(i == 0)
    def _():
        a_s[1] = jnp.zeros(a_s.shape[1:], F32)
        rhs_s[1] = jnp.zeros(rhs_s.shape[1:], BF16)

    slot = i % 2
    row0 = jnp.minimum(i, ntiles - 1) * tg
    first = jnp.logical_or(row0 == 0, row0 == ctx_len)
    last = jnp.logical_or(row0 + tg == ctx_len, row0 + tg == total)
    xs_ref[0:HALO, :] = jnp.where(first, 0.0, prev_ref[...])
    xs_ref[HALO:HALO + tg, :] = cur_ref[...]
    xs_ref[HALO + tg:HALO + tg + HALO, :] = jnp.where(last, 0.0, next_ref[...])
    pad = (GDN_CONV - 1) // 2

    def conv_silu(col, r0):
        ls = slice(col * dk, (col + 1) * dk)
        acc = xs_ref[pl.ds(HALO - pad + r0, c_len), ls] * convw_ref[0:1, ls]
        for j in range(1, GDN_CONV):
            acc = acc + xs_ref[pl.ds(HALO - pad + j + r0, c_len), ls] * convw_ref[j:j + 1, ls]
        return _silu(acc)

    def l2n(t):
        return t * lax.rsqrt(jnp.sum(t * t, axis=-1, keepdims=True) + EPS)

    ri = lax.broadcasted_iota(jnp.int32, (c_len, c_len), 0)
    ci = lax.broadcasted_iota(jnp.int32, (c_len, c_len), 1)
    strict = (ci < ri, ci > ri)
    incl = (ci <= ri, ci >= ri)
    tri = (incl[0].astype(F32), incl[1].astype(F32))
    last_row = (c_len - 1, 0)
    masks = _tri_inverse_masks(c_len)
    f = {}

    def front_matmuls():
        g_all = [-jnp.exp(abp_ref[1:2, :]) * jax.nn.softplus(ab_ref[c * c_len:(c + 1) * c_len, :] + abp_ref[0:1, :])
                 for c in chunks]
        f["b_all"] = [jax.nn.sigmoid(ab_ref[c * c_len:(c + 1) * c_len, :]) for c in chunks]
        f["gcs"] = [[_dot(tri[d], g_all[c], HIGHEST) for d in range(2)] for c in chunks]
        f["gct"] = [[jnp.transpose(x) for x in f["gcs"][c]] for c in chunks]
        pairs = [(c, h) for c in chunks for h in heads]
        f["qs"] = {ch: l2n(conv_silu(ch[1], ch[0] * c_len)) * (dk ** -0.5) for ch in pairs}
        f["ks"] = {ch: l2n(conv_silu(GDN_HEADS + ch[1], ch[0] * c_len)) for ch in pairs}
        f["vs"] = {ch: conv_silu(2 * GDN_HEADS + ch[1], ch[0] * c_len) for ch in pairs}
        f["kts"] = {ch: jnp.transpose(f["ks"][ch]) for ch in pairs}
        ktb = {ch: f["kts"][ch].astype(BF16) for ch in pairs}
        f["kk"] = {ch: _dot(f["ks"][ch].astype(BF16), ktb[ch]) for ch in pairs}
        f["qkt"] = {ch: _dot(f["qs"][ch].astype(BF16), ktb[ch]) for ch in pairs}

    a_prev = [a_s[1 - slot, n] for n in range(len(units))]
    t_prev = _tri_inverse(a_prev, masks, between=front_matmuls)
    uw = [_dot(t_.astype(BF16), rhs_s[1 - slot, n]) for n, t_ in enumerate(t_prev)]
    for n, (c, h, d) in enumerate(units):
        rs = slice(c * c_len, (c + 1) * c_len)
        u_ref[d, rs, hsl[h]] = uw[n][:, 0:dk].astype(BF16)
        w_ref[d, rs, hsl[h]] = uw[n][:, dk:2 * dk].astype(BF16)

    for n, (c, h, d) in enumerate(units):
        rs = slice(c * c_len, (c + 1) * c_len)
        lane = d * GDN_HEADS + h
        gc = f["gcs"][c][d][:, lane:lane + 1]
        gr = f["gct"][c][d][lane:lane + 1, :]
        beta = f["b_all"][c][:, 2 * GDN_HEADS + lane:2 * GDN_HEADS + lane + 1]
        e = jnp.exp(jnp.where(incl[d], gc - gr, -jnp.inf))
        egc = jnp.exp(gc)
        g_last = gc[last_row[d]:last_row[d] + 1, :]
        a_s[slot, n] = jnp.where(strict[d], beta * f["kk"][(c, h)] * e, 0.0)
        rhs_s[slot, n, :, 0:dk] = (beta * f["vs"][(c, h)]).astype(BF16)
        rhs_s[slot, n, :, dk:2 * dk] = (beta * egc * f["ks"][(c, h)]).astype(BF16)
        qd_ref[d, rs, hsl[h]] = (f["qs"][(c, h)] * egc).astype(BF16)
        a2_ref[d, c, 0:c_len, hsl[h]] = (f["qkt"][(c, h)] * e).astype(BF16)
        a2_ref[d, c, c_len:2 * c_len, hsl[h]] = (f["kts"][(c, h)] * jnp.exp(g_last - gr)).astype(BF16)
        gt_ref[d, c, h:h + 1, :] = jnp.broadcast_to(jnp.exp(g_last), (1, LANES))


def _gdn_prep_call(gqkv, ab, conv_w, abp, *, l, tg, ctx_len):
    la, wq = gqkv.shape
    c_len = GDN_CHUNK
    nch = la // c_len
    cpt = tg // c_len
    hb = tg // HALO
    nhalo = la // HALO
    ntiles = la // tg
    n_units = cpt * GDN_HEADS * 2
    kern = functools.partial(_gdn_prep_kernel, tg=tg, ctx_len=ctx_len, total=la, ntiles=ntiles)
    cur = lambda i: jnp.minimum(i, ntiles - 1)
    prv = lambda i: jnp.maximum(i - 1, 0)
    rows_cur = pl.BlockSpec((2, tg, GDN_W), lambda i: (0, cur(i), 0))
    rows_prv = pl.BlockSpec((2, tg, GDN_W), lambda i: (0, prv(i), 0))
    return pl.pallas_call(
        kern,
        grid=(ntiles + 1,),
        in_specs=[pl.BlockSpec((HALO, wq), lambda i: (jnp.maximum(cur(i) * hb - 1, 0), 0)),
                  pl.BlockSpec((tg, wq), lambda i: (cur(i), 0)),
                  pl.BlockSpec((HALO, wq), lambda i: (jnp.minimum((cur(i) + 1) * hb, nhalo - 1), 0)),
                  pl.BlockSpec((tg, LANES), lambda i: (cur(i), 0)),
                  _layer_spec(conv_w, l), _layer_spec(abp, l)],
        out_specs=[rows_prv, rows_prv, rows_cur,
                   pl.BlockSpec((2, cpt, 2 * c_len, GDN_W), lambda i: (0, cur(i), 0, 0)),
                   pl.BlockSpec((2, cpt, GDN_HEADS, LANES), lambda i: (0, cur(i), 0, 0))],
        out_shape=[jax.ShapeDtypeStruct((2, la, GDN_W), BF16),
                   jax.ShapeDtypeStruct((2, la, GDN_W), BF16),
                   jax.ShapeDtypeStruct((2, la, GDN_W), BF16),
                   jax.ShapeDtypeStruct((2, nch, 2 * c_len, GDN_W), BF16),
                   jax.ShapeDtypeStruct((2, nch, GDN_HEADS, LANES), F32)],
        scratch_shapes=[pltpu.VMEM((tg + 2 * HALO, wq), F32),
                        pltpu.VMEM((2, n_units, c_len, c_len), F32),
                        pltpu.VMEM((2, n_units, c_len, 2 * GDN_HEAD_DIM), BF16)],
        compiler_params=_params(("arbitrary",)),
        name="gdn_prep",
    )(gqkv, gqkv, gqkv, ab, conv_w, abp)


def _scan_stages(uf, ub, wf, wb, qdf, qdb, a2f, a2b, gtf, gtb, of_ref, ob_ref, s_ref):
    dk = GDN_HEAD_DIM
    c_len = GDN_CHUNK
    dirs = ((uf, wf, qdf, a2f, gtf, of_ref), (ub, wb, qdb, a2b, gtb, ob_ref))
    chains = [(d, h) for d in range(2) for h in range(GDN_HEADS)]
    hs = [slice(h * dk, (h + 1) * dk) for h in range(GDN_HEADS)]
    s = [s_ref[d * GDN_HEADS + h] for d, h in chains]
    for step in range(SCAN_CHUNKS):
        sub = (step, SCAN_CHUNKS - 1 - step)
        rows = [slice(sub[d] * c_len, (sub[d] + 1) * c_len) for d in range(2)]
        r1 = [_dot(jnp.concatenate([dirs[d][1][rows[d], hs[h]], dirs[d][2][rows[d], hs[h]]], axis=0),
                   s_.astype(BF16)) for (d, h), s_ in zip(chains, s)]
        yield
        vb = [(dirs[d][0][rows[d], hs[h]].astype(F32) - r[0:c_len]).astype(BF16) for (d, h), r in zip(chains, r1)]
        r2 = [_dot(dirs[d][3][sub[d], :, hs[h]], v_) for (d, h), v_ in zip(chains, vb)]
        yield
        for (d, h), ra, rb in zip(chains, r1, r2):
            dirs[d][5][rows[d], hs[h]] = (ra[c_len:2 * c_len] + rb[0:c_len]).astype(BF16)
        s = [s_ * dirs[d][4][sub[d], h:h + 1, :] + rb[c_len:2 * c_len] for (d, h), s_, rb in zip(chains, s, r2)]
    for (d, h), s_ in zip(chains, s):
        s_ref[d * GDN_HEADS + h] = s_


def _attn_scan_kernel(*refs, l, ctx_blocks, n_blocks):
    attn_in, scan_in = refs[0:10], refs[10:20]
    o_attn, of_ref, ob_ref, s_ref = refs[20:24]

    @pl.when(pl.program_id(0) == 0)
    def _():
        s_ref[...] = jnp.zeros(s_ref.shape, F32)

    stages = [_scan_stages(*scan_in, of_ref, ob_ref, s_ref),
              _attn_stages(*attn_in, o_attn, l=l, ctx_blocks=ctx_blocks, n_blocks=n_blocks)]
    while stages:
        for g in list(stages):
            try:
                next(g)
            except StopIteration:
                stages.remove(g)


def _attn_scan_call(sink, q, k, v, u, w, qd, a2, gt, *, l, ctx_len):
    la = q.shape[0]
    blk = ATTN_BLOCK
    step = ATTN_STEP_BLOCKS
    nb = la // blk
    cb = ctx_len // blk
    rows = step * blk
    assert rows == SCAN_CHUNKS * GDN_CHUNK and la % rows == 0 and ctx_len % rows == 0
    nsteps = la // rows
    ncs = ctx_len // rows
    kern = functools.partial(_attn_scan_kernel, l=l, ctx_blocks=cb, n_blocks=nb)
    kvw = k.shape[1]
    prev = pl.BlockSpec((blk, kvw), lambda p: (jnp.maximum(step * p - 1, 0), 0))
    cur = pl.BlockSpec((rows, kvw), lambda p: (p, 0))
    nxt = pl.BlockSpec((blk, kvw), lambda p: (jnp.minimum(step * (p + 1), nb - 1), 0))
    cx = pl.BlockSpec((ctx_len, kvw), lambda p: (0, 0))

    def bwd(n):
        return jnp.where(n < ncs, ncs - 1 - n, nsteps + ncs - 1 - n)

    def both(make):
        return [make(0, lambda n: n), make(1, bwd)]

    r_specs = both(lambda d, f: pl.BlockSpec((None, rows, GDN_W), lambda n: (d, f(n), 0)))
    a_specs = both(lambda d, f: pl.BlockSpec((None, SCAN_CHUNKS, 2 * GDN_CHUNK, GDN_W), lambda n: (d, f(n), 0, 0)))
    g_specs = both(lambda d, f: pl.BlockSpec((None, SCAN_CHUNKS, GDN_HEADS, LANES), lambda n: (d, f(n), 0, 0)))
    return pl.pallas_call(
        kern,
        grid=(nsteps,),
        in_specs=[pl.BlockSpec(memory_space=pltpu.SMEM),
                  pl.BlockSpec((rows, q.shape[1]), lambda p: (p, 0)),
                  prev, cur, nxt, prev, cur, nxt, cx, cx] + r_specs + r_specs + r_specs + a_specs + g_specs,
        out_specs=[pl.BlockSpec((rows, q.shape[1]), lambda p: (p, 0)),
                   pl.BlockSpec((rows, GDN_W), lambda n: (n, 0)),
                   pl.BlockSpec((rows, GDN_W), lambda n: (bwd(n), 0))],
        out_shape=[jax.ShapeDtypeStruct(q.shape, BF16),
                   jax.ShapeDtypeStruct((la, GDN_W), BF16), jax.ShapeDtypeStruct((la, GDN_W), BF16)],
        scratch_shapes=[pltpu.VMEM((2 * GDN_HEADS, GDN_HEAD_DIM, GDN_HEAD_DIM), F32)],
        compiler_params=_params(("arbitrary",)),
        name="attn_scan",
    )(sink, q, k, k, k, v, v, v, k, v, u, u, w, w, qd, qd, a2, a2, gt, gt)


def _merge_ffn_kernel(x_ref, attn_ref, of_ref, ob_ref, z_ref, pprev_ref, pcur_ref, pnext_ref, gate_ref,
                      modm_ref, gnorm_ref, poolw_ref, pscale_ref, wba_ref, wbg_ref, wbp_ref, wout_ref,
                      modf_ref, gainf_ref, win_ref, wo2_ref, fgain_ref, out_ref, ps_ref, xn_ref,
                      *, tm, ctx_len, total, ntiles, d_ff, fc, final):
    i = pl.program_id(0)
    d = x_ref.shape[1]
    slot = i % 2

    @pl.when(i == 0)
    def _():
        xn_ref[1] = jnp.zeros(xn_ref.shape[1:], F32)

    row0 = jnp.minimum(i, ntiles - 1) * tm
    first = jnp.logical_or(row0 == 0, row0 == ctx_len)
    last = jnp.logical_or(row0 + tm == ctx_len, row0 + tm == total)
    ps_ref[0:HALO, :] = jnp.where(first, 0.0, pprev_ref[...])
    ps_ref[HALO:HALO + tm, :] = pcur_ref[...]
    ps_ref[HALO + tm:HALO + tm + HALO, :] = jnp.where(last, 0.0, pnext_ref[...])
    rows = _row_ids(row0, tm)
    is_ctx = rows < ctx_len
    seg_pos = jnp.where(is_ctx, rows, rows - ctx_len)
    seg_len = jnp.where(is_ctx, ctx_len, total - ctx_len)
    m = {}

    def merge_vector_prep():
        m["pool_r"] = []
        for gi, win in enumerate(POOL_SIZES):
            ls = slice(gi * POOL_GROUP, (gi + 1) * POOL_GROUP)
            s = ps_ref[pl.ds(HALO - win // 2, tm), ls]
            for o in range(1, win):
                s = s + ps_ref[pl.ds(HALO - win // 2 + o, tm), ls]
            cnt = (jnp.minimum(seg_pos + win // 2, seg_len) - jnp.maximum(seg_pos - win // 2, 0)).astype(F32)
            m["pool_r"].append((s / cnt - pcur_ref[:, ls]).astype(BF16))
        gdn = []
        for h in range(GDN_HEADS):
            hs = slice(h * GDN_HEAD_DIM, (h + 1) * GDN_HEAD_DIM)
            o = of_ref[:, hs].astype(F32) + ob_ref[:, hs].astype(F32)
            o = o * lax.rsqrt(jnp.mean(o * o, axis=-1, keepdims=True) + EPS) * gnorm_ref[...]
            gdn.append(o * _silu(z_ref[:, hs].astype(F32)))
        m["gdn_o"] = jnp.concatenate(gdn, axis=1).astype(BF16)

    def merge_branches():
        pooled = [_dot(r, poolw_ref[gi]) * pscale_ref[:, gi * POOL_GROUP:(gi + 1) * POOL_GROUP]
                  for gi, r in enumerate(m["pool_r"])]
        pool_o = jnp.concatenate(pooled, axis=1).astype(BF16)
        m["merged"] = (gate_ref[:, 0:d].astype(F32) * _dot(attn_ref[...], wba_ref[...])
                       + gate_ref[:, d:2 * d].astype(F32) * _dot(m["gdn_o"], wbg_ref[...])
                       + gate_ref[:, 2 * d:3 * d].astype(F32) * _dot(pool_o, wbp_ref[...])).astype(BF16)

    def merge_out():
        y = _dot(m["merged"], wout_ref[...])
        xn_ref[slot] = x_ref[...] + _mod_gate(modm_ref, rows, ctx_len, d) * y

    xp = xn_ref[1 - slot]
    rows_p = _row_ids((i - 1) * tm, tm)
    hb = _norm_mod(xp, gainf_ref[...], modf_ref, rows_p, ctx_len).astype(BF16)
    acc = jnp.zeros(xp.shape, F32)
    n_chunks = d_ff // fc
    for j in range(n_chunks):
        gt = _dot(hb, win_ref[:, j * fc:(j + 1) * fc])
        up = _dot(hb, win_ref[:, d_ff + j * fc:d_ff + (j + 1) * fc])
        if j == 0:
            merge_vector_prep()
        acc = acc + _dot((_silu(gt) * up).astype(BF16), wo2_ref[j * fc:(j + 1) * fc, :])
    merge_branches()
    merge_out()
    y = xp + (0.5 * _mod_gate(modf_ref, rows_p, ctx_len, d)) * acc
    if final:
        y = y * lax.rsqrt(jnp.mean(y * y, axis=-1, keepdims=True) + EPS) * fgain_ref[...]
    out_ref[...] = y


def _merge_ffn_call(xa, attn_o, o_f, o_b, z, pin, gates, mods, gnorm, pool_w, pool_scale, wba, wbg, wbp, wout,
                    gainf, w_in, w_out, fgain, *, l, tm, ctx_len, final):
    la, d = xa.shape
    hb = tm // HALO
    nhalo = la // HALO
    pw = pin.shape[1]
    ntiles = la // tm
    d_ff = w_out.shape[1]
    fc = d_ff // 2
    assert d_ff // fc == 2
    kern = functools.partial(_merge_ffn_kernel, tm=tm, ctx_len=ctx_len, total=la, ntiles=ntiles,
                             d_ff=d_ff, fc=fc, final=final)
    cur = lambda i: jnp.minimum(i, ntiles - 1)
    row = lambda width: pl.BlockSpec((tm, width), lambda i: (cur(i), 0))
    if final:
        skip = ctx_len // tm
        out_spec = pl.BlockSpec((tm, d), lambda i: (jnp.maximum(i - 1 - skip, 0), 0))
        out_shape = jax.ShapeDtypeStruct((la - ctx_len, d), F32)
        aliases = {}
    else:
        out_spec = pl.BlockSpec((tm, d), lambda i: (jnp.maximum(i - 1, 0), 0))
        out_shape = jax.ShapeDtypeStruct((la, d), F32)
        aliases = {0: 0}
    return pl.pallas_call(
        kern,
        grid=(ntiles + 1,),
        in_specs=[row(d), row(attn_o.shape[1]), row(GDN_W), row(GDN_W), row(GDN_W),
                  pl.BlockSpec((HALO, pw), lambda i: (jnp.maximum(cur(i) * hb - 1, 0), 0)),
                  row(pw),
                  pl.BlockSpec((HALO, pw), lambda i: (jnp.minimum((cur(i) + 1) * hb, nhalo - 1), 0)),
                  row(gates.shape[1]),
                  _mod_spec(d, l, 1), _layer_spec(gnorm, l), _layer_spec(pool_w, l), _layer_spec(pool_scale, l),
                  _layer_spec(wba, l), _layer_spec(wbg, l), _layer_spec(wbp, l), _layer_spec(wout, l),
                  _mod_spec(d, l, 2), _layer_spec(gainf, l), _layer_spec(w_in, l), _layer_spec(w_out, l),
                  _const_spec((1, d))],
        out_specs=out_spec,
        out_shape=out_shape,
        input_output_aliases=aliases,
        scratch_shapes=[pltpu.VMEM((tm + 2 * HALO, pw), F32), pltpu.VMEM((2, tm, d), F32)],
        compiler_params=_params(("arbitrary",)),
        name="merge_ffn",
    )(xa, attn_o, o_f, o_b, z, pin, pin, pin, gates, mods, gnorm, pool_w, pool_scale, wba, wbg, wbp, wout,
      mods, gainf, w_in, w_out, fgain)


def _rope_tables(seq, ctx_len):
    quarter = ATTN_HEAD_DIM // 4
    rows = seq // GRID_W
    inv = ROPE_THETA ** (-jnp.arange(quarter, dtype=F32) / quarter)

    def axis_tables(n):
        ang = jnp.arange(n, dtype=F32)[:, None] * inv[None, :]
        c, s = jnp.cos(ang), jnp.sin(ang)
        return jnp.concatenate([c, c], axis=1), jnp.concatenate([-s, s], axis=1)

    cr, sr = axis_tables(rows)
    cc, sc = axis_tables(GRID_W)

    def full(tr, tc):
        t = jnp.concatenate([jnp.broadcast_to(tr[:, None, :], (rows, GRID_W, 2 * quarter)),
                             jnp.broadcast_to(tc[None, :, :], (rows, GRID_W, 2 * quarter))], axis=-1)
        t = t.reshape(seq, ATTN_HEAD_DIM)
        return jnp.concatenate([t] * (LANES // ATTN_HEAD_DIM), axis=1)

    cos = jnp.concatenate([jnp.ones((ctx_len, LANES), F32), full(cr, cc)], axis=0)
    sin = jnp.concatenate([jnp.zeros((ctx_len, LANES), F32), full(sr, sc)], axis=0)
    return cos, sin


PACK_TILES = 3


def _pack_kernel(*refs, short_tile, short_rows):
    out_ref = refs[-1]
    window = jnp.concatenate([r[...] for r in refs[:-1]], axis=0)
    rows = lax.broadcasted_iota(jnp.int32, (LANES, window.shape[1]), 0)
    for s in range(PACK_TILES):
        t = PACK_TILES * pl.program_id(1) + s
        own = window[(s + 1) * LANES:(s + 2) * LANES]
        shifted = window[s * LANES + short_rows:(s + 1) * LANES + short_rows]
        src = jnp.where(t < short_tile, own, jnp.where(t == short_tile, jnp.where(rows < short_rows, own, 0.0), shifted))
        out_ref[:, s * LANES:(s + 1) * LANES] = jnp.transpose(src).astype(BF16)


def _pack_w_in(w_in):
    depth, d, n = w_in.shape
    offs, n_packed = _sec_offsets()
    (short_off, _), short_rows = offs["ab"], _SEC_SRC[[name for name, _ in _SEC].index("ab")]
    assert short_off % LANES == 0 and sum(w != s for (_, w), s in zip(_SEC, _SEC_SRC)) == 1
    assert n_packed % (PACK_TILES * LANES) == 0
    short_tile = short_off // LANES
    last = (n - 1) // LANES

    def src_block(k):
        return pl.BlockSpec((None, LANES, d), lambda l, j: (l, jnp.clip(PACK_TILES * j - 1 + k, 0, last), 0))

    kern = functools.partial(_pack_kernel, short_tile=short_tile, short_rows=short_rows)
    wt = jnp.swapaxes(w_in, 1, 2)
    n_src = PACK_TILES + 1
    return pl.pallas_call(
        kern,
        grid=(depth, n_packed // (PACK_TILES * LANES)),
        in_specs=[src_block(k) for k in range(n_src)],
        out_specs=pl.BlockSpec((None, d, PACK_TILES * LANES), lambda l, j: (l, 0, j)),
        out_shape=jax.ShapeDtypeStruct((depth, d, n_packed), BF16),
        compiler_params=_params(("parallel", "parallel")),
        name="pack_w_in",
    )(*([wt] * n_src))


def kernel(x, c, ctx, c_ctx, w_ada, b_ada, norm_ffn1, w_ffn1_in, w_ffn1_out, norm_mix, w_in, attn_sink,
           gdn_conv, gdn_a_log, gdn_dt_bias, gdn_norm, pool_w, pool_scale, w_branch_attn, w_branch_gdn,
           w_branch_pool, w_out, norm_ffn2, w_ffn2_in, w_ffn2_out, final_norm):
    batch, seq, d = x.shape
    assert batch == 1, "single-sequence kernel"
    ctx_len = ctx.shape[1]
    depth = w_ada.shape[0]
    tm = ctx_len
    assert tm % (SCAN_CHUNKS * GDN_CHUNK) == 0 and seq % tm == 0 and seq % GRID_W == 0

    c_cols = jnp.zeros((d, LANES), F32).at[:, 0].set(c[0]).at[:, 1].set(c_ctx)
    mods = _ada_call(c_cols, w_ada, b_ada)
    cos, sin = _rope_tables(seq, ctx_len)

    bf = lambda w: w.astype(BF16)
    unit = lambda p: p[:, None, :]
    w1i, w1o, w2i, w2o = bf(w_ffn1_in), bf(w_ffn1_out), bf(w_ffn2_in), bf(w_ffn2_out)
    wproj = _pack_w_in(w_in)
    wba, wbg, wbp, wo, pw = bf(w_branch_attn), bf(w_branch_gdn), bf(w_branch_pool), bf(w_out), bf(pool_w)
    n1, nm, n2, gn, psc = unit(norm_ffn1), unit(norm_mix), unit(norm_ffn2), unit(gdn_norm), unit(pool_scale)
    abp = jnp.stack([gdn_dt_bias.reshape(depth, -1), gdn_a_log.reshape(depth, -1)], axis=1)
    abp = jnp.pad(abp, ((0, 0), (0, 6), (0, LANES - 2 * GDN_HEADS)))

    for l in range(depth):
        xa, q, k, v, gqkv, z, ab, pin, gates = _ffn_proj_call(
            x[0] if l == 0 else xa, ctx[0] if l == 0 else None, mods, n1, w1i, w1o, nm, wproj, cos, sin,
            l=l, tm=tm, ctx_len=ctx_len)
        u, w, qd, a2, gt = _gdn_prep_call(gqkv, ab, gdn_conv, abp, l=l, tg=tm, ctx_len=ctx_len)
        attn_o, o_f, o_b = _attn_scan_call(attn_sink, q, k, v, u, w, qd, a2, gt, l=l, ctx_len=ctx_len)
        xa = _merge_ffn_call(xa, attn_o, o_f, o_b, z, pin, gates, mods, gn, pw, psc, wba, wbg, wbp, wo,
                             n2, w2i, w2o, final_norm[None], l=l, tm=tm, ctx_len=ctx_len, final=l == depth - 1)
    return xa[None]
```

```python
import functools

import jax
import jax.numpy as jnp
from jax import lax
from jax.experimental import pallas as pl
from jax.experimental.pallas import tpu as pltpu

F32 = jnp.float32
BF16 = jnp.bfloat16
HIGHEST = lax.Precision.HIGHEST

GRID_W = 64
ATTN_HEADS = 8
ATTN_KV_HEADS = 2
ATTN_HEAD_DIM = 64
ATTN_BLOCK = 128
ATTN_STEP_BLOCKS = 2
ROPE_THETA = 10000.0
GDN_HEADS = 4
GDN_HEAD_DIM = 128
GDN_W = GDN_HEADS * GDN_HEAD_DIM
GDN_CONV = 5
GDN_CHUNK = 128
GDN_BASE = 16
SCAN_CHUNKS = 2
POOL_SIZES = (2, 4, 8, 16)
POOL_GROUP = 128
POOL_WIDTH = POOL_GROUP * len(POOL_SIZES)
N_MOD = 9
N_BRANCH = 3
EPS = 1e-6

LANES = 128
HALO = 8
VMEM_LIMIT = 56 * 1024 * 1024
COL_CHUNK = 512


def _dot(a, b, precision=None):
    return jnp.dot(a, b, preferred_element_type=F32, precision=precision)


def _dot_nt(a, b):
    return lax.dot_general(a, b, (((1,), (1,)), ((), ())), preferred_element_type=F32)


def _silu(x):
    return x * jax.nn.sigmoid(x)


def _params(sem):
    return pltpu.CompilerParams(dimension_semantics=sem, vmem_limit_bytes=VMEM_LIMIT)


def _const_spec(shape):
    nd = len(shape)
    return pl.BlockSpec(shape, lambda *_: (0,) * nd, pipeline_mode=pl.Buffered(1))


def _layer_spec(arr, l):
    nd = arr.ndim - 1
    return pl.BlockSpec((None,) + arr.shape[1:], lambda *_: (l,) + (0,) * nd, pipeline_mode=pl.Buffered(1))


def _mod_spec(d, l, j):
    return pl.BlockSpec((None, 8, 3 * d), lambda *_: (l, 0, j), pipeline_mode=pl.Buffered(1))


def _norm_mod(x, gain, mod_ref, rows, ctx_len):
    d = x.shape[1]
    y = x * lax.rsqrt(jnp.mean(x * x, axis=-1, keepdims=True) + EPS) * gain
    is_ctx = rows < ctx_len
    shift = jnp.where(is_ctx, mod_ref[1:2, 0:d], mod_ref[0:1, 0:d])
    scale = jnp.where(is_ctx, mod_ref[1:2, d:2 * d], mod_ref[0:1, d:2 * d])
    return y * (1.0 + scale) + shift


def _mod_gate(mod_ref, rows, ctx_len, d):
    return jnp.where(rows < ctx_len, mod_ref[1:2, 2 * d:3 * d], mod_ref[0:1, 2 * d:3 * d])


def _row_ids(row0, n):
    return row0 + lax.broadcasted_iota(jnp.int32, (n, 1), 0)


def _ada_kernel(c_ref, w_ref, b_ref, out_ref):
    w = w_ref[...]
    rows = [jnp.sum(w * _silu(c_ref[:, r:r + 1]), axis=0, keepdims=True) for r in range(2)]
    out_ref[...] = jnp.concatenate(rows + [jnp.zeros((6, w.shape[1]), F32)], axis=0) + b_ref[...]


def _ada_call(c_cols, w_ada, b_ada):
    depth, d, n = w_ada.shape
    tn = n // 8
    return pl.pallas_call(
        _ada_kernel,
        grid=(depth, n // tn),
        in_specs=[pl.BlockSpec((d, LANES), lambda l, j: (0, 0)),
                  pl.BlockSpec((None, d, tn), lambda l, j: (l, 0, j)),
                  pl.BlockSpec((None, 1, tn), lambda l, j: (l, 0, j))],
        out_specs=pl.BlockSpec((None, 8, tn), lambda l, j: (l, 0, j)),
        out_shape=jax.ShapeDtypeStruct((depth, 8, n), F32),
        compiler_params=_params(("parallel", "parallel")),
        name="ada_mod",
    )(c_cols, w_ada, b_ada.reshape(depth, 1, n))


def _swiglu_half_step(x, rows, mod_ref, gain_ref, win_ref, wout_ref, *, ctx_len, d_ff, fc):
    hb = _norm_mod(x, gain_ref[...], mod_ref, rows, ctx_len).astype(BF16)
    acc = jnp.zeros(x.shape, F32)
    for j in range(d_ff // fc):
        gt = _dot(hb, win_ref[:, j * fc:(j + 1) * fc])
        up = _dot(hb, win_ref[:, d_ff + j * fc:d_ff + (j + 1) * fc])
        acc = acc + _dot((_silu(gt) * up).astype(BF16), wout_ref[j * fc:(j + 1) * fc, :])
    return x + (0.5 * _mod_gate(mod_ref, rows, ctx_len, x.shape[1])) * acc


_SEC = (("q", 512), ("k", 128), ("v", 128), ("gqkv", 3 * GDN_W), ("z", GDN_W), ("ab", LANES),
        ("pin", POOL_WIDTH), ("gl", N_BRANCH * 1024))
_SEC_SRC = (512, 128, 128, 3 * GDN_W, GDN_W, 4 * GDN_HEADS, POOL_WIDTH, N_BRANCH * 1024)


def _sec_offsets():
    offs, o = {}, 0
    for name, width in _SEC:
        offs[name] = (o, width)
        o += width
    return offs, o


def _rope(t, cos, sin):
    n = t.shape[-1]
    lane = lax.broadcasted_iota(jnp.int32, t.shape, 1)
    swapped = jnp.where(lane % 32 < 16, pltpu.roll(t, n - 16, 1), pltpu.roll(t, 16, 1))
    reps = n // LANES
    c = jnp.concatenate([cos] * reps, axis=1) if reps > 1 else cos
    s = jnp.concatenate([sin] * reps, axis=1) if reps > 1 else sin
    return t * c + swapped * s


def _dup_halves(t):
    lo = lax.broadcasted_iota(jnp.int32, t.shape, 1) < LANES // 2
    r = pltpu.roll(t, LANES // 2, 1)
    return jnp.concatenate([jnp.where(lo, t, r), jnp.where(lo, r, t)], axis=1)


def _ffn_proj_kernel(*refs, tm, ctx_len, d_ff, fc, first_layer):
    if first_layer:
        ctx_ref, refs = refs[0], refs[1:]
    (x_ref, modf_ref, gainf_ref, win_ref, wout_ref, modp_ref, gainp_ref, w_ref, cos_ref, sin_ref,
     xo_ref, q_ref, k_ref, v_ref, gqkv_ref, z_ref, ab_ref, pin_ref, gate_ref) = refs
    offs, _ = _sec_offsets()
    i = pl.program_id(0)
    rows = _row_ids(i * tm, tm)
    x = x_ref[...]
    if first_layer:
        x = jnp.where(i == 0, ctx_ref[...], x)
    x = _swiglu_half_step(x, rows, modf_ref, gainf_ref, win_ref, wout_ref, ctx_len=ctx_len, d_ff=d_ff, fc=fc)
    xo_ref[...] = x
    hb = _norm_mod(x, gainp_ref[...], modp_ref, rows, ctx_len).astype(BF16)

    def sec(name, out_ref, fn):
        o, width = offs[name]
        step = min(width, COL_CHUNK)
        for a in range(0, width, step):
            out_ref[:, a:a + step] = fn(_dot(hb, w_ref[:, o + a:o + a + step])).astype(out_ref.dtype)

    cos, sin = cos_ref[...], sin_ref[...]
    sec("q", q_ref, lambda t: _rope(t, cos, sin) * (ATTN_HEAD_DIM ** -0.5))
    o, width = offs["k"]
    k_ref[...] = _dup_halves(_rope(_dot(hb, w_ref[:, o:o + width]), cos, sin)).astype(BF16)
    o, width = offs["v"]
    v_ref[...] = _dup_halves(_dot(hb, w_ref[:, o:o + width])).astype(BF16)
    ident = lambda t: t
    sec("gqkv", gqkv_ref, ident)
    sec("z", z_ref, ident)
    sec("ab", ab_ref, ident)
    sec("pin", pin_ref, ident)
    sec("gl", gate_ref, jax.nn.sigmoid)


def _ffn_proj_call(x_in, ctx, mods, gainf, w_in, w_out, gainp, w, cos, sin, *, l, tm, ctx_len):
    first_layer = ctx is not None
    d = x_in.shape[1]
    la = x_in.shape[0] + (ctx_len if first_layer else 0)
    offs, _ = _sec_offsets()
    d_ff = w_out.shape[1]
    fc = d_ff // 2
    kern = functools.partial(_ffn_proj_kernel, tm=tm, ctx_len=ctx_len, d_ff=d_ff, fc=fc, first_layer=first_layer)
    row = lambda width: pl.BlockSpec((tm, width), lambda i: (i, 0))
    if first_layer:
        assert tm == ctx_len
        lead_specs = [_const_spec((tm, d)), pl.BlockSpec((tm, d), lambda i: (jnp.maximum(i - 1, 0), 0))]
        lead_args = (ctx, x_in)
        aliases = {}
    else:
        lead_specs = [row(d)]
        lead_args = (x_in,)
        aliases = {0: 0}
    outs = ((d, F32), (512, BF16), (2 * LANES, BF16), (2 * LANES, BF16), (3 * GDN_W, F32), (GDN_W, BF16),
            (LANES, F32), (POOL_WIDTH, F32), (offs["gl"][1], BF16))
    return pl.pallas_call(
        kern,
        grid=(la // tm,),
        in_specs=lead_specs + [_mod_spec(d, l, 0), _layer_spec(gainf, l), _layer_spec(w_in, l),
                               _layer_spec(w_out, l), _mod_spec(d, l, 1), _layer_spec(gainp, l),
                               _layer_spec(w, l), row(LANES), row(LANES)],
        out_specs=[row(width) for width, _ in outs],
        out_shape=[jax.ShapeDtypeStruct((la, width), dt) for width, dt in outs],
        input_output_aliases=aliases,
        compiler_params=_params(("parallel",)),
        name="ffn_proj",
    )(*lead_args, mods, gainf, w_in, w_out, mods, gainp, w, cos, sin)


def _attn_stages(sink_ref, q_ref, kp_ref, kc_ref, kn_ref, vp_ref, vc_ref, vn_ref, kx_ref, vx_ref,
                 o_ref, *, l, ctx_blocks, n_blocks):
    b0 = ATTN_STEP_BLOCKS * pl.program_id(0)
    blk = ATTN_BLOCK
    is_lat = b0 >= ctx_blocks
    ri = lax.broadcasted_iota(jnp.int32, (blk, blk), 0)
    ci = lax.broadcasted_iota(jnp.int32, (blk, blk), 1)
    ninf = -jnp.inf
    lo = lax.broadcasted_iota(jnp.int32, (blk, LANES), 1) < LANES // 2
    pairs = ATTN_HEADS // ATTN_KV_HEADS // 2
    groups = [(sb, hk) for sb in range(ATTN_STEP_BLOCKS) for hk in range(ATTN_KV_HEADS)]

    def window(ref_p, ref_c, ref_n, sb, ks):
        blocks = [ref_p[:, ks]] + [ref_c[j * blk:(j + 1) * blk, ks] for j in range(ATTN_STEP_BLOCKS)] + [ref_n[:, ks]]
        return jnp.concatenate(blocks[sb:sb + 3], axis=0)

    bias = []
    for sb in range(ATTN_STEP_BLOCKS):
        prev_ok = b0 + sb - 1 >= ctx_blocks
        next_ok = jnp.logical_and(is_lat, b0 + sb + 1 < n_blocks)
        bias.append(jnp.concatenate(
            [jnp.where(jnp.logical_and(ci >= ri, prev_ok), 0.0, ninf),
             jnp.where(jnp.broadcast_to(is_lat, (blk, blk)), 0.0, ninf),
             jnp.where(jnp.logical_and(ci <= ri, next_ok), 0.0, ninf)], axis=1))

    scores, values, sinks = [], [], []
    for sb, hk in groups:
        ks = slice(hk * LANES, (hk + 1) * LANES)
        rows = slice(sb * blk, (sb + 1) * blk)
        q_parts, sk = [], []
        for j in range(pairs):
            col = (hk * pairs + j) * LANES
            qp = q_ref[rows, col:col + LANES]
            q_parts += [jnp.where(lo, qp, jnp.zeros_like(qp)), jnp.where(lo, jnp.zeros_like(qp), qp)]
            sk += [sink_ref[l, (hk * pairs + j) * 2], sink_ref[l, (hk * pairs + j) * 2 + 1]]
        q4 = jnp.concatenate(q_parts, axis=0)
        scores.append((_dot_nt(q4, window(kp_ref, kc_ref, kn_ref, sb, ks)), _dot_nt(q4, kx_ref[:, ks])))
        values.append((window(vp_ref, vc_ref, vn_ref, sb, ks), vx_ref[:, ks]))
        sinks.append(sk)
        yield
    probs = []
    for g, (sb, hk) in enumerate(groups):
        s_win, s_ctx = scores[g]
        p_win, p_ctx, inv = [], [], []
        for j, sk in enumerate(sinks[g]):
            rs = slice(j * blk, (j + 1) * blk)
            sw = s_win[rs] + bias[sb]
            sx = s_ctx[rs]
            m = jnp.maximum(jnp.maximum(jnp.max(sw, axis=-1, keepdims=True),
                                        jnp.max(sx, axis=-1, keepdims=True)), sk)
            pw = jnp.exp(sw - m)
            px = jnp.exp(sx - m)
            denom = jnp.sum(pw, axis=-1, keepdims=True) + jnp.sum(px, axis=-1, keepdims=True) + jnp.exp(sk - m)
            p_win.append(pw.astype(BF16))
            p_ctx.append(px.astype(BF16))
            inv.append(1.0 / denom)
        probs.append((jnp.concatenate(p_win, axis=0), jnp.concatenate(p_ctx, axis=0), inv))
    for g, (sb, hk) in enumerate(groups):
        p_win, p_ctx, inv = probs[g]
        vw, vx = values[g]
        r = _dot(p_win, vw) + _dot(p_ctx, vx)
        for j in range(pairs):
            r0 = r[(2 * j) * blk:(2 * j + 1) * blk] * inv[2 * j]
            r1 = r[(2 * j + 1) * blk:(2 * j + 2) * blk] * inv[2 * j + 1]
            col = (hk * pairs + j) * LANES
            o_ref[sb * blk:(sb + 1) * blk, col:col + LANES] = jnp.where(lo, r0, r1).astype(BF16)
        yield


def _tri_inverse_masks(n):
    ri = lax.broadcasted_iota(jnp.int32, (n, n), 0)
    ci = lax.broadcasted_iota(jnp.int32, (n, n), 1)
    eye = (ri == ci).astype(F32)
    same = lambda s: (ri // s) == (ci // s)
    offs, s = [], GDN_BASE
    while s < n:
        offs.append(jnp.logical_and(same(2 * s), jnp.logical_not(same(s))))
        s *= 2
    return eye, same(GDN_BASE), offs


def _interleave(*stage_generators):
    stages = list(stage_generators)
    while stages:
        for g in list(stages):
            try:
                next(g)
            except StopIteration:
                stages.remove(g)


def _tri_inverse_stages(a_list, masks, out):
    eye, diag, offs = masks
    a0 = [jnp.where(diag, a, 0.0) for a in a_list]
    a0b = [x.astype(BF16) for x in a0]
    t = [eye - x for x in a0]
    p = [_dot(x, x) for x in a0b]
    yield
    s = 2
    while True:
        pb = [x.astype(BF16) for x in p]
        t = [ti + _dot(ti.astype(BF16), pi) for ti, pi in zip(t, pb)]
        yield
        s *= 2
        if s >= GDN_BASE:
            break
        p = [_dot(x, x) for x in pb]
        yield
    for off in offs:
        tb = [x.astype(BF16) for x in t]
        mid = [_dot(ti, jnp.where(off, a, 0.0).astype(BF16)).astype(BF16) for ti, a in zip(tb, a_list)]
        yield
        t = [ti - _dot(mi, tbi) for ti, mi, tbi in zip(t, mid, tb)]
        yield
    out.extend(t)


def _gdn_prep_kernel(prev_ref, cur_ref, next_ref, ab_ref, convw_ref, abp_ref,
                     u_ref, w_ref, qd_ref, a2_ref, gt_ref, xs_ref, a_s, rhs_s, *, tg, ctx_len, total, ntiles):
    i = pl.program_id(0)
    dk = GDN_HEAD_DIM
    c_len = GDN_CHUNK
    heads = range(GDN_HEADS)
    chunks = range(tg // c_len)
    units = [(c, h, d) for c in chunks for h in heads for d in range(2)]
    hsl = [slice(h * dk, (h + 1) * dk) for h in heads]

    @pl.when(i == 0)
    def _():
        a_s[1] = jnp.zeros(a_s.shape[1:], F32)
        rhs_s[1] = jnp.zeros(rhs_s.shape[1:], BF16)

    slot = i % 2
    row0 = jnp.minimum(i, ntiles - 1) * tg
    first = jnp.logical_or(row0 == 0, row0 == ctx_len)
    last = jnp.logical_or(row0 + tg == ctx_len, row0 + tg == total)
    xs_ref[0:HALO, :] = jnp.where(first, 0.0, prev_ref[...])
    xs_ref[HALO:HALO + tg, :] = cur_ref[...]
    xs_ref[HALO + tg:HALO + tg + HALO, :] = jnp.where(last, 0.0, next_ref[...])
    pad = (GDN_CONV - 1) // 2

    def conv_silu(col, r0):
        ls = slice(col * dk, (col + 1) * dk)
        acc = xs_ref[pl.ds(HALO - pad + r0, c_len), ls] * convw_ref[0:1, ls]
        for j in range(1, GDN_CONV):
            acc = acc + xs_ref[pl.ds(HALO - pad + j + r0, c_len), ls] * convw_ref[j:j + 1, ls]
        return _silu(acc)

    def l2n(t):
        return t * lax.rsqrt(jnp.sum(t * t, axis=-1, keepdims=True) + EPS)

    ri = lax.broadcasted_iota(jnp.int32, (c_len, c_len), 0)
    ci = lax.broadcasted_iota(jnp.int32, (c_len, c_len), 1)
    strict = (ci < ri, ci > ri)
    incl = (ci <= ri, ci >= ri)
    tri = (incl[0].astype(F32), incl[1].astype(F32))
    last_row = (c_len - 1, 0)
    masks = _tri_inverse_masks(c_len)

    def back():
        a_prev = [a_s[1 - slot, n] for n in range(len(units))]
        t_prev = []
        yield from _tri_inverse_stages(a_prev, masks, t_prev)
        uw = [_dot(t_.astype(BF16), rhs_s[1 - slot, n]) for n, t_ in enumerate(t_prev)]
        yield
        for n, (c, h, d) in enumerate(units):
            rs = slice(c * c_len, (c + 1) * c_len)
            u_ref[d, rs, hsl[h]] = uw[n][:, 0:dk].astype(BF16)
            w_ref[d, rs, hsl[h]] = uw[n][:, dk:2 * dk].astype(BF16)

    def front():
        g_all = [-jnp.exp(abp_ref[1:2, :]) * jax.nn.softplus(ab_ref[c * c_len:(c + 1) * c_len, :] + abp_ref[0:1, :])
                 for c in chunks]
        b_all = [jax.nn.sigmoid(ab_ref[c * c_len:(c + 1) * c_len, :]) for c in chunks]
        gcs = [[_dot(tri[d], g_all[c], HIGHEST) for d in range(2)] for c in chunks]
        gct = [[jnp.transpose(x) for x in gcs[c]] for c in chunks]
        yield

        def finish(c, h, q, k, v, kt, kk, qkt):
            rs = slice(c * c_len, (c + 1) * c_len)
            for d in range(2):
                n = units.index((c, h, d))
                lane = d * GDN_HEADS + h
                gc = gcs[c][d][:, lane:lane + 1]
                gr = gct[c][d][lane:lane + 1, :]
                beta = b_all[c][:, 2 * GDN_HEADS + lane:2 * GDN_HEADS + lane + 1]
                e = jnp.exp(jnp.where(incl[d], gc - gr, -jnp.inf))
                egc = jnp.exp(gc)
                g_last = gc[last_row[d]:last_row[d] + 1, :]
                a_s[slot, n] = jnp.where(strict[d], beta * kk * e, 0.0)
                rhs_s[slot, n, :, 0:dk] = (beta * v).astype(BF16)
                rhs_s[slot, n, :, dk:2 * dk] = (beta * egc * k).astype(BF16)
                qd_ref[d, rs, hsl[h]] = (q * egc).astype(BF16)
                a2_ref[d, c, 0:c_len, hsl[h]] = (qkt * e).astype(BF16)
                a2_ref[d, c, c_len:2 * c_len, hsl[h]] = (kt * jnp.exp(g_last - gr)).astype(BF16)
                gt_ref[d, c, h:h + 1, :] = jnp.broadcast_to(jnp.exp(g_last), (1, LANES))

        pending = None
        for c in chunks:
            for h in heads:
                q = l2n(conv_silu(h, c * c_len)) * (dk ** -0.5)
                k = l2n(conv_silu(GDN_HEADS + h, c * c_len))
                v = conv_silu(2 * GDN_HEADS + h, c * c_len)
                kt = jnp.transpose(k)
                ktb = kt.astype(BF16)
                issued = (c, h, q, k, v, kt, _dot(k.astype(BF16), ktb), _dot(q.astype(BF16), ktb))
                yield
                if pending is not None:
                    finish(*pending)
                pending = issued
        finish(*pending)

    _interleave(back(), front())


def _gdn_prep_call(gqkv, ab, conv_w, abp, *, l, tg, ctx_len):
    la, wq = gqkv.shape
    c_len = GDN_CHUNK
    nch = la // c_len
    cpt = tg // c_len
    hb = tg // HALO
    nhalo = la // HALO
    ntiles = la // tg
    n_units = cpt * GDN_HEADS * 2
    kern = functools.partial(_gdn_prep_kernel, tg=tg, ctx_len=ctx_len, total=la, ntiles=ntiles)
    cur = lambda i: jnp.minimum(i, ntiles - 1)
    prv = lambda i: jnp.maximum(i - 1, 0)
    rows_cur = pl.BlockSpec((2, tg, GDN_W), lambda i: (0, cur(i), 0))
    rows_prv = pl.BlockSpec((2, tg, GDN_W), lambda i: (0, prv(i), 0))
    return pl.pallas_call(
        kern,
        grid=(ntiles + 1,),
        in_specs=[pl.BlockSpec((HALO, wq), lambda i: (jnp.maximum(cur(i) * hb - 1, 0), 0)),
                  pl.BlockSpec((tg, wq), lambda i: (cur(i), 0)),
                  pl.BlockSpec((HALO, wq), lambda i: (jnp.minimum((cur(i) + 1) * hb, nhalo - 1), 0)),
                  pl.BlockSpec((tg, LANES), lambda i: (cur(i), 0)),
                  _layer_spec(conv_w, l), _layer_spec(abp, l)],
        out_specs=[rows_prv, rows_prv, rows_cur,
                   pl.BlockSpec((2, cpt, 2 * c_len, GDN_W), lambda i: (0, cur(i), 0, 0)),
                   pl.BlockSpec((2, cpt, GDN_HEADS, LANES), lambda i: (0, cur(i), 0, 0))],
        out_shape=[jax.ShapeDtypeStruct((2, la, GDN_W), BF16),
                   jax.ShapeDtypeStruct((2, la, GDN_W), BF16),
                   jax.ShapeDtypeStruct((2, la, GDN_W), BF16),
                   jax.ShapeDtypeStruct((2, nch, 2 * c_len, GDN_W), BF16),
                   jax.ShapeDtypeStruct((2, nch, GDN_HEADS, LANES), F32)],
        scratch_shapes=[pltpu.VMEM((tg + 2 * HALO, wq), F32),
                        pltpu.VMEM((2, n_units, c_len, c_len), F32),
                        pltpu.VMEM((2, n_units, c_len, 2 * GDN_HEAD_DIM), BF16)],
        compiler_params=_params(("arbitrary",)),
        name="gdn_prep",
    )(gqkv, gqkv, gqkv, ab, conv_w, abp)


def _scan_stages(uf, ub, wf, wb, qdf, qdb, a2f, a2b, gtf, gtb, of_ref, ob_ref, s_ref):
    dk = GDN_HEAD_DIM
    c_len = GDN_CHUNK
    dirs = ((uf, wf, qdf, a2f, gtf, of_ref), (ub, wb, qdb, a2b, gtb, ob_ref))
    chains = [(d, h) for d in range(2) for h in range(GDN_HEADS)]
    hs = [slice(h * dk, (h + 1) * dk) for h in range(GDN_HEADS)]
    s = [s_ref[d * GDN_HEADS + h] for d, h in chains]
    for step in range(SCAN_CHUNKS):
        sub = (step, SCAN_CHUNKS - 1 - step)
        rows = [slice(sub[d] * c_len, (sub[d] + 1) * c_len) for d in range(2)]
        r1 = [_dot(jnp.concatenate([dirs[d][1][rows[d], hs[h]], dirs[d][2][rows[d], hs[h]]], axis=0),
                   s_.astype(BF16)) for (d, h), s_ in zip(chains, s)]
        yield
        vb = [(dirs[d][0][rows[d], hs[h]].astype(F32) - r[0:c_len]).astype(BF16) for (d, h), r in zip(chains, r1)]
        r2 = [_dot(dirs[d][3][sub[d], :, hs[h]], v_) for (d, h), v_ in zip(chains, vb)]
        yield
        for (d, h), ra, rb in zip(chains, r1, r2):
            dirs[d][5][rows[d], hs[h]] = (ra[c_len:2 * c_len] + rb[0:c_len]).astype(BF16)
        s = [s_ * dirs[d][4][sub[d], h:h + 1, :] + rb[c_len:2 * c_len] for (d, h), s_, rb in zip(chains, s, r2)]
    for (d, h), s_ in zip(chains, s):
        s_ref[d * GDN_HEADS + h] = s_


def _attn_scan_kernel(*refs, l, ctx_blocks, n_blocks):
    attn_in, scan_in = refs[0:10], refs[10:20]
    o_attn, of_ref, ob_ref, s_ref = refs[20:24]

    @pl.when(pl.program_id(0) == 0)
    def _():
        s_ref[...] = jnp.zeros(s_ref.shape, F32)

    _interleave(_scan_stages(*scan_in, of_ref, ob_ref, s_ref),
                _attn_stages(*attn_in, o_attn, l=l, ctx_blocks=ctx_blocks, n_blocks=n_blocks))


def _attn_scan_call(sink, q, k, v, u, w, qd, a2, gt, *, l, ctx_len):
    la = q.shape[0]
    blk = ATTN_BLOCK
    step = ATTN_STEP_BLOCKS
    nb = la // blk
    cb = ctx_len // blk
    rows = step * blk
    assert rows == SCAN_CHUNKS * GDN_CHUNK and la % rows == 0 and ctx_len % rows == 0
    nsteps = la // rows
    ncs = ctx_len // rows
    kern = functools.partial(_attn_scan_kernel, l=l, ctx_blocks=cb, n_blocks=nb)
    kvw = k.shape[1]
    prev = pl.BlockSpec((blk, kvw), lambda p: (jnp.maximum(step * p - 1, 0), 0))
    cur = pl.BlockSpec((rows, kvw), lambda p: (p, 0))
    nxt = pl.BlockSpec((blk, kvw), lambda p: (jnp.minimum(step * (p + 1), nb - 1), 0))
    cx = pl.BlockSpec((ctx_len, kvw), lambda p: (0, 0))

    def bwd(n):
        return jnp.where(n < ncs, ncs - 1 - n, nsteps + ncs - 1 - n)

    def both(make):
        return [make(0, lambda n: n), make(1, bwd)]

    r_specs = both(lambda d, f: pl.BlockSpec((None, rows, GDN_W), lambda n: (d, f(n), 0)))
    a_specs = both(lambda d, f: pl.BlockSpec((None, SCAN_CHUNKS, 2 * GDN_CHUNK, GDN_W), lambda n: (d, f(n), 0, 0)))
    g_specs = both(lambda d, f: pl.BlockSpec((None, SCAN_CHUNKS, GDN_HEADS, LANES), lambda n: (d, f(n), 0, 0)))
    return pl.pallas_call(
        kern,
        grid=(nsteps,),
        in_specs=[pl.BlockSpec(memory_space=pltpu.SMEM),
                  pl.BlockSpec((rows, q.shape[1]), lambda p: (p, 0)),
                  prev, cur, nxt, prev, cur, nxt, cx, cx] + r_specs + r_specs + r_specs + a_specs + g_specs,
        out_specs=[pl.BlockSpec((rows, q.shape[1]), lambda p: (p, 0)),
                   pl.BlockSpec((rows, GDN_W), lambda n: (n, 0)),
                   pl.BlockSpec((rows, GDN_W), lambda n: (bwd(n), 0))],
        out_shape=[jax.ShapeDtypeStruct(q.shape, BF16),
                   jax.ShapeDtypeStruct((la, GDN_W), BF16), jax.ShapeDtypeStruct((la, GDN_W), BF16)],
        scratch_shapes=[pltpu.VMEM((2 * GDN_HEADS, GDN_HEAD_DIM, GDN_HEAD_DIM), F32)],
        compiler_params=_params(("arbitrary",)),
        name="attn_scan",
    )(sink, q, k, k, k, v, v, v, k, v, u, u, w, w, qd, qd, a2, a2, gt, gt)


def _merge_ffn_kernel(x_ref, attn_ref, of_ref, ob_ref, z_ref, pprev_ref, pcur_ref, pnext_ref, gate_ref,
                      modm_ref, gnorm_ref, poolw_ref, pscale_ref, wba_ref, wbg_ref, wbp_ref, wout_ref,
                      modf_ref, gainf_ref, win_ref, wo2_ref, fgain_ref, out_ref, ps_ref, xn_ref,
                      *, tm, ctx_len, total, ntiles, d_ff, fc, final):
    i = pl.program_id(0)
    d = x_ref.shape[1]
    slot = i % 2

    @pl.when(i == 0)
    def _():
        xn_ref[1] = jnp.zeros(xn_ref.shape[1:], F32)

    row0 = jnp.minimum(i, ntiles - 1) * tm
    first = jnp.logical_or(row0 == 0, row0 == ctx_len)
    last = jnp.logical_or(row0 + tm == ctx_len, row0 + tm == total)
    ps_ref[0:HALO, :] = jnp.where(first, 0.0, pprev_ref[...])
    ps_ref[HALO:HALO + tm, :] = pcur_ref[...]
    ps_ref[HALO + tm:HALO + tm + HALO, :] = jnp.where(last, 0.0, pnext_ref[...])
    rows = _row_ids(row0, tm)
    is_ctx = rows < ctx_len
    seg_pos = jnp.where(is_ctx, rows, rows - ctx_len)
    seg_len = jnp.where(is_ctx, ctx_len, total - ctx_len)
    m = {}

    def merge_vector_prep():
        m["pool_r"] = []
        for gi, win in enumerate(POOL_SIZES):
            ls = slice(gi * POOL_GROUP, (gi + 1) * POOL_GROUP)
            s = ps_ref[pl.ds(HALO - win // 2, tm), ls]
            for o in range(1, win):
                s = s + ps_ref[pl.ds(HALO - win // 2 + o, tm), ls]
            cnt = (jnp.minimum(seg_pos + win // 2, seg_len) - jnp.maximum(seg_pos - win // 2, 0)).astype(F32)
            m["pool_r"].append((s / cnt - pcur_ref[:, ls]).astype(BF16))
        gdn = []
        for h in range(GDN_HEADS):
            hs = slice(h * GDN_HEAD_DIM, (h + 1) * GDN_HEAD_DIM)
            o = of_ref[:, hs].astype(F32) + ob_ref[:, hs].astype(F32)
            o = o * lax.rsqrt(jnp.mean(o * o, axis=-1, keepdims=True) + EPS) * gnorm_ref[...]
            gdn.append(o * _silu(z_ref[:, hs].astype(F32)))
        m["gdn_o"] = jnp.concatenate(gdn, axis=1).astype(BF16)

    def merge_branches():
        pooled = [_dot(r, poolw_ref[gi]) * pscale_ref[:, gi * POOL_GROUP:(gi + 1) * POOL_GROUP]
                  for gi, r in enumerate(m["pool_r"])]
        pool_o = jnp.concatenate(pooled, axis=1).astype(BF16)
        m["merged"] = (gate_ref[:, 0:d].astype(F32) * _dot(attn_ref[...], wba_ref[...])
                       + gate_ref[:, d:2 * d].astype(F32) * _dot(m["gdn_o"], wbg_ref[...])
                       + gate_ref[:, 2 * d:3 * d].astype(F32) * _dot(pool_o, wbp_ref[...])).astype(BF16)

    def merge_out():
        y = _dot(m["merged"], wout_ref[...])
        xn_ref[slot] = x_ref[...] + _mod_gate(modm_ref, rows, ctx_len, d) * y

    xp = xn_ref[1 - slot]
    rows_p = _row_ids((i - 1) * tm, tm)
    hb = _norm_mod(xp, gainf_ref[...], modf_ref, rows_p, ctx_len).astype(BF16)
    acc = jnp.zeros(xp.shape, F32)
    n_chunks = d_ff // fc
    for j in range(n_chunks):
        gt = _dot(hb, win_ref[:, j * fc:(j + 1) * fc])
        up = _dot(hb, win_ref[:, d_ff + j * fc:d_ff + (j + 1) * fc])
        if j == 0:
            merge_vector_prep()
        acc = acc + _dot((_silu(gt) * up).astype(BF16), wo2_ref[j * fc:(j + 1) * fc, :])
    merge_branches()
    merge_out()
    y = xp + (0.5 * _mod_gate(modf_ref, rows_p, ctx_len, d)) * acc
    if final:
        y = y * lax.rsqrt(jnp.mean(y * y, axis=-1, keepdims=True) + EPS) * fgain_ref[...]
    out_ref[...] = y


def _merge_ffn_call(xa, attn_o, o_f, o_b, z, pin, gates, mods, gnorm, pool_w, pool_scale, wba, wbg, wbp, wout,
                    gainf, w_in, w_out, fgain, *, l, tm, ctx_len, final):
    la, d = xa.shape
    hb = tm // HALO
    nhalo = la // HALO
    pw = pin.shape[1]
    ntiles = la // tm
    d_ff = w_out.shape[1]
    fc = d_ff // 2
    assert d_ff // fc == 2
    kern = functools.partial(_merge_ffn_kernel, tm=tm, ctx_len=ctx_len, total=la, ntiles=ntiles,
                             d_ff=d_ff, fc=fc, final=final)
    cur = lambda i: jnp.minimum(i, ntiles - 1)
    row = lambda width: pl.BlockSpec((tm, width), lambda i: (cur(i), 0))
    if final:
        skip = ctx_len // tm
        out_spec = pl.BlockSpec((tm, d), lambda i: (jnp.maximum(i - 1 - skip, 0), 0))
        out_shape = jax.ShapeDtypeStruct((la - ctx_len, d), F32)
        aliases = {}
    else:
        out_spec = pl.BlockSpec((tm, d), lambda i: (jnp.maximum(i - 1, 0), 0))
        out_shape = jax.ShapeDtypeStruct((la, d), F32)
        aliases = {0: 0}
    return pl.pallas_call(
        kern,
        grid=(ntiles + 1,),
        in_specs=[row(d), row(attn_o.shape[1]), row(GDN_W), row(GDN_W), row(GDN_W),
                  pl.BlockSpec((HALO, pw), lambda i: (jnp.maximum(cur(i) * hb - 1, 0), 0)),
                  row(pw),
                  pl.BlockSpec((HALO, pw), lambda i: (jnp.minimum((cur(i) + 1) * hb, nhalo - 1), 0)),
                  row(gates.shape[1]),
                  _mod_spec(d, l, 1), _layer_spec(gnorm, l), _layer_spec(pool_w, l), _layer_spec(pool_scale, l),
                  _layer_spec(wba, l), _layer_spec(wbg, l), _layer_spec(wbp, l), _layer_spec(wout, l),
                  _mod_spec(d, l, 2), _layer_spec(gainf, l), _layer_spec(w_in, l), _layer_spec(w_out, l),
                  _const_spec((1, d))],
        out_specs=out_spec,
        out_shape=out_shape,
        input_output_aliases=aliases,
        scratch_shapes=[pltpu.VMEM((tm + 2 * HALO, pw), F32), pltpu.VMEM((2, tm, d), F32)],
        compiler_params=_params(("arbitrary",)),
        name="merge_ffn",
    )(xa, attn_o, o_f, o_b, z, pin, pin, pin, gates, mods, gnorm, pool_w, pool_scale, wba, wbg, wbp, wout,
      mods, gainf, w_in, w_out, fgain)


def _rope_tables(seq, ctx_len):
    quarter = ATTN_HEAD_DIM // 4
    rows = seq // GRID_W
    inv = ROPE_THETA ** (-jnp.arange(quarter, dtype=F32) / quarter)

    def axis_tables(n):
        ang = jnp.arange(n, dtype=F32)[:, None] * inv[None, :]
        c, s = jnp.cos(ang), jnp.sin(ang)
        return jnp.concatenate([c, c], axis=1), jnp.concatenate([-s, s], axis=1)

    cr, sr = axis_tables(rows)
    cc, sc = axis_tables(GRID_W)

    def full(tr, tc):
        t = jnp.concatenate([jnp.broadcast_to(tr[:, None, :], (rows, GRID_W, 2 * quarter)),
                             jnp.broadcast_to(tc[None, :, :], (rows, GRID_W, 2 * quarter))], axis=-1)
        t = t.reshape(seq, ATTN_HEAD_DIM)
        return jnp.concatenate([t] * (LANES // ATTN_HEAD_DIM), axis=1)

    cos = jnp.concatenate([jnp.ones((ctx_len, LANES), F32), full(cr, cc)], axis=0)
    sin = jnp.concatenate([jnp.zeros((ctx_len, LANES), F32), full(sr, sc)], axis=0)
    return cos, sin


PACK_TILES = 17


def _pack_kernel(*refs, short_tile, short_rows):
    out_ref = refs[-1]
    window = jnp.concatenate([r[...] for r in refs[:-1]], axis=0)
    rows = lax.broadcasted_iota(jnp.int32, (LANES, window.shape[1]), 0)
    for s in range(PACK_TILES):
        t = PACK_TILES * pl.program_id(1) + s
        own = window[(s + 1) * LANES:(s + 2) * LANES]
        shifted = window[s * LANES + short_rows:(s + 1) * LANES + short_rows]
        src = jnp.where(t < short_tile, own, jnp.where(t == short_tile, jnp.where(rows < short_rows, own, 0.0), shifted))
        out_ref[:, s * LANES:(s + 1) * LANES] = jnp.transpose(src).astype(BF16)


def _pack_w_in(w_in):
    depth, d, n = w_in.shape
    offs, n_packed = _sec_offsets()
    (short_off, _), short_rows = offs["ab"], _SEC_SRC[[name for name, _ in _SEC].index("ab")]
    assert short_off % LANES == 0 and sum(w != s for (_, w), s in zip(_SEC, _SEC_SRC)) == 1
    assert n_packed % (PACK_TILES * LANES) == 0
    short_tile = short_off // LANES
    last = (n - 1) // LANES

    def src_block(k):
        return pl.BlockSpec((None, LANES, d), lambda l, j: (l, jnp.clip(PACK_TILES * j - 1 + k, 0, last), 0))

    kern = functools.partial(_pack_kernel, short_tile=short_tile, short_rows=short_rows)
    wt = jnp.swapaxes(w_in, 1, 2)
    n_src = PACK_TILES + 1
    return pl.pallas_call(
        kern,
        grid=(depth, n_packed // (PACK_TILES * LANES)),
        in_specs=[src_block(k) for k in range(n_src)],
        out_specs=pl.BlockSpec((None, d, PACK_TILES * LANES), lambda l, j: (l, 0, j)),
        out_shape=jax.ShapeDtypeStruct((depth, d, n_packed), BF16),
        compiler_params=_params(("parallel", "parallel")),
        name="pack_w_in",
    )(*([wt] * n_src))


def kernel(x, c, ctx, c_ctx, w_ada, b_ada, norm_ffn1, w_ffn1_in, w_ffn1_out, norm_mix, w_in, attn_sink,
           gdn_conv, gdn_a_log, gdn_dt_bias, gdn_norm, pool_w, pool_scale, w_branch_attn, w_branch_gdn,
           w_branch_pool, w_out, norm_ffn2, w_ffn2_in, w_ffn2_out, final_norm):
    batch, seq, d = x.shape
    assert batch == 1, "single-sequence kernel"
    ctx_len = ctx.shape[1]
    depth = w_ada.shape[0]
    tm = ctx_len
    assert tm % (SCAN_CHUNKS * GDN_CHUNK) == 0 and seq % tm == 0 and seq % GRID_W == 0

    c_cols = jnp.zeros((d, LANES), F32).at[:, 0].set(c[0]).at[:, 1].set(c_ctx)
    mods = _ada_call(c_cols, w_ada, b_ada)
    cos, sin = _rope_tables(seq, ctx_len)

    bf = lambda w: w.astype(BF16)
    unit = lambda p: p[:, None, :]
    w1i, w1o, w2i, w2o = bf(w_ffn1_in), bf(w_ffn1_out), bf(w_ffn2_in), bf(w_ffn2_out)
    wproj = _pack_w_in(w_in)
    wba, wbg, wbp, wo, pw = bf(w_branch_attn), bf(w_branch_gdn), bf(w_branch_pool), bf(w_out), bf(pool_w)
    n1, nm, n2, gn, psc = unit(norm_ffn1), unit(norm_mix), unit(norm_ffn2), unit(gdn_norm), unit(pool_scale)
    abp = jnp.stack([gdn_dt_bias.reshape(depth, -1), gdn_a_log.reshape(depth, -1)], axis=1)
    abp = jnp.pad(abp, ((0, 0), (0, 6), (0, LANES - 2 * GDN_HEADS)))

    for l in range(depth):
        xa, q, k, v, gqkv, z, ab, pin, gates = _ffn_proj_call(
            x[0] if l == 0 else xa, ctx[0] if l == 0 else None, mods, n1, w1i, w1o, nm, wproj, cos, sin,
            l=l, tm=tm, ctx_len=ctx_len)
        u, w, qd, a2, gt = _gdn_prep_call(gqkv, ab, gdn_conv, abp, l=l, tg=tm, ctx_len=ctx_len)
        attn_o, o_f, o_b = _attn_scan_call(attn_sink, q, k, v, u, w, qd, a2, gt, l=l, ctx_len=ctx_len)
        xa = _merge_ffn_call(xa, attn_o, o_f, o_b, z, pin, gates, mods, gn, pw, psc, wba, wbg, wbp, wo,
                             n2, w2i, w2o, final_norm[None], l=l, tm=tm, ctx_len=ctx_len, final=l == depth - 1)
    return xa[None]
```

```python
import functools

import jax
import jax.numpy as jnp
from jax import lax
from jax.experimental import pallas as pl
from jax.experimental.pallas import tpu as pltpu

F32 = jnp.float32
BF16 = jnp.bfloat16
HIGHEST = lax.Precision.HIGHEST

GRID_W = 64
ATTN_HEADS = 8
ATTN_KV_HEADS = 2
ATTN_HEAD_DIM = 64
ATTN_BLOCK = 128
ATTN_STEP_BLOCKS = 2
ROPE_THETA = 10000.0
GDN_HEADS = 4
GDN_HEAD_DIM = 128
GDN_W = GDN_HEADS * GDN_HEAD_DIM
GDN_CONV = 5
GDN_CHUNK = 128
GDN_BASE = 16
SCAN_CHUNKS = 2
POOL_SIZES = (2, 4, 8, 16)
POOL_GROUP = 128
POOL_WIDTH = POOL_GROUP * len(POOL_SIZES)
N_MOD = 9
N_BRANCH = 3
EPS = 1e-6

LANES = 128
HALO = 8
VMEM_LIMIT = 56 * 1024 * 1024
COL_CHUNK = 512


def _dot(a, b, precision=None):
    return jnp.dot(a, b, preferred_element_type=F32, precision=precision)


def _dot_nt(a, b):
    return lax.dot_general(a, b, (((1,), (1,)), ((), ())), preferred_element_type=F32)


def _silu(x):
    return x * jax.nn.sigmoid(x)


def _params(sem):
    return pltpu.CompilerParams(dimension_semantics=sem, vmem_limit_bytes=VMEM_LIMIT)


def _const_spec(shape):
    nd = len(shape)
    return pl.BlockSpec(shape, lambda *_: (0,) * nd, pipeline_mode=pl.Buffered(1))


def _layer_spec(arr, l):
    nd = arr.ndim - 1
    return pl.BlockSpec((None,) + arr.shape[1:], lambda *_: (l,) + (0,) * nd, pipeline_mode=pl.Buffered(1))


def _mod_spec(d, l, j):
    return pl.BlockSpec((None, 8, 3 * d), lambda *_: (l, 0, j), pipeline_mode=pl.Buffered(1))


def _norm_mod(x, gain, mod_ref, rows, ctx_len):
    d = x.shape[1]
    y = x * lax.rsqrt(jnp.mean(x * x, axis=-1, keepdims=True) + EPS) * gain
    is_ctx = rows < ctx_len
    shift = jnp.where(is_ctx, mod_ref[1:2, 0:d], mod_ref[0:1, 0:d])
    scale = jnp.where(is_ctx, mod_ref[1:2, d:2 * d], mod_ref[0:1, d:2 * d])
    return y * (1.0 + scale) + shift


def _mod_gate(mod_ref, rows, ctx_len, d):
    return jnp.where(rows < ctx_len, mod_ref[1:2, 2 * d:3 * d], mod_ref[0:1, 2 * d:3 * d])


def _row_ids(row0, n):
    return row0 + lax.broadcasted_iota(jnp.int32, (n, 1), 0)


def _ada_kernel(c_ref, w_ref, b_ref, out_ref):
    w = w_ref[...]
    rows = [jnp.sum(w * _silu(c_ref[:, r:r + 1]), axis=0, keepdims=True) for r in range(2)]
    out_ref[...] = jnp.concatenate(rows + [jnp.zeros((6, w.shape[1]), F32)], axis=0) + b_ref[...]


def _ada_call(c_cols, w_ada, b_ada):
    depth, d, n = w_ada.shape
    tn = n // 8
    return pl.pallas_call(
        _ada_kernel,
        grid=(depth, n // tn),
        in_specs=[pl.BlockSpec((d, LANES), lambda l, j: (0, 0)),
                  pl.BlockSpec((None, d, tn), lambda l, j: (l, 0, j)),
                  pl.BlockSpec((None, 1, tn), lambda l, j: (l, 0, j))],
        out_specs=pl.BlockSpec((None, 8, tn), lambda l, j: (l, 0, j)),
        out_shape=jax.ShapeDtypeStruct((depth, 8, n), F32),
        compiler_params=_params(("parallel", "parallel")),
        name="ada_mod",
    )(c_cols, w_ada, b_ada.reshape(depth, 1, n))


def _swiglu_half_step(x, rows, mod_ref, gain_ref, win_ref, wout_ref, *, ctx_len, d_ff, fc):
    hb = _norm_mod(x, gain_ref[...], mod_ref, rows, ctx_len).astype(BF16)
    acc = jnp.zeros(x.shape, F32)
    for j in range(d_ff // fc):
        gt = _dot(hb, win_ref[:, j * fc:(j + 1) * fc])
        up = _dot(hb, win_ref[:, d_ff + j * fc:d_ff + (j + 1) * fc])
        acc = acc + _dot((_silu(gt) * up).astype(BF16), wout_ref[j * fc:(j + 1) * fc, :])
    return x + (0.5 * _mod_gate(mod_ref, rows, ctx_len, x.shape[1])) * acc


_SEC = (("q", 512), ("k", 128), ("v", 128), ("gqkv", 3 * GDN_W), ("z", GDN_W), ("ab", LANES),
        ("pin", POOL_WIDTH), ("gl", N_BRANCH * 1024))
_SEC_SRC = (512, 128, 128, 3 * GDN_W, GDN_W, 4 * GDN_HEADS, POOL_WIDTH, N_BRANCH * 1024)


def _sec_offsets():
    offs, o = {}, 0
    for name, width in _SEC:
        offs[name] = (o, width)
        o += width
    return offs, o


def _rope(t, cos, sin):
    n = t.shape[-1]
    lane = lax.broadcasted_iota(jnp.int32, t.shape, 1)
    swapped = jnp.where(lane % 32 < 16, pltpu.roll(t, n - 16, 1), pltpu.roll(t, 16, 1))
    reps = n // LANES
    c = jnp.concatenate([cos] * reps, axis=1) if reps > 1 else cos
    s = jnp.concatenate([sin] * reps, axis=1) if reps > 1 else sin
    return t * c + swapped * s


def _dup_halves(t):
    lo = lax.broadcasted_iota(jnp.int32, t.shape, 1) < LANES // 2
    r = pltpu.roll(t, LANES // 2, 1)
    return jnp.concatenate([jnp.where(lo, t, r), jnp.where(lo, r, t)], axis=1)


def _ffn_proj_kernel(*refs, tm, ctx_len, d_ff, fc, first_layer):
    if first_layer:
        ctx_ref, refs = refs[0], refs[1:]
    (x_ref, modf_ref, gainf_ref, win_ref, wout_ref, modp_ref, gainp_ref, w_ref, cos_ref, sin_ref,
     xo_ref, q_ref, k_ref, v_ref, gqkv_ref, z_ref, ab_ref, pin_ref, gate_ref) = refs
    offs, _ = _sec_offsets()
    i = pl.program_id(0)
    rows = _row_ids(i * tm, tm)
    x = x_ref[...]
    if first_layer:
        x = jnp.where(i == 0, ctx_ref[...], x)
    x = _swiglu_half_step(x, rows, modf_ref, gainf_ref, win_ref, wout_ref, ctx_len=ctx_len, d_ff=d_ff, fc=fc)
    xo_ref[...] = x
    hb = _norm_mod(x, gainp_ref[...], modp_ref, rows, ctx_len).astype(BF16)

    def sec(name, out_ref, fn):
        o, width = offs[name]
        step = min(width, COL_CHUNK)
        for a in range(0, width, step):
            out_ref[:, a:a + step] = fn(_dot(hb, w_ref[:, o + a:o + a + step])).astype(out_ref.dtype)

    cos, sin = cos_ref[...], sin_ref[...]
    sec("q", q_ref, lambda t: _rope(t, cos, sin) * (ATTN_HEAD_DIM ** -0.5))
    o, width = offs["k"]
    k_ref[...] = _dup_halves(_rope(_dot(hb, w_ref[:, o:o + width]), cos, sin)).astype(BF16)
    o, width = offs["v"]
    v_ref[...] = _dup_halves(_dot(hb, w_ref[:, o:o + width])).astype(BF16)
    ident = lambda t: t
    sec("gqkv", gqkv_ref, ident)
    sec("z", z_ref, ident)
    sec("ab", ab_ref, ident)
    sec("pin", pin_ref, ident)
    sec("gl", gate_ref, jax.nn.sigmoid)


def _ffn_proj_call(x_in, ctx, mods, gainf, w_in, w_out, gainp, w, cos, sin, *, l, tm, ctx_len):
    first_layer = ctx is not None
    d = x_in.shape[1]
    la = x_in.shape[0] + (ctx_len if first_layer else 0)
    offs, _ = _sec_offsets()
    d_ff = w_out.shape[1]
    fc = d_ff // 2
    kern = functools.partial(_ffn_proj_kernel, tm=tm, ctx_len=ctx_len, d_ff=d_ff, fc=fc, first_layer=first_layer)
    row = lambda width: pl.BlockSpec((tm, width), lambda i: (i, 0))
    if first_layer:
        assert tm == ctx_len
        lead_specs = [_const_spec((tm, d)), pl.BlockSpec((tm, d), lambda i: (jnp.maximum(i - 1, 0), 0))]
        lead_args = (ctx, x_in)
        aliases = {}
    else:
        lead_specs = [row(d)]
        lead_args = (x_in,)
        aliases = {0: 0}
    outs = ((d, F32), (512, BF16), (2 * LANES, BF16), (2 * LANES, BF16), (3 * GDN_W, F32), (GDN_W, BF16),
            (LANES, F32), (POOL_WIDTH, F32), (offs["gl"][1], BF16))
    return pl.pallas_call(
        kern,
        grid=(la // tm,),
        in_specs=lead_specs + [_mod_spec(d, l, 0), _layer_spec(gainf, l), _layer_spec(w_in, l),
                               _layer_spec(w_out, l), _mod_spec(d, l, 1), _layer_spec(gainp, l),
                               _layer_spec(w, l), row(LANES), row(LANES)],
        out_specs=[row(width) for width, _ in outs],
        out_shape=[jax.ShapeDtypeStruct((la, width), dt) for width, dt in outs],
        input_output_aliases=aliases,
        compiler_params=_params(("parallel",)),
        name="ffn_proj",
    )(*lead_args, mods, gainf, w_in, w_out, mods, gainp, w, cos, sin)


def _attn_stages(sink_ref, q_ref, kp_ref, kc_ref, kn_ref, vp_ref, vc_ref, vn_ref, kx_ref, vx_ref,
                 o_ref, *, l, ctx_blocks, n_blocks):
    b0 = ATTN_STEP_BLOCKS * pl.program_id(0)
    blk = ATTN_BLOCK
    is_lat = b0 >= ctx_blocks
    ri = lax.broadcasted_iota(jnp.int32, (blk, blk), 0)
    ci = lax.broadcasted_iota(jnp.int32, (blk, blk), 1)
    ninf = -jnp.inf
    lo = lax.broadcasted_iota(jnp.int32, (blk, LANES), 1) < LANES // 2
    pairs = ATTN_HEADS // ATTN_KV_HEADS // 2
    groups = [(sb, hk) for sb in range(ATTN_STEP_BLOCKS) for hk in range(ATTN_KV_HEADS)]

    def window(ref_p, ref_c, ref_n, sb, ks):
        blocks = [ref_p[:, ks]] + [ref_c[j * blk:(j + 1) * blk, ks] for j in range(ATTN_STEP_BLOCKS)] + [ref_n[:, ks]]
        return jnp.concatenate(blocks[sb:sb + 3], axis=0)

    bias = []
    for sb in range(ATTN_STEP_BLOCKS):
        prev_ok = b0 + sb - 1 >= ctx_blocks
        next_ok = jnp.logical_and(is_lat, b0 + sb + 1 < n_blocks)
        bias.append(jnp.concatenate(
            [jnp.where(jnp.logical_and(ci >= ri, prev_ok), 0.0, ninf),
             jnp.where(jnp.broadcast_to(is_lat, (blk, blk)), 0.0, ninf),
             jnp.where(jnp.logical_and(ci <= ri, next_ok), 0.0, ninf)], axis=1))

    scores, values, sinks = [], [], []
    for sb, hk in groups:
        ks = slice(hk * LANES, (hk + 1) * LANES)
        rows = slice(sb * blk, (sb + 1) * blk)
        q_parts, sk = [], []
        for j in range(pairs):
            col = (hk * pairs + j) * LANES
            qp = q_ref[rows, col:col + LANES]
            q_parts += [jnp.where(lo, qp, jnp.zeros_like(qp)), jnp.where(lo, jnp.zeros_like(qp), qp)]
            sk += [sink_ref[l, (hk * pairs + j) * 2], sink_ref[l, (hk * pairs + j) * 2 + 1]]
        q4 = jnp.concatenate(q_parts, axis=0)
        scores.append((_dot_nt(q4, window(kp_ref, kc_ref, kn_ref, sb, ks)), _dot_nt(q4, kx_ref[:, ks])))
        values.append((window(vp_ref, vc_ref, vn_ref, sb, ks), vx_ref[:, ks]))
        sinks.append(sk)
        yield
    probs = []
    for g, (sb, hk) in enumerate(groups):
        s_win, s_ctx = scores[g]
        p_win, p_ctx, inv = [], [], []
        for j, sk in enumerate(sinks[g]):
            rs = slice(j * blk, (j + 1) * blk)
            sw = s_win[rs] + bias[sb]
            sx = s_ctx[rs]
            m = jnp.maximum(jnp.maximum(jnp.max(sw, axis=-1, keepdims=True),
                                        jnp.max(sx, axis=-1, keepdims=True)), sk)
            pw = jnp.exp(sw - m)
            px = jnp.exp(sx - m)
            denom = jnp.sum(pw, axis=-1, keepdims=True) + jnp.sum(px, axis=-1, keepdims=True) + jnp.exp(sk - m)
            p_win.append(pw.astype(BF16))
            p_ctx.append(px.astype(BF16))
            inv.append(1.0 / denom)
        probs.append((jnp.concatenate(p_win, axis=0), jnp.concatenate(p_ctx, axis=0), inv))
    for g, (sb, hk) in enumerate(groups):
        p_win, p_ctx, inv = probs[g]
        vw, vx = values[g]
        r = _dot(p_win, vw) + _dot(p_ctx, vx)
        for j in range(pairs):
            r0 = r[(2 * j) * blk:(2 * j + 1) * blk] * inv[2 * j]
            r1 = r[(2 * j + 1) * blk:(2 * j + 2) * blk] * inv[2 * j + 1]
            col = (hk * pairs + j) * LANES
            o_ref[sb * blk:(sb + 1) * blk, col:col + LANES] = jnp.where(lo, r0, r1).astype(BF16)
        yield


def _tri_inverse_masks(n):
    ri = lax.broadcasted_iota(jnp.int32, (n, n), 0)
    ci = lax.broadcasted_iota(jnp.int32, (n, n), 1)
    eye = (ri == ci).astype(F32)
    same = lambda s: (ri // s) == (ci // s)
    offs, s = [], GDN_BASE
    while s < n:
        offs.append(jnp.logical_and(same(2 * s), jnp.logical_not(same(s))))
        s *= 2
    return eye, same(GDN_BASE), offs


def _interleave(*stage_generators):
    stages = list(stage_generators)
    while stages:
        for g in list(stages):
            try:
                next(g)
            except StopIteration:
                stages.remove(g)


def _tri_inverse(a_list, masks, between=None):
    eye, diag, offs = masks
    a0 = [jnp.where(diag, a, 0.0) for a in a_list]
    a0b = [x.astype(BF16) for x in a0]
    t = [eye - x for x in a0]
    p = [_dot(x, x) for x in a0b]
    s = 2
    while True:
        pb = [x.astype(BF16) for x in p]
        t = [ti + _dot(ti.astype(BF16), pi) for ti, pi in zip(t, pb)]
        s *= 2
        if s >= GDN_BASE:
            break
        p = [_dot(x, x) for x in pb]
    if between is not None:
        between()
    for off in offs:
        tb = [x.astype(BF16) for x in t]
        mid = [_dot(ti, jnp.where(off, a, 0.0).astype(BF16)).astype(BF16) for ti, a in zip(tb, a_list)]
        t = [ti - _dot(mi, tbi) for ti, mi, tbi in zip(t, mid, tb)]
    return t


def _gdn_prep_kernel(prev_ref, cur_ref, next_ref, ab_ref, convw_ref, abp_ref,
                     u_ref, w_ref, qd_ref, a2_ref, gt_ref, xs_ref, a_s, rhs_s, *, tg, ctx_len, total, ntiles):
    i = pl.program_id(0)
    dk = GDN_HEAD_DIM
    c_len = GDN_CHUNK
    heads = range(GDN_HEADS)
    chunks = range(tg // c_len)
    units = [(c, h, d) for c in chunks for h in heads for d in range(2)]
    hsl = [slice(h * dk, (h + 1) * dk) for h in heads]

    @pl.when(i == 0)
    def _():
        a_s[1] = jnp.zeros(a_s.shape[1:], F32)
        rhs_s[1] = jnp.zeros(rhs_s.shape[1:], BF16)

    slot = i % 2
    row0 = jnp.minimum(i, ntiles - 1) * tg
    first = jnp.logical_or(row0 == 0, row0 == ctx_len)
    last = jnp.logical_or(row0 + tg == ctx_len, row0 + tg == total)
    xs_ref[0:HALO, :] = jnp.where(first, 0.0, prev_ref[...])
    xs_ref[HALO:HALO + tg, :] = cur_ref[...]
    xs_ref[HALO + tg:HALO + tg + HALO, :] = jnp.where(last, 0.0, next_ref[...])
    pad = (GDN_CONV - 1) // 2

    def conv_silu(col, r0):
        ls = slice(col * dk, (col + 1) * dk)
        acc = xs_ref[pl.ds(HALO - pad + r0, c_len), ls] * convw_ref[0:1, ls]
        for j in range(1, GDN_CONV):
            acc = acc + xs_ref[pl.ds(HALO - pad + j + r0, c_len), ls] * convw_ref[j:j + 1, ls]
        return _silu(acc)

    def l2n(t):
        return t * lax.rsqrt(jnp.sum(t * t, axis=-1, keepdims=True) + EPS)

    ri = lax.broadcasted_iota(jnp.int32, (c_len, c_len), 0)
    ci = lax.broadcasted_iota(jnp.int32, (c_len, c_len), 1)
    strict = (ci < ri, ci > ri)
    incl = (ci <= ri, ci >= ri)
    tri = (incl[0].astype(F32), incl[1].astype(F32))
    last_row = (c_len - 1, 0)
    masks = _tri_inverse_masks(c_len)
    f = {}

    def front_matmuls():
        g_all = [-jnp.exp(abp_ref[1:2, :]) * jax.nn.softplus(ab_ref[c * c_len:(c + 1) * c_len, :] + abp_ref[0:1, :])
                 for c in chunks]
        f["b_all"] = [jax.nn.sigmoid(ab_ref[c * c_len:(c + 1) * c_len, :]) for c in chunks]
        f["gcs"] = [[_dot(tri[d], g_all[c], HIGHEST) for d in range(2)] for c in chunks]
        f["gct"] = [[jnp.transpose(x) for x in f["gcs"][c]] for c in chunks]
        pairs = [(c, h) for c in chunks for h in heads]
        f["qs"] = {ch: l2n(conv_silu(ch[1], ch[0] * c_len)) * (dk ** -0.5) for ch in pairs}
        f["ks"] = {ch: l2n(conv_silu(GDN_HEADS + ch[1], ch[0] * c_len)) for ch in pairs}
        f["vs"] = {ch: conv_silu(2 * GDN_HEADS + ch[1], ch[0] * c_len) for ch in pairs}
        f["kts"] = {ch: jnp.transpose(f["ks"][ch]) for ch in pairs}
        ktb = {ch: f["kts"][ch].astype(BF16) for ch in pairs}
        f["kk"] = {ch: _dot(f["ks"][ch].astype(BF16), ktb[ch]) for ch in pairs}
        f["qkt"] = {ch: _dot(f["qs"][ch].astype(BF16), ktb[ch]) for ch in pairs}

    a_prev = [a_s[1 - slot, n] for n in range(len(units))]
    t_prev = _tri_inverse(a_prev, masks, between=front_matmuls)
    uw = [_dot(t_.astype(BF16), rhs_s[1 - slot, n]) for n, t_ in enumerate(t_prev)]
    for n, (c, h, d) in enumerate(units):
        rs = slice(c * c_len, (c + 1) * c_len)
        u_ref[d, rs, hsl[h]] = uw[n][:, 0:dk].astype(BF16)
        w_ref[d, rs, hsl[h]] = uw[n][:, dk:2 * dk].astype(BF16)

    for n, (c, h, d) in enumerate(units):
        rs = slice(c * c_len, (c + 1) * c_len)
        lane = d * GDN_HEADS + h
        gc = f["gcs"][c][d][:, lane:lane + 1]
        gr = f["gct"][c][d][lane:lane + 1, :]
        beta = f["b_all"][c][:, 2 * GDN_HEADS + lane:2 * GDN_HEADS + lane + 1]
        e = jnp.exp(jnp.where(incl[d], gc - gr, -jnp.inf))
        egc = jnp.exp(gc)
        g_last = gc[last_row[d]:last_row[d] + 1, :]
        a_s[slot, n] = jnp.where(strict[d], beta * f["kk"][(c, h)] * e, 0.0)
        rhs_s[slot, n, :, 0:dk] = (beta * f["vs"][(c, h)]).astype(BF16)
        rhs_s[slot, n, :, dk:2 * dk] = (beta * egc * f["ks"][(c, h)]).astype(BF16)
        qd_ref[d, rs, hsl[h]] = (f["qs"][(c, h)] * egc).astype(BF16)
        a2_ref[d, c, 0:c_len, hsl[h]] = (f["qkt"][(c, h)] * e).astype(BF16)
        a2_ref[d, c, c_len:2 * c_len, hsl[h]] = (f["kts"][(c, h)] * jnp.exp(g_last - gr)).astype(BF16)
        gt_ref[d, c, h:h + 1, :] = jnp.broadcast_to(jnp.exp(g_last), (1, LANES))


def _gdn_prep_call(gqkv, ab, conv_w, abp, *, l, tg, ctx_len):
    la, wq = gqkv.shape
    c_len = GDN_CHUNK
    nch = la // c_len
    cpt = tg // c_len
    hb = tg // HALO
    nhalo = la // HALO
    ntiles = la // tg
    n_units = cpt * GDN_HEADS * 2
    kern = functools.partial(_gdn_prep_kernel, tg=tg, ctx_len=ctx_len, total=la, ntiles=ntiles)
    cur = lambda i: jnp.minimum(i, ntiles - 1)
    prv = lambda i: jnp.maximum(i - 1, 0)
    rows_cur = pl.BlockSpec((2, tg, GDN_W), lambda i: (0, cur(i), 0))
    rows_prv = pl.BlockSpec((2, tg, GDN_W), lambda i: (0, prv(i), 0))
    return pl.pallas_call(
        kern,
        grid=(ntiles + 1,),
        in_specs=[pl.BlockSpec((HALO, wq), lambda i: (jnp.maximum(cur(i) * hb - 1, 0), 0)),
                  pl.BlockSpec((tg, wq), lambda i: (cur(i), 0)),
                  pl.BlockSpec((HALO, wq), lambda i: (jnp.minimum((cur(i) + 1) * hb, nhalo - 1), 0)),
                  pl.BlockSpec((tg, LANES), lambda i: (cur(i), 0)),
                  _layer_spec(conv_w, l), _layer_spec(abp, l)],
        out_specs=[rows_prv, rows_prv, rows_cur,
                   pl.BlockSpec((2, cpt, 2 * c_len, GDN_W), lambda i: (0, cur(i), 0, 0)),
                   pl.BlockSpec((2, cpt, GDN_HEADS, LANES), lambda i: (0, cur(i), 0, 0))],
        out_shape=[jax.ShapeDtypeStruct((2, la, GDN_W), BF16),
                   jax.ShapeDtypeStruct((2, la, GDN_W), BF16),
                   jax.ShapeDtypeStruct((2, la, GDN_W), BF16),
                   jax.ShapeDtypeStruct((2, nch, 2 * c_len, GDN_W), BF16),
                   jax.ShapeDtypeStruct((2, nch, GDN_HEADS, LANES), F32)],
        scratch_shapes=[pltpu.VMEM((tg + 2 * HALO, wq), F32),
                        pltpu.VMEM((2, n_units, c_len, c_len), F32),
                        pltpu.VMEM((2, n_units, c_len, 2 * GDN_HEAD_DIM), BF16)],
        compiler_params=_params(("arbitrary",)),
        name="gdn_prep",
    )(gqkv, gqkv, gqkv, ab, conv_w, abp)


def _scan_stages(uf, ub, wf, wb, qdf, qdb, a2f, a2b, gtf, gtb, of_ref, ob_ref, s_ref):
    dk = GDN_HEAD_DIM
    c_len = GDN_CHUNK
    dirs = ((uf, wf, qdf, a2f, gtf, of_ref), (ub, wb, qdb, a2b, gtb, ob_ref))
    chains = [(d, h) for d in range(2) for h in range(GDN_HEADS)]
    hs = [slice(h * dk, (h + 1) * dk) for h in range(GDN_HEADS)]
    s = [s_ref[d * GDN_HEADS + h] for d, h in chains]
    for step in range(SCAN_CHUNKS):
        sub = (step, SCAN_CHUNKS - 1 - step)
        rows = [slice(sub[d] * c_len, (sub[d] + 1) * c_len) for d in range(2)]
        r1 = [_dot(jnp.concatenate([dirs[d][1][rows[d], hs[h]], dirs[d][2][rows[d], hs[h]]], axis=0),
                   s_.astype(BF16)) for (d, h), s_ in zip(chains, s)]
        yield
        vb = [(dirs[d][0][rows[d], hs[h]].astype(F32) - r[0:c_len]).astype(BF16) for (d, h), r in zip(chains, r1)]
        r2 = [_dot(dirs[d][3][sub[d], :, hs[h]], v_) for (d, h), v_ in zip(chains, vb)]
        yield
        for (d, h), ra, rb in zip(chains, r1, r2):
            dirs[d][5][rows[d], hs[h]] = (ra[c_len:2 * c_len] + rb[0:c_len]).astype(BF16)
        s = [s_ * dirs[d][4][sub[d], h:h + 1, :] + rb[c_len:2 * c_len] for (d, h), s_, rb in zip(chains, s, r2)]
    for (d, h), s_ in zip(chains, s):
        s_ref[d * GDN_HEADS + h] = s_


def _attn_scan_kernel(*refs, l, ctx_blocks, n_blocks):
    attn_in, scan_in = refs[0:10], refs[10:20]
    o_attn, of_ref, ob_ref, s_ref = refs[20:24]

    @pl.when(pl.program_id(0) == 0)
    def _():
        s_ref[...] = jnp.zeros(s_ref.shape, F32)

    _interleave(_scan_stages(*scan_in, of_ref, ob_ref, s_ref),
                _attn_stages(*attn_in, o_attn, l=l, ctx_blocks=ctx_blocks, n_blocks=n_blocks))


def _attn_scan_call(sink, q, k, v, u, w, qd, a2, gt, *, l, ctx_len):
    la = q.shape[0]
    blk = ATTN_BLOCK
    step = ATTN_STEP_BLOCKS
    nb = la // blk
    cb = ctx_len // blk
    rows = step * blk
    assert rows == SCAN_CHUNKS * GDN_CHUNK and la % rows == 0 and ctx_len % rows == 0
    nsteps = la // rows
    ncs = ctx_len // rows
    kern = functools.partial(_attn_scan_kernel, l=l, ctx_blocks=cb, n_blocks=nb)
    kvw = k.shape[1]
    prev = pl.BlockSpec((blk, kvw), lambda p: (jnp.maximum(step * p - 1, 0), 0))
    cur = pl.BlockSpec((rows, kvw), lambda p: (p, 0))
    nxt = pl.BlockSpec((blk, kvw), lambda p: (jnp.minimum(step * (p + 1), nb - 1), 0))
    cx = pl.BlockSpec((ctx_len, kvw), lambda p: (0, 0))

    def bwd(n):
        return jnp.where(n < ncs, ncs - 1 - n, nsteps + ncs - 1 - n)

    def both(make):
        return [make(0, lambda n: n), make(1, bwd)]

    r_specs = both(lambda d, f: pl.BlockSpec((None, rows, GDN_W), lambda n: (d, f(n), 0)))
    a_specs = both(lambda d, f: pl.BlockSpec((None, SCAN_CHUNKS, 2 * GDN_CHUNK, GDN_W), lambda n: (d, f(n), 0, 0)))
    g_specs = both(lambda d, f: pl.BlockSpec((None, SCAN_CHUNKS, GDN_HEADS, LANES), lambda n: (d, f(n), 0, 0)))
    return pl.pallas_call(
        kern,
        grid=(nsteps,),
        in_specs=[pl.BlockSpec(memory_space=pltpu.SMEM),
                  pl.BlockSpec((rows, q.shape[1]), lambda p: (p, 0)),
                  prev, cur, nxt, prev, cur, nxt, cx, cx] + r_specs + r_specs + r_specs + a_specs + g_specs,
        out_specs=[pl.BlockSpec((rows, q.shape[1]), lambda p: (p, 0)),
                   pl.BlockSpec((rows, GDN_W), lambda n: (n, 0)),
                   pl.BlockSpec((rows, GDN_W), lambda n: (bwd(n), 0))],
        out_shape=[jax.ShapeDtypeStruct(q.shape, BF16),
                   jax.ShapeDtypeStruct((la, GDN_W), BF16), jax.ShapeDtypeStruct((la, GDN_W), BF16)],
        scratch_shapes=[pltpu.VMEM((2 * GDN_HEADS, GDN_HEAD_DIM, GDN_HEAD_DIM), F32)],
        compiler_params=_params(("arbitrary",)),
        name="attn_scan",
    )(sink, q, k, k, k, v, v, v, k, v, u, u, w, w, qd, qd, a2, a2, gt, gt)


def _merge_ffn_kernel(x_ref, attn_ref, of_ref, ob_ref, z_ref, pprev_ref, pcur_ref, pnext_ref, gate_ref,
                      modm_ref, gnorm_ref, poolw_ref, pscale_ref, wba_ref, wbg_ref, wbp_ref, wout_ref,
                      modf_ref, gainf_ref, win_ref, wo2_ref, fgain_ref, out_ref, ps_ref, xn_ref,
                      *, tm, ctx_len, total, ntiles, d_ff, fc, final):
    i = pl.program_id(0)
    d = x_ref.shape[1]
    slot = i % 2

    @pl.when(i == 0)
    def _():
        xn_ref[1] = jnp.zeros(xn_ref.shape[1:], F32)

    row0 = jnp.minimum(i, ntiles - 1) * tm
    first = jnp.logical_or(row0 == 0, row0 == ctx_len)
    last = jnp.logical_or(row0 + tm == ctx_len, row0 + tm == total)
    ps_ref[0:HALO, :] = jnp.where(first, 0.0, pprev_ref[...])
    ps_ref[HALO:HALO + tm, :] = pcur_ref[...]
    ps_ref[HALO + tm:HALO + tm + HALO, :] = jnp.where(last, 0.0, pnext_ref[...])
    rows = _row_ids(row0, tm)
    is_ctx = rows < ctx_len
    seg_pos = jnp.where(is_ctx, rows, rows - ctx_len)
    seg_len = jnp.where(is_ctx, ctx_len, total - ctx_len)
    m = {}

    def merge_vector_prep():
        m["pool_r"] = []
        for gi, win in enumerate(POOL_SIZES):
            ls = slice(gi * POOL_GROUP, (gi + 1) * POOL_GROUP)
            s = ps_ref[pl.ds(HALO - win // 2, tm), ls]
            for o in range(1, win):
                s = s + ps_ref[pl.ds(HALO - win // 2 + o, tm), ls]
            cnt = (jnp.minimum(seg_pos + win // 2, seg_len) - jnp.maximum(seg_pos - win // 2, 0)).astype(F32)
            m["pool_r"].append((s / cnt - pcur_ref[:, ls]).astype(BF16))
        gdn = []
        for h in range(GDN_HEADS):
            hs = slice(h * GDN_HEAD_DIM, (h + 1) * GDN_HEAD_DIM)
            o = of_ref[:, hs].astype(F32) + ob_ref[:, hs].astype(F32)
            o = o * lax.rsqrt(jnp.mean(o * o, axis=-1, keepdims=True) + EPS) * gnorm_ref[...]
            gdn.append(o * _silu(z_ref[:, hs].astype(F32)))
        m["gdn_o"] = jnp.concatenate(gdn, axis=1).astype(BF16)

    def merge_branches():
        pooled = [_dot(r, poolw_ref[gi]) * pscale_ref[:, gi * POOL_GROUP:(gi + 1) * POOL_GROUP]
                  for gi, r in enumerate(m["pool_r"])]
        pool_o = jnp.concatenate(pooled, axis=1).astype(BF16)
        m["merged"] = (gate_ref[:, 0:d].astype(F32) * _dot(attn_ref[...], wba_ref[...])
                       + gate_ref[:, d:2 * d].astype(F32) * _dot(m["gdn_o"], wbg_ref[...])
                       + gate_ref[:, 2 * d:3 * d].astype(F32) * _dot(pool_o, wbp_ref[...])).astype(BF16)

    def merge_out():
        y = _dot(m["merged"], wout_ref[...])
        xn_ref[slot] = x_ref[...] + _mod_gate(modm_ref, rows, ctx_len, d) * y

    xp = xn_ref[1 - slot]
    rows_p = _row_ids((i - 1) * tm, tm)
    hb = _norm_mod(xp, gainf_ref[...], modf_ref, rows_p, ctx_len).astype(BF16)
    acc = jnp.zeros(xp.shape, F32)
    n_chunks = d_ff // fc
    for j in range(n_chunks):
        gt = _dot(hb, win_ref[:, j * fc:(j + 1) * fc])
        up = _dot(hb, win_ref[:, d_ff + j * fc:d_ff + (j + 1) * fc])
        if j == 0:
            merge_vector_prep()
        acc = acc + _dot((_silu(gt) * up).astype(BF16), wo2_ref[j * fc:(j + 1) * fc, :])
    merge_branches()
    merge_out()
    y = xp + (0.5 * _mod_gate(modf_ref, rows_p, ctx_len, d)) * acc
    if final:
        y = y * lax.rsqrt(jnp.mean(y * y, axis=-1, keepdims=True) + EPS) * fgain_ref[...]
    out_ref[...] = y


def _merge_ffn_call(xa, attn_o, o_f, o_b, z, pin, gates, mods, gnorm, pool_w, pool_scale, wba, wbg, wbp, wout,
                    gainf, w_in, w_out, fgain, *, l, tm, ctx_len, final):
    la, d = xa.shape
    hb = tm // HALO
    nhalo = la // HALO
    pw = pin.shape[1]
    ntiles = la // tm
    d_ff = w_out.shape[1]
    fc = d_ff // 2
    assert d_ff // fc == 2
    kern = functools.partial(_merge_ffn_kernel, tm=tm, ctx_len=ctx_len, total=la, ntiles=ntiles,
                             d_ff=d_ff, fc=fc, final=final)
    cur = lambda i: jnp.minimum(i, ntiles - 1)
    row = lambda width: pl.BlockSpec((tm, width), lambda i: (cur(i), 0))
    if final:
        skip = ctx_len // tm
        out_spec = pl.BlockSpec((tm, d), lambda i: (jnp.maximum(i - 1 - skip, 0), 0))
        out_shape = jax.ShapeDtypeStruct((la - ctx_len, d), F32)
        aliases = {}
    else:
        out_spec = pl.BlockSpec((tm, d), lambda i: (jnp.maximum(i - 1, 0), 0))
        out_shape = jax.ShapeDtypeStruct((la, d), F32)
        aliases = {0: 0}
    return pl.pallas_call(
        kern,
        grid=(ntiles + 1,),
        in_specs=[row(d), row(attn_o.shape[1]), row(GDN_W), row(GDN_W), row(GDN_W),
                  pl.BlockSpec((HALO, pw), lambda i: (jnp.maximum(cur(i) * hb - 1, 0), 0)),
                  row(pw),
                  pl.BlockSpec((HALO, pw), lambda i: (jnp.minimum((cur(i) + 1) * hb, nhalo - 1), 0)),
                  row(gates.shape[1]),
                  _mod_spec(d, l, 1), _layer_spec(gnorm, l), _layer_spec(pool_w, l), _layer_spec(pool_scale, l),
                  _layer_spec(wba, l), _layer_spec(wbg, l), _layer_spec(wbp, l), _layer_spec(wout, l),
                  _mod_spec(d, l, 2), _layer_spec(gainf, l), _layer_spec(w_in, l), _layer_spec(w_out, l),
                  _const_spec((1, d))],
        out_specs=out_spec,
        out_shape=out_shape,
        input_output_aliases=aliases,
        scratch_shapes=[pltpu.VMEM((tm + 2 * HALO, pw), F32), pltpu.VMEM((2, tm, d), F32)],
        compiler_params=_params(("arbitrary",)),
        name="merge_ffn",
    )(xa, attn_o, o_f, o_b, z, pin, pin, pin, gates, mods, gnorm, pool_w, pool_scale, wba, wbg, wbp, wout,
      mods, gainf, w_in, w_out, fgain)


def _rope_tables(seq, ctx_len):
    quarter = ATTN_HEAD_DIM // 4
    rows = seq // GRID_W
    inv = ROPE_THETA ** (-jnp.arange(quarter, dtype=F32) / quarter)

    def axis_tables(n):
        ang = jnp.arange(n, dtype=F32)[:, None] * inv[None, :]
        c, s = jnp.cos(ang), jnp.sin(ang)
        return jnp.concatenate([c, c], axis=1), jnp.concatenate([-s, s], axis=1)

    cr, sr = axis_tables(rows)
    cc, sc = axis_tables(GRID_W)

    def full(tr, tc):
        t = jnp.concatenate([jnp.broadcast_to(tr[:, None, :], (rows, GRID_W, 2 * quarter)),
                             jnp.broadcast_to(tc[None, :, :], (rows, GRID_W, 2 * quarter))], axis=-1)
        t = t.reshape(seq, ATTN_HEAD_DIM)
        return jnp.concatenate([t] * (LANES // ATTN_HEAD_DIM), axis=1)

    cos = jnp.concatenate([jnp.ones((ctx_len, LANES), F32), full(cr, cc)], axis=0)
    sin = jnp.concatenate([jnp.zeros((ctx_len, LANES), F32), full(sr, sc)], axis=0)
    return cos, sin


PACK_TILES = 17


def _pack_kernel(*refs, short_tile, short_rows):
    out_ref = refs[-1]
    window = jnp.concatenate([r[...] for r in refs[:-1]], axis=0)
    rows = lax.broadcasted_iota(jnp.int32, (LANES, window.shape[1]), 0)
    for s in range(PACK_TILES):
        t = PACK_TILES * pl.program_id(1) + s
        own = window[(s + 1) * LANES:(s + 2) * LANES]
        shifted = window[s * LANES + short_rows:(s + 1) * LANES + short_rows]
        src = jnp.where(t < short_tile, own, jnp.where(t == short_tile, jnp.where(rows < short_rows, own, 0.0), shifted))
        out_ref[:, s * LANES:(s + 1) * LANES] = jnp.transpose(src).astype(BF16)


def _pack_w_in(w_in):
    depth, d, n = w_in.shape
    offs, n_packed = _sec_offsets()
    (short_off, _), short_rows = offs["ab"], _SEC_SRC[[name for name, _ in _SEC].index("ab")]
    assert short_off % LANES == 0 and sum(w != s for (_, w), s in zip(_SEC, _SEC_SRC)) == 1
    assert n_packed % (PACK_TILES * LANES) == 0
    short_tile = short_off // LANES
    last = (n - 1) // LANES

    def src_block(k):
        return pl.BlockSpec((None, LANES, d), lambda l, j: (l, jnp.clip(PACK_TILES * j - 1 + k, 0, last), 0))

    kern = functools.partial(_pack_kernel, short_tile=short_tile, short_rows=short_rows)
    wt = jnp.swapaxes(w_in, 1, 2)
    n_src = PACK_TILES + 1
    return pl.pallas_call(
        kern,
        grid=(depth, n_packed // (PACK_TILES * LANES)),
        in_specs=[src_block(k) for k in range(n_src)],
        out_specs=pl.BlockSpec((None, d, PACK_TILES * LANES), lambda l, j: (l, 0, j)),
        out_shape=jax.ShapeDtypeStruct((depth, d, n_packed), BF16),
        compiler_params=_params(("parallel", "parallel")),
        name="pack_w_in",
    )(*([wt] * n_src))


def kernel(x, c, ctx, c_ctx, w_ada, b_ada, norm_ffn1, w_ffn1_in, w_ffn1_out, norm_mix, w_in, attn_sink,
           gdn_conv, gdn_a_log, gdn_dt_bias, gdn_norm, pool_w, pool_scale, w_branch_attn, w_branch_gdn,
           w_branch_pool, w_out, norm_ffn2, w_ffn2_in, w_ffn2_out, final_norm):
    batch, seq, d = x.shape
    assert batch == 1, "single-sequence kernel"
    ctx_len = ctx.shape[1]
    depth = w_ada.shape[0]
    tm = ctx_len
    assert tm % (SCAN_CHUNKS * GDN_CHUNK) == 0 and seq % tm == 0 and seq % GRID_W == 0

    c_cols = jnp.zeros((d, LANES), F32).at[:, 0].set(c[0]).at[:, 1].set(c_ctx)
    mods = _ada_call(c_cols, w_ada, b_ada)
    cos, sin = _rope_tables(seq, ctx_len)

    bf = lambda w: w.astype(BF16)
    unit = lambda p: p[:, None, :]
    w1i, w1o, w2i, w2o = bf(w_ffn1_in), bf(w_ffn1_out), bf(w_ffn2_in), bf(w_ffn2_out)
    wproj = _pack_w_in(w_in)
    wba, wbg, wbp, wo, pw = bf(w_branch_attn), bf(w_branch_gdn), bf(w_branch_pool), bf(w_out), bf(pool_w)
    n1, nm, n2, gn, psc = unit(norm_ffn1), unit(norm_mix), unit(norm_ffn2), unit(gdn_norm), unit(pool_scale)
    abp = jnp.stack([gdn_dt_bias.reshape(depth, -1), gdn_a_log.reshape(depth, -1)], axis=1)
    abp = jnp.pad(abp, ((0, 0), (0, 6), (0, LANES - 2 * GDN_HEADS)))

    for l in range(depth):
        xa, q, k, v, gqkv, z, ab, pin, gates = _ffn_proj_call(
            x[0] if l == 0 else xa, ctx[0] if l == 0 else None, mods, n1, w1i, w1o, nm, wproj, cos, sin,
            l=l, tm=tm, ctx_len=ctx_len)
        u, w, qd, a2, gt = _gdn_prep_call(gqkv, ab, gdn_conv, abp, l=l, tg=tm, ctx_len=ctx_len)
        attn_o, o_f, o_b = _attn_scan_call(attn_sink, q, k, v, u, w, qd, a2, gt, l=l, ctx_len=ctx_len)
        xa = _merge_ffn_call(xa, attn_o, o_f, o_b, z, pin, gates, mods, gn, pw, psc, wba, wbg, wbp, wo,
                             n2, w2i, w2o, final_norm[None], l=l, tm=tm, ctx_len=ctx_len, final=l == depth - 1)
    return xa[None]
```

```python
import functools

import jax
import jax.numpy as jnp
from jax import lax
from jax.experimental import pallas as pl
from jax.experimental.pallas import tpu as pltpu

F32 = jnp.float32
BF16 = jnp.bfloat16
HIGHEST = lax.Precision.HIGHEST

GRID_W = 64
ATTN_HEADS = 8
ATTN_KV_HEADS = 2
ATTN_HEAD_DIM = 64
ATTN_BLOCK = 128
ATTN_STEP_BLOCKS = 2
ROPE_THETA = 10000.0
GDN_HEADS = 4
GDN_HEAD_DIM = 128
GDN_W = GDN_HEADS * GDN_HEAD_DIM
GDN_CONV = 5
GDN_CHUNK = 128
GDN_BASE = 16
SCAN_CHUNKS = 2
POOL_SIZES = (2, 4, 8, 16)
POOL_GROUP = 128
POOL_WIDTH = POOL_GROUP * len(POOL_SIZES)
N_MOD = 9
N_BRANCH = 3
EPS = 1e-6

LANES = 128
HALO = 8
VMEM_LIMIT = 56 * 1024 * 1024
COL_CHUNK = 512


def _dot(a, b, precision=None):
    return jnp.dot(a, b, preferred_element_type=F32, precision=precision)


def _dot_nt(a, b):
    return lax.dot_general(a, b, (((1,), (1,)), ((), ())), preferred_element_type=F32)


def _silu(x):
    return x * jax.nn.sigmoid(x)


def _params(sem):
    return pltpu.CompilerParams(dimension_semantics=sem, vmem_limit_bytes=VMEM_LIMIT)


def _const_spec(shape):
    nd = len(shape)
    return pl.BlockSpec(shape, lambda *_: (0,) * nd, pipeline_mode=pl.Buffered(1))


def _layer_spec(arr, l):
    nd = arr.ndim - 1
    return pl.BlockSpec((None,) + arr.shape[1:], lambda *_: (l,) + (0,) * nd, pipeline_mode=pl.Buffered(1))


def _mod_spec(d, l, j):
    return pl.BlockSpec((None, 8, 3 * d), lambda *_: (l, 0, j), pipeline_mode=pl.Buffered(1))


def _norm_mod(x, gain, mod_ref, rows, ctx_len):
    d = x.shape[1]
    y = x * lax.rsqrt(jnp.mean(x * x, axis=-1, keepdims=True) + EPS) * gain
    is_ctx = rows < ctx_len
    shift = jnp.where(is_ctx, mod_ref[1:2, 0:d], mod_ref[0:1, 0:d])
    scale = jnp.where(is_ctx, mod_ref[1:2, d:2 * d], mod_ref[0:1, d:2 * d])
    return y * (1.0 + scale) + shift


def _mod_gate(mod_ref, rows, ctx_len, d):
    return jnp.where(rows < ctx_len, mod_ref[1:2, 2 * d:3 * d], mod_ref[0:1, 2 * d:3 * d])


def _row_ids(row0, n):
    return row0 + lax.broadcasted_iota(jnp.int32, (n, 1), 0)


def _ada_kernel(c_ref, w_ref, b_ref, out_ref):
    w = w_ref[...]
    rows = [jnp.sum(w * _silu(c_ref[:, r:r + 1]), axis=0, keepdims=True) for r in range(2)]
    out_ref[...] = jnp.concatenate(rows + [jnp.zeros((6, w.shape[1]), F32)], axis=0) + b_ref[...]


def _ada_call(c_cols, w_ada, b_ada):
    depth, d, n = w_ada.shape
    tn = n // 8
    return pl.pallas_call(
        _ada_kernel,
        grid=(depth, n // tn),
        in_specs=[pl.BlockSpec((d, LANES), lambda l, j: (0, 0)),
                  pl.BlockSpec((None, d, tn), lambda l, j: (l, 0, j)),
                  pl.BlockSpec((None, 1, tn), lambda l, j: (l, 0, j))],
        out_specs=pl.BlockSpec((None, 8, tn), lambda l, j: (l, 0, j)),
        out_shape=jax.ShapeDtypeStruct((depth, 8, n), F32),
        compiler_params=_params(("parallel", "parallel")),
        name="ada_mod",
    )(c_cols, w_ada, b_ada.reshape(depth, 1, n))


def _swiglu_half_step(x, rows, mod_ref, gain_ref, win_ref, wout_ref, *, ctx_len, d_ff, fc):
    hb = _norm_mod(x, gain_ref[...], mod_ref, rows, ctx_len).astype(BF16)
    acc = jnp.zeros(x.shape, F32)
    for j in range(d_ff // fc):
        gt = _dot(hb, win_ref[:, j * fc:(j + 1) * fc])
        up = _dot(hb, win_ref[:, d_ff + j * fc:d_ff + (j + 1) * fc])
        acc = acc + _dot((_silu(gt) * up).astype(BF16), wout_ref[j * fc:(j + 1) * fc, :])
    return x + (0.5 * _mod_gate(mod_ref, rows, ctx_len, x.shape[1])) * acc


_SEC = (("q", 512), ("k", 128), ("v", 128), ("gqkv", 3 * GDN_W), ("z", GDN_W), ("ab", LANES),
        ("pin", POOL_WIDTH), ("gl", N_BRANCH * 1024))
_SEC_SRC = (512, 128, 128, 3 * GDN_W, GDN_W, 4 * GDN_HEADS, POOL_WIDTH, N_BRANCH * 1024)


def _sec_offsets():
    offs, o = {}, 0
    for name, width in _SEC:
        offs[name] = (o, width)
        o += width
    return offs, o


def _rope(t, cos, sin):
    n = t.shape[-1]
    lane = lax.broadcasted_iota(jnp.int32, t.shape, 1)
    swapped = jnp.where(lane % 32 < 16, pltpu.roll(t, n - 16, 1), pltpu.roll(t, 16, 1))
    reps = n // LANES
    c = jnp.concatenate([cos] * reps, axis=1) if reps > 1 else cos
    s = jnp.concatenate([sin] * reps, axis=1) if reps > 1 else sin
    return t * c + swapped * s


def _dup_halves(t):
    lo = lax.broadcasted_iota(jnp.int32, t.shape, 1) < LANES // 2
    r = pltpu.roll(t, LANES // 2, 1)
    return jnp.concatenate([jnp.where(lo, t, r), jnp.where(lo, r, t)], axis=1)


def _ffn_proj_kernel(*refs, tm, ctx_len, d_ff, fc, first_layer):
    if first_layer:
        ctx_ref, refs = refs[0], refs[1:]
    (x_ref, modf_ref, gainf_ref, win_ref, wout_ref, modp_ref, gainp_ref, w_ref, cos_ref, sin_ref,
     xo_ref, q_ref, k_ref, v_ref, gqkv_ref, z_ref, ab_ref, pin_ref, gate_ref) = refs
    offs, _ = _sec_offsets()
    i = pl.program_id(0)
    rows = _row_ids(i * tm, tm)
    x = x_ref[...]
    if first_layer:
        x = jnp.where(i == 0, ctx_ref[...], x)
    x = _swiglu_half_step(x, rows, modf_ref, gainf_ref, win_ref, wout_ref, ctx_len=ctx_len, d_ff=d_ff, fc=fc)
    xo_ref[...] = x
    hb = _norm_mod(x, gainp_ref[...], modp_ref, rows, ctx_len).astype(BF16)

    def sec(name, out_ref, fn):
        o, width = offs[name]
        step = min(width, COL_CHUNK)
        for a in range(0, width, step):
            out_ref[:, a:a + step] = fn(_dot(hb, w_ref[:, o + a:o + a + step])).astype(out_ref.dtype)

    cos, sin = cos_ref[...], sin_ref[...]
    sec("q", q_ref, lambda t: _rope(t, cos, sin) * (ATTN_HEAD_DIM ** -0.5))
    o, width = offs["k"]
    k_ref[...] = _dup_halves(_rope(_dot(hb, w_ref[:, o:o + width]), cos, sin)).astype(BF16)
    o, width = offs["v"]
    v_ref[...] = _dup_halves(_dot(hb, w_ref[:, o:o + width])).astype(BF16)
    ident = lambda t: t
    sec("gqkv", gqkv_ref, ident)
    sec("z", z_ref, ident)
    sec("ab", ab_ref, ident)
    sec("pin", pin_ref, ident)
    sec("gl", gate_ref, jax.nn.sigmoid)


def _ffn_proj_call(x_in, ctx, mods, gainf, w_in, w_out, gainp, w, cos, sin, *, l, tm, ctx_len):
    first_layer = ctx is not None
    d = x_in.shape[1]
    la = x_in.shape[0] + (ctx_len if first_layer else 0)
    offs, _ = _sec_offsets()
    d_ff = w_out.shape[1]
    fc = d_ff
    kern = functools.partial(_ffn_proj_kernel, tm=tm, ctx_len=ctx_len, d_ff=d_ff, fc=fc, first_layer=first_layer)
    row = lambda width: pl.BlockSpec((tm, width), lambda i: (i, 0))
    if first_layer:
        assert tm == ctx_len
        lead_specs = [_const_spec((tm, d)), pl.BlockSpec((tm, d), lambda i: (jnp.maximum(i - 1, 0), 0))]
        lead_args = (ctx, x_in)
        aliases = {}
    else:
        lead_specs = [row(d)]
        lead_args = (x_in,)
        aliases = {0: 0}
    outs = ((d, F32), (512, BF16), (2 * LANES, BF16), (2 * LANES, BF16), (3 * GDN_W, F32), (GDN_W, BF16),
            (LANES, F32), (POOL_WIDTH, F32), (offs["gl"][1], BF16))
    return pl.pallas_call(
        kern,
        grid=(la // tm,),
        in_specs=lead_specs + [_mod_spec(d, l, 0), _layer_spec(gainf, l), _layer_spec(w_in, l),
                               _layer_spec(w_out, l), _mod_spec(d, l, 1), _layer_spec(gainp, l),
                               _layer_spec(w, l), row(LANES), row(LANES)],
        out_specs=[row(width) for width, _ in outs],
        out_shape=[jax.ShapeDtypeStruct((la, width), dt) for width, dt in outs],
        input_output_aliases=aliases,
        compiler_params=_params(("parallel",)),
        name="ffn_proj",
    )(*lead_args, mods, gainf, w_in, w_out, mods, gainp, w, cos, sin)


def _attn_stages(sink_ref, q_ref, kp_ref, kc_ref, kn_ref, vp_ref, vc_ref, vn_ref, kx_ref, vx_ref,
                 o_ref, *, l, ctx_blocks, n_blocks):
    b0 = ATTN_STEP_BLOCKS * pl.program_id(0)
    blk = ATTN_BLOCK
    is_lat = b0 >= ctx_blocks
    ri = lax.broadcasted_iota(jnp.int32, (blk, blk), 0)
    ci = lax.broadcasted_iota(jnp.int32, (blk, blk), 1)
    ninf = -jnp.inf
    lo = lax.broadcasted_iota(jnp.int32, (blk, LANES), 1) < LANES // 2
    pairs = ATTN_HEADS // ATTN_KV_HEADS // 2
    groups = [(sb, hk) for sb in range(ATTN_STEP_BLOCKS) for hk in range(ATTN_KV_HEADS)]

    def window(ref_p, ref_c, ref_n, sb, ks):
        blocks = [ref_p[:, ks]] + [ref_c[j * blk:(j + 1) * blk, ks] for j in range(ATTN_STEP_BLOCKS)] + [ref_n[:, ks]]
        return jnp.concatenate(blocks[sb:sb + 3], axis=0)

    bias = []
    for sb in range(ATTN_STEP_BLOCKS):
        prev_ok = b0 + sb - 1 >= ctx_blocks
        next_ok = jnp.logical_and(is_lat, b0 + sb + 1 < n_blocks)
        bias.append(jnp.concatenate(
            [jnp.where(jnp.logical_and(ci >= ri, prev_ok), 0.0, ninf),
             jnp.where(jnp.broadcast_to(is_lat, (blk, blk)), 0.0, ninf),
             jnp.where(jnp.logical_and(ci <= ri, next_ok), 0.0, ninf)], axis=1))

    scores, values, sinks = [], [], []
    for sb, hk in groups:
        ks = slice(hk * LANES, (hk + 1) * LANES)
        rows = slice(sb * blk, (sb + 1) * blk)
        q_parts, sk = [], []
        for j in range(pairs):
            col = (hk * pairs + j) * LANES
            qp = q_ref[rows, col:col + LANES]
            q_parts += [jnp.where(lo, qp, jnp.zeros_like(qp)), jnp.where(lo, jnp.zeros_like(qp), qp)]
            sk += [sink_ref[l, (hk * pairs + j) * 2], sink_ref[l, (hk * pairs + j) * 2 + 1]]
        q4 = jnp.concatenate(q_parts, axis=0)
        scores.append((_dot_nt(q4, window(kp_ref, kc_ref, kn_ref, sb, ks)), _dot_nt(q4, kx_ref[:, ks])))
        values.append((window(vp_ref, vc_ref, vn_ref, sb, ks), vx_ref[:, ks]))
        sinks.append(sk)
        yield
    probs = []
    for g, (sb, hk) in enumerate(groups):
        s_win, s_ctx = scores[g]
        p_win, p_ctx, inv = [], [], []
        for j, sk in enumerate(sinks[g]):
            rs = slice(j * blk, (j + 1) * blk)
            sw = s_win[rs] + bias[sb]
            sx = s_ctx[rs]
            m = jnp.maximum(jnp.maximum(jnp.max(sw, axis=-1, keepdims=True),
                                        jnp.max(sx, axis=-1, keepdims=True)), sk)
            pw = jnp.exp(sw - m)
            px = jnp.exp(sx - m)
            denom = jnp.sum(pw, axis=-1, keepdims=True) + jnp.sum(px, axis=-1, keepdims=True) + jnp.exp(sk - m)
            p_win.append(pw.astype(BF16))
            p_ctx.append(px.astype(BF16))
            inv.append(1.0 / denom)
        probs.append((jnp.concatenate(p_win, axis=0), jnp.concatenate(p_ctx, axis=0), inv))
    for g, (sb, hk) in enumerate(groups):
        p_win, p_ctx, inv = probs[g]
        vw, vx = values[g]
        r = _dot(p_win, vw) + _dot(p_ctx, vx)
        for j in range(pairs):
            r0 = r[(2 * j) * blk:(2 * j + 1) * blk] * inv[2 * j]
            r1 = r[(2 * j + 1) * blk:(2 * j + 2) * blk] * inv[2 * j + 1]
            col = (hk * pairs + j) * LANES
            o_ref[sb * blk:(sb + 1) * blk, col:col + LANES] = jnp.where(lo, r0, r1).astype(BF16)
        yield


def _tri_inverse_masks(n):
    ri = lax.broadcasted_iota(jnp.int32, (n, n), 0)
    ci = lax.broadcasted_iota(jnp.int32, (n, n), 1)
    eye = (ri == ci).astype(F32)
    same = lambda s: (ri // s) == (ci // s)
    offs, s = [], GDN_BASE
    while s < n:
        offs.append(jnp.logical_and(same(2 * s), jnp.logical_not(same(s))))
        s *= 2
    return eye, same(GDN_BASE), offs


def _interleave(*stage_generators):
    stages = list(stage_generators)
    while stages:
        for g in list(stages):
            try:
                next(g)
            except StopIteration:
                stages.remove(g)


def _tri_inverse(a_list, masks, between=None):
    eye, diag, offs = masks
    a0 = [jnp.where(diag, a, 0.0) for a in a_list]
    a0b = [x.astype(BF16) for x in a0]
    t = [eye - x for x in a0]
    p = [_dot(x, x) for x in a0b]
    s = 2
    while True:
        pb = [x.astype(BF16) for x in p]
        t = [ti + _dot(ti.astype(BF16), pi) for ti, pi in zip(t, pb)]
        s *= 2
        if s >= GDN_BASE:
            break
        p = [_dot(x, x) for x in pb]
    if between is not None:
        between()
    for off in offs:
        tb = [x.astype(BF16) for x in t]
        mid = [_dot(ti, jnp.where(off, a, 0.0).astype(BF16)).astype(BF16) for ti, a in zip(tb, a_list)]
        t = [ti - _dot(mi, tbi) for ti, mi, tbi in zip(t, mid, tb)]
    return t


def _gdn_prep_kernel(prev_ref, cur_ref, next_ref, ab_ref, convw_ref, abp_ref,
                     u_ref, w_ref, qd_ref, a2_ref, gt_ref, xs_ref, a_s, rhs_s, *, tg, ctx_len, total, ntiles):
    i = pl.program_id(0)
    dk = GDN_HEAD_DIM
    c_len = GDN_CHUNK
    heads = range(GDN_HEADS)
    chunks = range(tg // c_len)
    units = [(c, h, d) for c in chunks for h in heads for d in range(2)]
    hsl = [slice(h * dk, (h + 1) * dk) for h in heads]

    @pl.when(i == 0)
    def _():
        a_s[1] = jnp.zeros(a_s.shape[1:], F32)
        rhs_s[1] = jnp.zeros(rhs_s.shape[1:], BF16)

    slot = i % 2
    row0 = jnp.minimum(i, ntiles - 1) * tg
    first = jnp.logical_or(row0 == 0, row0 == ctx_len)
    last = jnp.logical_or(row0 + tg == ctx_len, row0 + tg == total)
    xs_ref[0:HALO, :] = jnp.where(first, 0.0, prev_ref[...])
    xs_ref[HALO:HALO + tg, :] = cur_ref[...]
    xs_ref[HALO + tg:HALO + tg + HALO, :] = jnp.where(last, 0.0, next_ref[...])
    pad = (GDN_CONV - 1) // 2

    def conv_silu(col, r0):
        ls = slice(col * dk, (col + 1) * dk)
        acc = xs_ref[pl.ds(HALO - pad + r0, c_len), ls] * convw_ref[0:1, ls]
        for j in range(1, GDN_CONV):
            acc = acc + xs_ref[pl.ds(HALO - pad + j + r0, c_len), ls] * convw_ref[j:j + 1, ls]
        return _silu(acc)

    def l2n(t):
        return t * lax.rsqrt(jnp.sum(t * t, axis=-1, keepdims=True) + EPS)

    ri = lax.broadcasted_iota(jnp.int32, (c_len, c_len), 0)
    ci = lax.broadcasted_iota(jnp.int32, (c_len, c_len), 1)
    strict = (ci < ri, ci > ri)
    incl = (ci <= ri, ci >= ri)
    tri = (incl[0].astype(F32), incl[1].astype(F32))
    last_row = (c_len - 1, 0)
    masks = _tri_inverse_masks(c_len)
    f = {}

    def front_matmuls():
        g_all = [-jnp.exp(abp_ref[1:2, :]) * jax.nn.softplus(ab_ref[c * c_len:(c + 1) * c_len, :] + abp_ref[0:1, :])
                 for c in chunks]
        f["b_all"] = [jax.nn.sigmoid(ab_ref[c * c_len:(c + 1) * c_len, :]) for c in chunks]
        f["gcs"] = [[_dot(tri[d], g_all[c], HIGHEST) for d in range(2)] for c in chunks]
        f["gct"] = [[jnp.transpose(x) for x in f["gcs"][c]] for c in chunks]
        pairs = [(c, h) for c in chunks for h in heads]
        f["qs"] = {ch: l2n(conv_silu(ch[1], ch[0] * c_len)) * (dk ** -0.5) for ch in pairs}
        f["ks"] = {ch: l2n(conv_silu(GDN_HEADS + ch[1], ch[0] * c_len)) for ch in pairs}
        f["vs"] = {ch: conv_silu(2 * GDN_HEADS + ch[1], ch[0] * c_len) for ch in pairs}
        f["kts"] = {ch: jnp.transpose(f["ks"][ch]) for ch in pairs}
        ktb = {ch: f["kts"][ch].astype(BF16) for ch in pairs}
        f["kk"] = {ch: _dot(f["ks"][ch].astype(BF16), ktb[ch]) for ch in pairs}
        f["qkt"] = {ch: _dot(f["qs"][ch].astype(BF16), ktb[ch]) for ch in pairs}

    a_prev = [a_s[1 - slot, n] for n in range(len(units))]
    t_prev = _tri_inverse(a_prev, masks, between=front_matmuls)
    uw = [_dot(t_.astype(BF16), rhs_s[1 - slot, n]) for n, t_ in enumerate(t_prev)]
    for n, (c, h, d) in enumerate(units):
        rs = slice(c * c_len, (c + 1) * c_len)
        u_ref[d, rs, hsl[h]] = uw[n][:, 0:dk].astype(BF16)
        w_ref[d, rs, hsl[h]] = uw[n][:, dk:2 * dk].astype(BF16)

    for n, (c, h, d) in enumerate(units):
        rs = slice(c * c_len, (c + 1) * c_len)
        lane = d * GDN_HEADS + h
        gc = f["gcs"][c][d][:, lane:lane + 1]
        gr = f["gct"][c][d][lane:lane + 1, :]
        beta = f["b_all"][c][:, 2 * GDN_HEADS + lane:2 * GDN_HEADS + lane + 1]
        e = jnp.exp(jnp.where(incl[d], gc - gr, -jnp.inf))
        egc = jnp.exp(gc)
        g_last = gc[last_row[d]:last_row[d] + 1, :]
        a_s[slot, n] = jnp.where(strict[d], beta * f["kk"][(c, h)] * e, 0.0)
        rhs_s[slot, n, :, 0:dk] = (beta * f["vs"][(c, h)]).astype(BF16)
        rhs_s[slot, n, :, dk:2 * dk] = (beta * egc * f["ks"][(c, h)]).astype(BF16)
        qd_ref[d, rs, hsl[h]] = (f["qs"][(c, h)] * egc).astype(BF16)
        a2_ref[d, c, 0:c_len, hsl[h]] = (f["qkt"][(c, h)] * e).astype(BF16)
        a2_ref[d, c, c_len:2 * c_len, hsl[h]] = (f["kts"][(c, h)] * jnp.exp(g_last - gr)).astype(BF16)
        gt_ref[d, c, h:h + 1, :] = jnp.broadcast_to(jnp.exp(g_last), (1, LANES))


def _gdn_prep_call(gqkv, ab, conv_w, abp, *, l, tg, ctx_len):
    la, wq = gqkv.shape
    c_len = GDN_CHUNK
    nch = la // c_len
    cpt = tg // c_len
    hb = tg // HALO
    nhalo = la // HALO
    ntiles = la // tg
    n_units = cpt * GDN_HEADS * 2
    kern = functools.partial(_gdn_prep_kernel, tg=tg, ctx_len=ctx_len, total=la, ntiles=ntiles)
    cur = lambda i: jnp.minimum(i, ntiles - 1)
    prv = lambda i: jnp.maximum(i - 1, 0)
    rows_cur = pl.BlockSpec((2, tg, GDN_W), lambda i: (0, cur(i), 0))
    rows_prv = pl.BlockSpec((2, tg, GDN_W), lambda i: (0, prv(i), 0))
    return pl.pallas_call(
        kern,
        grid=(ntiles + 1,),
        in_specs=[pl.BlockSpec((HALO, wq), lambda i: (jnp.maximum(cur(i) * hb - 1, 0), 0)),
                  pl.BlockSpec((tg, wq), lambda i: (cur(i), 0)),
                  pl.BlockSpec((HALO, wq), lambda i: (jnp.minimum((cur(i) + 1) * hb, nhalo - 1), 0)),
                  pl.BlockSpec((tg, LANES), lambda i: (cur(i), 0)),
                  _layer_spec(conv_w, l), _layer_spec(abp, l)],
        out_specs=[rows_prv, rows_prv, rows_cur,
                   pl.BlockSpec((2, cpt, 2 * c_len, GDN_W), lambda i: (0, cur(i), 0, 0)),
                   pl.BlockSpec((2, cpt, GDN_HEADS, LANES), lambda i: (0, cur(i), 0, 0))],
        out_shape=[jax.ShapeDtypeStruct((2, la, GDN_W), BF16),
                   jax.ShapeDtypeStruct((2, la, GDN_W), BF16),
                   jax.ShapeDtypeStruct((2, la, GDN_W), BF16),
                   jax.ShapeDtypeStruct((2, nch, 2 * c_len, GDN_W), BF16),
                   jax.ShapeDtypeStruct((2, nch, GDN_HEADS, LANES), F32)],
        scratch_shapes=[pltpu.VMEM((tg + 2 * HALO, wq), F32),
                        pltpu.VMEM((2, n_units, c_len, c_len), F32),
                        pltpu.VMEM((2, n_units, c_len, 2 * GDN_HEAD_DIM), BF16)],
        compiler_params=_params(("arbitrary",)),
        name="gdn_prep",
    )(gqkv, gqkv, gqkv, ab, conv_w, abp)


def _scan_stages(uf, ub, wf, wb, qdf, qdb, a2f, a2b, gtf, gtb, of_ref, ob_ref, s_ref):
    dk = GDN_HEAD_DIM
    c_len = GDN_CHUNK
    dirs = ((uf, wf, qdf, a2f, gtf, of_ref), (ub, wb, qdb, a2b, gtb, ob_ref))
    chains = [(d, h) for d in range(2) for h in range(GDN_HEADS)]
    hs = [slice(h * dk, (h + 1) * dk) for h in range(GDN_HEADS)]
    s = [s_ref[d * GDN_HEADS + h] for d, h in chains]
    for step in range(SCAN_CHUNKS):
        sub = (step, SCAN_CHUNKS - 1 - step)
        rows = [slice(sub[d] * c_len, (sub[d] + 1) * c_len) for d in range(2)]
        r1 = [_dot(jnp.concatenate([dirs[d][1][rows[d], hs[h]], dirs[d][2][rows[d], hs[h]]], axis=0),
                   s_.astype(BF16)) for (d, h), s_ in zip(chains, s)]
        yield
        vb = [(dirs[d][0][rows[d], hs[h]].astype(F32) - r[0:c_len]).astype(BF16) for (d, h), r in zip(chains, r1)]
        r2 = [_dot(dirs[d][3][sub[d], :, hs[h]], v_) for (d, h), v_ in zip(chains, vb)]
        yield
        for (d, h), ra, rb in zip(chains, r1, r2):
            dirs[d][5][rows[d], hs[h]] = (ra[c_len:2 * c_len] + rb[0:c_len]).astype(BF16)
        s = [s_ * dirs[d][4][sub[d], h:h + 1, :] + rb[c_len:2 * c_len] for (d, h), s_, rb in zip(chains, s, r2)]
    for (d, h), s_ in zip(chains, s):
        s_ref[d * GDN_HEADS + h] = s_


def _attn_scan_kernel(*refs, l, ctx_blocks, n_blocks):
    attn_in, scan_in = refs[0:10], refs[10:20]
    o_attn, of_ref, ob_ref, s_ref = refs[20:24]

    @pl.when(pl.program_id(0) == 0)
    def _():
        s_ref[...] = jnp.zeros(s_ref.shape, F32)

    _interleave(_scan_stages(*scan_in, of_ref, ob_ref, s_ref),
                _attn_stages(*attn_in, o_attn, l=l, ctx_blocks=ctx_blocks, n_blocks=n_blocks))


def _attn_scan_call(sink, q, k, v, u, w, qd, a2, gt, *, l, ctx_len):
    la = q.shape[0]
    blk = ATTN_BLOCK
    step = ATTN_STEP_BLOCKS
    nb = la // blk
    cb = ctx_len // blk
    rows = step * blk
    assert rows == SCAN_CHUNKS * GDN_CHUNK and la % rows == 0 and ctx_len % rows == 0
    nsteps = la // rows
    ncs = ctx_len // rows
    kern = functools.partial(_attn_scan_kernel, l=l, ctx_blocks=cb, n_blocks=nb)
    kvw = k.shape[1]
    prev = pl.BlockSpec((blk, kvw), lambda p: (jnp.maximum(step * p - 1, 0), 0))
    cur = pl.BlockSpec((rows, kvw), lambda p: (p, 0))
    nxt = pl.BlockSpec((blk, kvw), lambda p: (jnp.minimum(step * (p + 1), nb - 1), 0))
    cx = pl.BlockSpec((ctx_len, kvw), lambda p: (0, 0))

    def bwd(n):
        return jnp.where(n < ncs, ncs - 1 - n, nsteps + ncs - 1 - n)

    def both(make):
        return [make(0, lambda n: n), make(1, bwd)]

    r_specs = both(lambda d, f: pl.BlockSpec((None, rows, GDN_W), lambda n: (d, f(n), 0)))
    a_specs = both(lambda d, f: pl.BlockSpec((None, SCAN_CHUNKS, 2 * GDN_CHUNK, GDN_W), lambda n: (d, f(n), 0, 0)))
    g_specs = both(lambda d, f: pl.BlockSpec((None, SCAN_CHUNKS, GDN_HEADS, LANES), lambda n: (d, f(n), 0, 0)))
    return pl.pallas_call(
        kern,
        grid=(nsteps,),
        in_specs=[pl.BlockSpec(memory_space=pltpu.SMEM),
                  pl.BlockSpec((rows, q.shape[1]), lambda p: (p, 0)),
                  prev, cur, nxt, prev, cur, nxt, cx, cx] + r_specs + r_specs + r_specs + a_specs + g_specs,
        out_specs=[pl.BlockSpec((rows, q.shape[1]), lambda p: (p, 0)),
                   pl.BlockSpec((rows, GDN_W), lambda n: (n, 0)),
                   pl.BlockSpec((rows, GDN_W), lambda n: (bwd(n), 0))],
        out_shape=[jax.ShapeDtypeStruct(q.shape, BF16),
                   jax.ShapeDtypeStruct((la, GDN_W), BF16), jax.ShapeDtypeStruct((la, GDN_W), BF16)],
        scratch_shapes=[pltpu.VMEM((2 * GDN_HEADS, GDN_HEAD_DIM, GDN_HEAD_DIM), F32)],
        compiler_params=_params(("arbitrary",)),
        name="attn_scan",
    )(sink, q, k, k, k, v, v, v, k, v, u, u, w, w, qd, qd, a2, a2, gt, gt)


def _merge_ffn_kernel(x_ref, attn_ref, of_ref, ob_ref, z_ref, pprev_ref, pcur_ref, pnext_ref, gate_ref,
                      modm_ref, gnorm_ref, poolw_ref, pscale_ref, wba_ref, wbg_ref, wbp_ref, wout_ref,
                      modf_ref, gainf_ref, win_ref, wo2_ref, fgain_ref, out_ref, ps_ref, xn_ref,
                      *, tm, ctx_len, total, ntiles, d_ff, fc, final):
    i = pl.program_id(0)
    d = x_ref.shape[1]
    slot = i % 2

    @pl.when(i == 0)
    def _():
        xn_ref[1] = jnp.zeros(xn_ref.shape[1:], F32)

    row0 = jnp.minimum(i, ntiles - 1) * tm
    first = jnp.logical_or(row0 == 0, row0 == ctx_len)
    last = jnp.logical_or(row0 + tm == ctx_len, row0 + tm == total)
    ps_ref[0:HALO, :] = jnp.where(first, 0.0, pprev_ref[...])
    ps_ref[HALO:HALO + tm, :] = pcur_ref[...]
    ps_ref[HALO + tm:HALO + tm + HALO, :] = jnp.where(last, 0.0, pnext_ref[...])
    rows = _row_ids(row0, tm)
    is_ctx = rows < ctx_len
    seg_pos = jnp.where(is_ctx, rows, rows - ctx_len)
    seg_len = jnp.where(is_ctx, ctx_len, total - ctx_len)
    m = {}

    def merge_vector_prep():
        m["pool_r"] = []
        for gi, win in enumerate(POOL_SIZES):
            ls = slice(gi * POOL_GROUP, (gi + 1) * POOL_GROUP)
            s = ps_ref[pl.ds(HALO - win // 2, tm), ls]
            for o in range(1, win):
                s = s + ps_ref[pl.ds(HALO - win // 2 + o, tm), ls]
            cnt = (jnp.minimum(seg_pos + win // 2, seg_len) - jnp.maximum(seg_pos - win // 2, 0)).astype(F32)
            m["pool_r"].append((s / cnt - pcur_ref[:, ls]).astype(BF16))
        gdn = []
        for h in range(GDN_HEADS):
            hs = slice(h * GDN_HEAD_DIM, (h + 1) * GDN_HEAD_DIM)
            o = of_ref[:, hs].astype(F32) + ob_ref[:, hs].astype(F32)
            o = o * lax.rsqrt(jnp.mean(o * o, axis=-1, keepdims=True) + EPS) * gnorm_ref[...]
            gdn.append(o * _silu(z_ref[:, hs].astype(F32)))
        m["gdn_o"] = jnp.concatenate(gdn, axis=1).astype(BF16)

    def merge_branches():
        pooled = [_dot(r, poolw_ref[gi]) * pscale_ref[:, gi * POOL_GROUP:(gi + 1) * POOL_GROUP]
                  for gi, r in enumerate(m["pool_r"])]
        pool_o = jnp.concatenate(pooled, axis=1).astype(BF16)
        m["merged"] = (gate_ref[:, 0:d].astype(F32) * _dot(attn_ref[...], wba_ref[...])
                       + gate_ref[:, d:2 * d].astype(F32) * _dot(m["gdn_o"], wbg_ref[...])
                       + gate_ref[:, 2 * d:3 * d].astype(F32) * _dot(pool_o, wbp_ref[...])).astype(BF16)

    def merge_out():
        y = _dot(m["merged"], wout_ref[...])
        xn_ref[slot] = x_ref[...] + _mod_gate(modm_ref, rows, ctx_len, d) * y

    xp = xn_ref[1 - slot]
    rows_p = _row_ids((i - 1) * tm, tm)
    hb = _norm_mod(xp, gainf_ref[...], modf_ref, rows_p, ctx_len).astype(BF16)
    acc = jnp.zeros(xp.shape, F32)
    n_chunks = d_ff // fc
    for j in range(n_chunks):
        gt = _dot(hb, win_ref[:, j * fc:(j + 1) * fc])
        up = _dot(hb, win_ref[:, d_ff + j * fc:d_ff + (j + 1) * fc])
        if j == 0:
            merge_vector_prep()
        acc = acc + _dot((_silu(gt) * up).astype(BF16), wo2_ref[j * fc:(j + 1) * fc, :])
    merge_branches()
    merge_out()
    y = xp + (0.5 * _mod_gate(modf_ref, rows_p, ctx_len, d)) * acc
    if final:
        y = y * lax.rsqrt(jnp.mean(y * y, axis=-1, keepdims=True) + EPS) * fgain_ref[...]
    out_ref[...] = y


def _merge_ffn_call(xa, attn_o, o_f, o_b, z, pin, gates, mods, gnorm, pool_w, pool_scale, wba, wbg, wbp, wout,
                    gainf, w_in, w_out, fgain, *, l, tm, ctx_len, final):
    la, d = xa.shape
    hb = tm // HALO
    nhalo = la // HALO
    pw = pin.shape[1]
    ntiles = la // tm
    d_ff = w_out.shape[1]
    fc = d_ff
    kern = functools.partial(_merge_ffn_kernel, tm=tm, ctx_len=ctx_len, total=la, ntiles=ntiles,
                             d_ff=d_ff, fc=fc, final=final)
    cur = lambda i: jnp.minimum(i, ntiles - 1)
    row = lambda width: pl.BlockSpec((tm, width), lambda i: (cur(i), 0))
    if final:
        skip = ctx_len // tm
        out_spec = pl.BlockSpec((tm, d), lambda i: (jnp.maximum(i - 1 - skip, 0), 0))
        out_shape = jax.ShapeDtypeStruct((la - ctx_len, d), F32)
        aliases = {}
    else:
        out_spec = pl.BlockSpec((tm, d), lambda i: (jnp.maximum(i - 1, 0), 0))
        out_shape = jax.ShapeDtypeStruct((la, d), F32)
        aliases = {0: 0}
    return pl.pallas_call(
        kern,
        grid=(ntiles + 1,),
        in_specs=[row(d), row(attn_o.shape[1]), row(GDN_W), row(GDN_W), row(GDN_W),
                  pl.BlockSpec((HALO, pw), lambda i: (jnp.maximum(cur(i) * hb - 1, 0), 0)),
                  row(pw),
                  pl.BlockSpec((HALO, pw), lambda i: (jnp.minimum((cur(i) + 1) * hb, nhalo - 1), 0)),
                  row(gates.shape[1]),
                  _mod_spec(d, l, 1), _layer_spec(gnorm, l), _layer_spec(pool_w, l), _layer_spec(pool_scale, l),
                  _layer_spec(wba, l), _layer_spec(wbg, l), _layer_spec(wbp, l), _layer_spec(wout, l),
                  _mod_spec(d, l, 2), _layer_spec(gainf, l), _layer_spec(w_in, l), _layer_spec(w_out, l),
                  _const_spec((1, d))],
        out_specs=out_spec,
        out_shape=out_shape,
        input_output_aliases=aliases,
        scratch_shapes=[pltpu.VMEM((tm + 2 * HALO, pw), F32), pltpu.VMEM((2, tm, d), F32)],
        compiler_params=_params(("arbitrary",)),
        name="merge_ffn",
    )(xa, attn_o, o_f, o_b, z, pin, pin, pin, gates, mods, gnorm, pool_w, pool_scale, wba, wbg, wbp, wout,
      mods, gainf, w_in, w_out, fgain)


def _rope_tables(seq, ctx_len):
    quarter = ATTN_HEAD_DIM // 4
    rows = seq // GRID_W
    inv = ROPE_THETA ** (-jnp.arange(quarter, dtype=F32) / quarter)

    def axis_tables(n):
        ang = jnp.arange(n, dtype=F32)[:, None] * inv[None, :]
        c, s = jnp.cos(ang), jnp.sin(ang)
        return jnp.concatenate([c, c], axis=1), jnp.concatenate([-s, s], axis=1)

    cr, sr = axis_tables(rows)
    cc, sc = axis_tables(GRID_W)

    def full(tr, tc):
        t = jnp.concatenate([jnp.broadcast_to(tr[:, None, :], (rows, GRID_W, 2 * quarter)),
                             jnp.broadcast_to(tc[None, :, :], (rows, GRID_W, 2 * quarter))], axis=-1)
        t = t.reshape(seq, ATTN_HEAD_DIM)
        return jnp.concatenate([t] * (LANES // ATTN_HEAD_DIM), axis=1)

    cos = jnp.concatenate([jnp.ones((ctx_len, LANES), F32), full(cr, cc)], axis=0)
    sin = jnp.concatenate([jnp.zeros((ctx_len, LANES), F32), full(sr, sc)], axis=0)
    return cos, sin


PACK_TILES = 17


def _pack_kernel(*refs, short_tile, short_rows):
    out_ref = refs[-1]
    window = jnp.concatenate([r[...] for r in refs[:-1]], axis=0)
    rows = lax.broadcasted_iota(jnp.int32, (LANES, window.shape[1]), 0)
    for s in range(PACK_TILES):
        t = PACK_TILES * pl.program_id(1) + s
        own = window[(s + 1) * LANES:(s + 2) * LANES]
        shifted = window[s * LANES + short_rows:(s + 1) * LANES + short_rows]
        src = jnp.where(t < short_tile, own, jnp.where(t == short_tile, jnp.where(rows < short_rows, own, 0.0), shifted))
        out_ref[:, s * LANES:(s + 1) * LANES] = jnp.transpose(src).astype(BF16)


def _pack_w_in(w_in):
    depth, d, n = w_in.shape
    offs, n_packed = _sec_offsets()
    (short_off, _), short_rows = offs["ab"], _SEC_SRC[[name for name, _ in _SEC].index("ab")]
    assert short_off % LANES == 0 and sum(w != s for (_, w), s in zip(_SEC, _SEC_SRC)) == 1
    assert n_packed % (PACK_TILES * LANES) == 0
    short_tile = short_off // LANES
    last = (n - 1) // LANES

    def src_block(k):
        return pl.BlockSpec((None, LANES, d), lambda l, j: (l, jnp.clip(PACK_TILES * j - 1 + k, 0, last), 0))

    kern = functools.partial(_pack_kernel, short_tile=short_tile, short_rows=short_rows)
    wt = jnp.swapaxes(w_in, 1, 2)
    n_src = PACK_TILES + 1
    return pl.pallas_call(
        kern,
        grid=(depth, n_packed // (PACK_TILES * LANES)),
        in_specs=[src_block(k) for k in range(n_src)],
        out_specs=pl.BlockSpec((None, d, PACK_TILES * LANES), lambda l, j: (l, 0, j)),
        out_shape=jax.ShapeDtypeStruct((depth, d, n_packed), BF16),
        compiler_params=_params(("parallel", "parallel")),
        name="pack_w_in",
    )(*([wt] * n_src))


def kernel(x, c, ctx, c_ctx, w_ada, b_ada, norm_ffn1, w_ffn1_in, w_ffn1_out, norm_mix, w_in, attn_sink,
           gdn_conv, gdn_a_log, gdn_dt_bias, gdn_norm, pool_w, pool_scale, w_branch_attn, w_branch_gdn,
           w_branch_pool, w_out, norm_ffn2, w_ffn2_in, w_ffn2_out, final_norm):
    batch, seq, d = x.shape
    assert batch == 1, "single-sequence kernel"
    ctx_len = ctx.shape[1]
    depth = w_ada.shape[0]
    tm = ctx_len
    assert tm % (SCAN_CHUNKS * GDN_CHUNK) == 0 and seq % tm == 0 and seq % GRID_W == 0

    c_cols = jnp.zeros((d, LANES), F32).at[:, 0].set(c[0]).at[:, 1].set(c_ctx)
    mods = _ada_call(c_cols, w_ada, b_ada)
    cos, sin = _rope_tables(seq, ctx_len)

    bf = lambda w: w.astype(BF16)
    unit = lambda p: p[:, None, :]
    w1i, w1o, w2i, w2o = bf(w_ffn1_in), bf(w_ffn1_out), bf(w_ffn2_in), bf(w_ffn2_out)
    wproj = _pack_w_in(w_in)
    wba, wbg, wbp, wo, pw = bf(w_branch_attn), bf(w_branch_gdn), bf(w_branch_pool), bf(w_out), bf(pool_w)
    n1, nm, n2, gn, psc = unit(norm_ffn1), unit(norm_mix), unit(norm_ffn2), unit(gdn_norm), unit(pool_scale)
    abp = jnp.stack([gdn_dt_bias.reshape(depth, -1), gdn_a_log.reshape(depth, -1)], axis=1)
    abp = jnp.pad(abp, ((0, 0), (0, 6), (0, LANES - 2 * GDN_HEADS)))

    for l in range(depth):
        xa, q, k, v, gqkv, z, ab, pin, gates = _ffn_proj_call(
            x[0] if l == 0 else xa, ctx[0] if l == 0 else None, mods, n1, w1i, w1o, nm, wproj, cos, sin,
            l=l, tm=tm, ctx_len=ctx_len)
        u, w, qd, a2, gt = _gdn_prep_call(gqkv, ab, gdn_conv, abp, l=l, tg=tm, ctx_len=ctx_len)
        attn_o, o_f, o_b = _attn_scan_call(attn_sink, q, k, v, u, w, qd, a2, gt, l=l, ctx_len=ctx_len)
        xa = _merge_ffn_call(xa, attn_o, o_f, o_b, z, pin, gates, mods, gn, pw, psc, wba, wbg, wbp, wo,
                             n2, w2i, w2o, final_norm[None], l=l, tm=tm, ctx_len=ctx_len, final=l == depth - 1)
    return xa[None]
```
